```python
import numpy as np
import jax
import jax.numpy as jnp
from jax import lax

D_MODEL = 2048
BATCH = 4
SEQ = 4096
DEPTH = 1

POOL_WINDOWS = (2, 4, 8, 16)
POOL_GROUPS = 4
POOL_WIDTH = D_MODEL // 2
POOL_GROUP_DIM = POOL_WIDTH // POOL_GROUPS

HEAD_DIM = 64
N_HEADS = (D_MODEL // 2) // HEAD_DIM
N_KV_GROUPS = 4
HEADS_PER_GROUP = N_HEADS // N_KV_GROUPS
Q_WIDTH = N_HEADS * HEAD_DIM
KV_WIDTH = N_KV_GROUPS * HEAD_DIM
CMP_BLOCK = 32
CMP_STRIDE = 16
CMP_HIDDEN = 256
SEL_BLOCK = 64
SEL_TOP_N = 16
WINDOW = 512
WIN_Q_BLOCK = 128
SEL_Q_BLOCK = 64
ROPE_THETA = 10000.0
FORCE_SCORE = 1e6

N_EXPERT_GROUPS = 8
EXPERTS_PER_GROUP = 8
N_EXPERTS = N_EXPERT_GROUPS * EXPERTS_PER_GROUP
EXPERT_HIDDEN = D_MODEL // 4
TOP_K = 2
EXPERT_ROW_BLOCK = 128

DN_ALPHA = (2.0 * DEPTH) ** 0.25
DN_BETA = (8.0 * DEPTH) ** -0.25
LN_EPS = 1e-5

IN_SPLIT_SIZES = (POOL_WIDTH, Q_WIDTH, KV_WIDTH, KV_WIDTH, KV_WIDTH, KV_WIDTH, KV_WIDTH, KV_WIDTH, 3 * N_HEADS, 2 * D_MODEL)
IN_COLS = POOL_WIDTH + Q_WIDTH + 6 * KV_WIDTH + 3 * N_HEADS + 2 * D_MODEL

kernel_name = "hybrid_pool_nsa_hmoe_deepnorm"


def layer_norm(x, g, b):
    xf = x.astype(jnp.float32)
    mu = jnp.mean(xf, axis=-1, keepdims=True)
    var = jnp.mean(jnp.square(xf - mu), axis=-1, keepdims=True)
    y = (xf - mu) * lax.rsqrt(var + LN_EPS) * g.astype(jnp.float32) + b.astype(jnp.float32)
    return y.astype(x.dtype)


def rope(x, pos):
    half = HEAD_DIM // 2
    inv_freq = ROPE_THETA ** (-2.0 * jnp.arange(half, dtype=jnp.float32) / HEAD_DIM)
    ang = pos.astype(jnp.float32)[:, None] * inv_freq[None, :]
    cos = jnp.cos(ang).astype(x.dtype)
    sin = jnp.sin(ang).astype(x.dtype)
    x1, x2 = x[..., :half], x[..., half:]
    return jnp.concatenate([x1 * cos - x2 * sin, x2 * cos + x1 * sin], axis=-1)


def masked_softmax(s, mask):
    s = s.astype(jnp.float32)
    m = jnp.max(jnp.where(mask, s, -jnp.inf), axis=-1, keepdims=True)
    e = jnp.exp(jnp.where(mask, s - m, -jnp.inf))
    return e / jnp.maximum(jnp.sum(e, axis=-1, keepdims=True), 1e-30)


def to_heads(t, n):
    b, s, _ = t.shape
    return t.reshape(b, s, n, HEAD_DIM).transpose(0, 2, 1, 3)


def pool_mixer(u, pool_mix, pool_scale):
    b, s, _ = u.shape
    ug = u.reshape(b, s, POOL_GROUPS, POOL_GROUP_DIM)
    cs = jnp.pad(jnp.cumsum(ug.astype(jnp.float32), axis=1), ((0, 0), (1, 0), (0, 0), (0, 0)))
    t = jnp.arange(s)
    means = []
    for g, w in enumerate(POOL_WINDOWS):
        cg = cs[:, :, g]
        lo = jnp.maximum(t + 1 - w, 0)
        cnt = jnp.minimum(t + 1, w).astype(jnp.float32)
        means.append((cg[:, 1:] - cg[:, lo]) / cnt[None, :, None])
    pooled = jnp.stack(means, axis=2).astype(u.dtype) - ug
    mixed = jnp.einsum('bsgc,gcd->bsgd', pooled, pool_mix)
    return mixed.reshape(b, s, POOL_WIDTH) * pool_scale


def compress_blocks(blocks, pos_emb, w1, w2):
    z = blocks + pos_emb
    z = z.reshape(z.shape[:-2] + (CMP_BLOCK * HEAD_DIM,))
    return jax.nn.gelu(z @ w1) @ w2


def native_sparse_attention(q, kc, vc, ks, vs, kw, vw, g_nsa, cmp_pos_k, cmp_pos_v,
                            cmp_k_w1, cmp_k_w2, cmp_v_w1, cmp_v_w2):
    b, s, _ = q.shape
    G, HG = N_KV_GROUPS, HEADS_PER_GROUP
    scale = HEAD_DIM ** -0.5
    pos = jnp.arange(s)
    qh = rope(to_heads(q, N_HEADS), pos).reshape(b, G, HG, s, HEAD_DIM)

    n_cmp = (s - CMP_BLOCK) // CMP_STRIDE + 1
    blk_idx = np.arange(n_cmp)[:, None] * CMP_STRIDE + np.arange(CMP_BLOCK)[None, :]
    cmp_end = blk_idx[:, -1]
    k_cmp = compress_blocks(to_heads(kc, G)[:, :, blk_idx], cmp_pos_k, cmp_k_w1, cmp_k_w2)
    k_cmp = rope(k_cmp, jnp.asarray(cmp_end))
    v_cmp = compress_blocks(to_heads(vc, G)[:, :, blk_idx], cmp_pos_v, cmp_v_w1, cmp_v_w2)
    s_cmp = jnp.einsum('bghqd,bgcd->bghqc', qh, k_cmp) * scale
    cmp_mask = jnp.asarray(cmp_end)[None, :] <= pos[:, None]
    p_cmp = masked_softmax(s_cmp, cmp_mask)
    o_cmp = jnp.einsum('bghqc,bgcd->bghqd', p_cmp.astype(v_cmp.dtype), v_cmp)

    n_sel = s // SEL_BLOCK
    top_n = min(SEL_TOP_N, n_sel)
    c_start = np.arange(n_cmp) * CMP_STRIDE
    s_start = np.arange(n_sel) * SEL_BLOCK
    overlap = ((c_start[:, None] + CMP_BLOCK - 1 >= s_start[None, :]) &
               (c_start[:, None] <= s_start[None, :] + SEL_BLOCK - 1)).astype(np.float32)
    imp = jnp.einsum('bghqc,cj->bgqj', p_cmp, jnp.asarray(overlap))
    cur = pos // SEL_BLOCK
    j = jnp.arange(n_sel)
    forced = (j[None, :] == 0) | (j[None, :] == cur[:, None]) | (j[None, :] == cur[:, None] - 1)
    future = j[None, :] > cur[:, None]
    imp = jnp.where(forced, FORCE_SCORE, jnp.where(future, -FORCE_SCORE, imp))
    _, sel_idx = lax.top_k(imp, top_n)

    k_sb = rope(to_heads(ks, G), pos).reshape(b, G, n_sel, SEL_BLOCK, HEAD_DIM)
    v_sb = to_heads(vs, G).reshape(b, G, n_sel, SEL_BLOCK, HEAD_DIM)
    nqb = s // SEL_Q_BLOCK
    q_blk = qh.reshape(b, G, HG, nqb, SEL_Q_BLOCK, HEAD_DIM).transpose(3, 0, 1, 2, 4, 5)
    idx_blk = sel_idx.reshape(b, G, nqb, SEL_Q_BLOCK, top_n).transpose(2, 0, 1, 3, 4)
    bi = jnp.arange(b)[:, None, None, None]
    gi = jnp.arange(G)[None, :, None, None]

    def sel_step(args):
        qb, ib, blk = args
        kg = k_sb[bi, gi, ib]
        vg = v_sb[bi, gi, ib]
        sc = jnp.einsum('bghqd,bgqnkd->bghqnk', qb, kg) * scale
        tq = blk * SEL_Q_BLOCK + jnp.arange(SEL_Q_BLOCK)
        kpos = ib[..., None] * SEL_BLOCK + jnp.arange(SEL_BLOCK)
        mask = (kpos <= tq[:, None, None]).reshape(b, G, 1, SEL_Q_BLOCK, top_n * SEL_BLOCK)
        p = masked_softmax(sc.reshape(b, G, HG, SEL_Q_BLOCK, top_n * SEL_BLOCK), mask)
        p = p.reshape(sc.shape).astype(vg.dtype)
        return jnp.einsum('bghqnk,bgqnkd->bghqd', p, vg)

    o_slc = lax.map(sel_step, (q_blk, idx_blk, jnp.arange(nqb)))
    o_slc = o_slc.transpose(1, 2, 3, 0, 4, 5).reshape(b, G, HG, s, HEAD_DIM)

    k_w = rope(to_heads(kw, G), pos)
    v_w = to_heads(vw, G)
    kpad = jnp.pad(k_w, ((0, 0), (0, 0), (WINDOW, 0), (0, 0)))
    vpad = jnp.pad(v_w, ((0, 0), (0, 0), (WINDOW, 0), (0, 0)))
    span = WINDOW + WIN_Q_BLOCK
    nwb = s // WIN_Q_BLOCK
    q_wblk = qh.reshape(b, G, HG, nwb, WIN_Q_BLOCK, HEAD_DIM).transpose(3, 0, 1, 2, 4, 5)

    def win_step(args):
        qb, blk = args
        start = blk * WIN_Q_BLOCK
        kb = lax.dynamic_slice_in_dim(kpad, start, span, axis=2)
        vb = lax.dynamic_slice_in_dim(vpad, start, span, axis=2)
        sc = jnp.einsum('bghqd,bgkd->bghqk', qb, kb) * scale
        tq = start + jnp.arange(WIN_Q_BLOCK)
        kpos = start - WINDOW + jnp.arange(span)
        mask = ((kpos[None, :] <= tq[:, None]) & (kpos[None, :] > tq[:, None] - WINDOW)
                & (kpos[None, :] >= 0))
        p = masked_softmax(sc, mask).astype(vb.dtype)
        return jnp.einsum('bghqk,bgkd->bghqd', p, vb)

    o_win = lax.map(win_step, (q_wblk, jnp.arange(nwb)))
    o_win = o_win.transpose(1, 2, 3, 0, 4, 5).reshape(b, G, HG, s, HEAD_DIM)

    g = jax.nn.sigmoid(g_nsa.astype(jnp.float32)).astype(q.dtype)
    g = g.reshape(b, s, 3, G, HG).transpose(2, 0, 3, 4, 1)[..., None]
    o = g[0] * o_cmp + g[1] * o_slc + g[2] * o_win
    return o.reshape(b, N_HEADS, s, HEAD_DIM).transpose(0, 2, 1, 3).reshape(b, s, Q_WIDTH)


def token_mixer(h, w_in, pool_mix, pool_scale, w_pool_proj, w_nsa_proj, cmp_pos_k, cmp_pos_v,
                cmp_k_w1, cmp_k_w2, cmp_v_w1, cmp_v_w2, w_out):
    proj = h @ w_in
    points = [int(v) for v in np.cumsum(IN_SPLIT_SIZES)[:-1]]
    u_pool, q, kc, vc, ks, vs, kw, vw, g_nsa, g_merge = jnp.split(proj, points, axis=-1)
    y_pool = pool_mixer(u_pool, pool_mix, pool_scale) @ w_pool_proj
    y_attn = native_sparse_attention(q, kc, vc, ks, vs, kw, vw, g_nsa, cmp_pos_k, cmp_pos_v,
                                     cmp_k_w1, cmp_k_w2, cmp_v_w1, cmp_v_w2) @ w_nsa_proj
    gates = jax.nn.sigmoid(g_merge.astype(jnp.float32)).astype(h.dtype)
    g_pool, g_attn = gates[..., :D_MODEL], gates[..., D_MODEL:]
    return (g_pool * y_pool + g_attn * y_attn) @ w_out


def hierarchical_moe(h, router_group_w, router_group_b, router_expert_w, router_expert_b,
                     w_gate, w_up, w_down):
    b, s, d = h.shape
    n = b * s
    t = h.reshape(n, d)
    gl = (t @ router_group_w + router_group_b).astype(jnp.float32)
    gp = jax.nn.softmax(gl, axis=-1)
    grp = jnp.argmax(gl, axis=-1)
    g_gate = jnp.take_along_axis(gp, grp[:, None], axis=1)[:, 0]
    el = jnp.einsum('nd,gde->nge', t, router_expert_w) + router_expert_b
    el = jnp.take_along_axis(el, grp[:, None, None], axis=1)[:, 0].astype(jnp.float32)
    top_v, top_i = lax.top_k(el, TOP_K)
    wts = jax.nn.softmax(top_v, axis=-1) * g_gate[:, None]
    eid = grp[:, None] * EXPERTS_PER_GROUP + top_i
    m = n * TOP_K
    flat_e = eid.reshape(-1)
    order = jnp.argsort(flat_e)
    sorted_e = flat_e[order]
    tok = order // TOP_K
    w_sorted = wts.reshape(-1)[order]
    sizes = jnp.bincount(flat_e, length=N_EXPERTS)
    padded = (sizes + EXPERT_ROW_BLOCK - 1) // EXPERT_ROW_BLOCK * EXPERT_ROW_BLOCK
    ends = jnp.cumsum(padded)
    group_start = jnp.cumsum(sizes) - sizes
    dest = (ends - padded)[sorted_e] + (jnp.arange(m) - group_start[sorted_e])
    n_blk = -(-(m + N_EXPERTS * (EXPERT_ROW_BLOCK - 1)) // EXPERT_ROW_BLOCK)
    rows = n_blk * EXPERT_ROW_BLOCK
    xbuf = jnp.zeros((rows, d), t.dtype).at[dest].set(t[tok])
    blk_e = jnp.minimum(jnp.searchsorted(ends, jnp.arange(n_blk) * EXPERT_ROW_BLOCK, side='right'),
                        N_EXPERTS - 1)

    def expert_block(args):
        xb, e = args
        return (jax.nn.silu(xb @ w_gate[e]) * (xb @ w_up[e])) @ w_down[e]

    ybuf = lax.map(expert_block, (xbuf.reshape(n_blk, EXPERT_ROW_BLOCK, d), blk_e)).reshape(rows, d)
    contrib = ybuf[dest] * w_sorted[:, None].astype(t.dtype)
    y = jnp.zeros_like(t).at[tok].add(contrib)
    return y.reshape(b, s, d)


def setup_inputs(seed: int = 0) -> dict:
    key = jax.random.key(seed)
    ks = jax.random.split(key, 26)
    L, D = DEPTH, D_MODEL

    def nrm(k, shape, scale):
        return jax.random.normal(k, shape, jnp.float32) * scale

    return {
        "x": nrm(ks[0], (BATCH, SEQ, D), 1.0),
        "w_in": nrm(ks[1], (L, D, IN_COLS), D ** -0.5),
        "pool_mix": nrm(ks[2], (L, POOL_GROUPS, POOL_GROUP_DIM, POOL_GROUP_DIM), POOL_GROUP_DIM ** -0.5),
        "pool_scale": 1.0 + nrm(ks[3], (L, POOL_WIDTH), 0.1),
        "w_pool_proj": nrm(ks[4], (L, POOL_WIDTH, D), POOL_WIDTH ** -0.5),
        "w_nsa_proj": nrm(ks[5], (L, Q_WIDTH, D), Q_WIDTH ** -0.5),
        "cmp_pos_k": nrm(ks[6], (L, CMP_BLOCK, HEAD_DIM), 0.1),
        "cmp_pos_v": nrm(ks[7], (L, CMP_BLOCK, HEAD_DIM), 0.1),
        "cmp_k_w1": nrm(ks[8], (L, CMP_BLOCK * HEAD_DIM, CMP_HIDDEN), (CMP_BLOCK * HEAD_DIM) ** -0.5),
        "cmp_k_w2": nrm(ks[9], (L, CMP_HIDDEN, HEAD_DIM), CMP_HIDDEN ** -0.5),
        "cmp_v_w1": nrm(ks[10], (L, CMP_BLOCK * HEAD_DIM, CMP_HIDDEN), (CMP_BLOCK * HEAD_DIM) ** -0.5),
        "cmp_v_w2": nrm(ks[11], (L, CMP_HIDDEN, HEAD_DIM), CMP_HIDDEN ** -0.5),
        "w_out": nrm(ks[12], (L, D, D), D ** -0.5 * DN_BETA),
        "ln1_g": 1.0 + nrm(ks[13], (L, D), 0.05),
        "ln1_b": nrm(ks[14], (L, D), 0.02),
        "router_group_w": nrm(ks[15], (L, D, N_EXPERT_GROUPS), D ** -0.5),
        "router_group_b": nrm(ks[16], (L, N_EXPERT_GROUPS), 0.01),
        "router_expert_w": nrm(ks[17], (L, N_EXPERT_GROUPS, D, EXPERTS_PER_GROUP), D ** -0.5),
        "router_expert_b": nrm(ks[18], (L, N_EXPERT_GROUPS, EXPERTS_PER_GROUP), 0.01),
        "w_gate": nrm(ks[19], (L, N_EXPERTS, D, EXPERT_HIDDEN), D ** -0.5),
        "w_up": nrm(ks[20], (L, N_EXPERTS, D, EXPERT_HIDDEN), D ** -0.5),
        "w_down": nrm(ks[21], (L, N_EXPERTS, EXPERT_HIDDEN, D), EXPERT_HIDDEN ** -0.5 * DN_BETA),
        "ln2_g": 1.0 + nrm(ks[22], (L, D), 0.05),
        "ln2_b": nrm(ks[23], (L, D), 0.02),
    }


def reference(x, w_in, pool_mix, pool_scale, w_pool_proj, w_nsa_proj, cmp_pos_k, cmp_pos_v,
              cmp_k_w1, cmp_k_w2, cmp_v_w1, cmp_v_w2, w_out, ln1_g, ln1_b,
              router_group_w, router_group_b, router_expert_w, router_expert_b,
              w_gate, w_up, w_down, ln2_g, ln2_b):
    h = x
    for l in range(DEPTH):
        y = token_mixer(h, w_in[l], pool_mix[l], pool_scale[l], w_pool_proj[l], w_nsa_proj[l],
                        cmp_pos_k[l], cmp_pos_v[l], cmp_k_w1[l], cmp_k_w2[l], cmp_v_w1[l],
                        cmp_v_w2[l], w_out[l])
        h = layer_norm(DN_ALPHA * h + y, ln1_g[l], ln1_b[l])
        y = hierarchical_moe(h, router_group_w[l], router_group_b[l], router_expert_w[l],
                             router_expert_b[l], w_gate[l], w_up[l], w_down[l])
        h = layer_norm(DN_ALPHA * h + y, ln2_g[l], ln2_b[l])
    return h
```

```python
import functools
import math

import numpy as np
import jax
import jax.numpy as jnp
from jax import lax
from jax.experimental import pallas as pl
from jax.experimental.pallas import tpu as pltpu

F32 = jnp.float32
BF16 = jnp.bfloat16
I32 = jnp.int32

POOL_WINDOWS = (2, 4, 8, 16)
POOL_GROUPS = 4
POOL_HALO = 16
HEAD_DIM = 64
HALF_DIM = HEAD_DIM // 2
N_KV_GROUPS = 4
CMP_BLOCK = 32
CMP_STRIDE = 16
SEL_BLOCK = 64
SEL_TOP_N = 16
WINDOW = 512
ROPE_THETA = 10000.0
FORCE_SCORE = 1e6
TOP_K = 2
LN_EPS = 1e-5
NEG = -1e30

LANES = 128
SUBLANES = 8
VMEM_BYTES_V7X = 64 * 1024 * 1024
VMEM_LIMIT = VMEM_BYTES_V7X - 8 * 1024 * 1024

ATTN_TILE = 256
EXPERT_ROWS = 256


def _cparams(sem, vmem=VMEM_LIMIT):
    return pltpu.CompilerParams(dimension_semantics=sem, vmem_limit_bytes=vmem)


def _tile(n, pref):
    t = min(n, pref)
    while n % t:
        t //= 2
    return t


def _proj_kernel(x_ref, w_ref, o_ref, xb_ref):
    @pl.when(pl.program_id(1) == 0)
    def _():
        xb_ref[...] = x_ref[...].astype(BF16)

    o_ref[...] = jnp.dot(xb_ref[...], w_ref[...], preferred_element_type=F32)


def _project(x, w, tm, tn):
    n, d = x.shape
    cols = w.shape[1]
    return pl.pallas_call(
        _proj_kernel,
        out_shape=jax.ShapeDtypeStruct((n, cols), F32),
        grid=(n // tm, cols // tn),
        in_specs=[pl.BlockSpec((tm, d), lambda i, j: (i, 0)),
                  pl.BlockSpec((d, tn), lambda i, j: (0, j))],
        out_specs=pl.BlockSpec((tm, tn), lambda i, j: (i, j)),
        scratch_shapes=[pltpu.VMEM((tm, d), BF16)],
        compiler_params=_cparams(("arbitrary", "arbitrary")),
        name="in_proj",
    )(x, w)


def _pool_kernel(u_ref, halo_ref, mix_ref, scale_ref, o_ref, ext_ref):
    i = pl.program_id(1)
    ts = u_ref.shape[0]
    gd = mix_ref.shape[1]
    ext_ref[POOL_HALO:, :] = u_ref[...]
    ext_ref[:POOL_HALO, :] = jnp.where(i == 0, 0.0, halo_ref[...])
    t = i * ts + lax.broadcasted_iota(I32, (ts, gd), 0)
    for g, w in enumerate(POOL_WINDOWS):
        cols = slice(g * gd, (g + 1) * gd)
        s = ext_ref[:, cols]
        k = 1
        while k < w:
            s = s + pltpu.roll(s, k, axis=0)
            k *= 2
        cnt = jnp.minimum(t + 1, w).astype(F32)
        pooled = s[POOL_HALO:, :] / cnt - u_ref[:, cols]
        mixed = jnp.dot(pooled.astype(BF16), mix_ref[g], preferred_element_type=F32)
        o_ref[:, cols] = (mixed * scale_ref[:, cols]).astype(BF16)


def _pool_mixer(proj, pool_mix_b, pool_scale, b, s, pw, ts):
    ns = s // ts
    hb = ts // POOL_HALO
    return pl.pallas_call(
        _pool_kernel,
        out_shape=jax.ShapeDtypeStruct((b * s, pw), BF16),
        grid=(b, ns),
        in_specs=[pl.BlockSpec((ts, pw), lambda bi, i: (bi * ns + i, 0)),
                  pl.BlockSpec((POOL_HALO, pw), lambda bi, i: (jnp.maximum((bi * ns + i) * hb - 1, 0), 0)),
                  pl.BlockSpec(pool_mix_b.shape, lambda bi, i: (0, 0, 0)),
                  pl.BlockSpec((1, pw), lambda bi, i: (0, 0))],
        out_specs=pl.BlockSpec((ts, pw), lambda bi, i: (bi * ns + i, 0)),
        scratch_shapes=[pltpu.VMEM((POOL_HALO + ts, pw), F32)],
        compiler_params=_cparams(("arbitrary", "arbitrary")),
        name="pool_mixer",
    )(proj, proj, pool_mix_b, pool_scale)


def _gelu_tanh(x):
    return 0.5 * x * (1.0 + jnp.tanh(math.sqrt(2.0 / math.pi) * (x + 0.044715 * (x * x * x))))


def _swap_halves(x):
    return jnp.concatenate([x[:, HALF_DIM:], x[:, :HALF_DIM]], axis=1)


def _cmp_kernel(rk_ref, rv_ref, pk_ref, pv_ref, kw1_ref, kw2_ref, vw1_ref, vw2t_ref, cos_ref, sin_ref,
                ko_ref, vo_ref):
    half = rk_ref.shape[3]
    nrow = rk_ref.shape[2]

    def hidden(r_ref, p_ref, w1_ref):
        r = r_ref[0, 0]
        r_next = pltpu.roll(r, nrow - 1, axis=0)
        a = (r + p_ref[:, :half]).astype(BF16)
        bb = (r_next + p_ref[:, half:]).astype(BF16)
        h = (jnp.dot(a, w1_ref[:half, :], preferred_element_type=F32)
             + jnp.dot(bb, w1_ref[half:, :], preferred_element_type=F32))
        return _gelu_tanh(h).astype(BF16)

    k = jnp.dot(hidden(rk_ref, pk_ref, kw1_ref), kw2_ref[...], preferred_element_type=F32)
    k = k * cos_ref[...] + _swap_halves(k) * sin_ref[...]
    ko_ref[0, 0] = k.astype(BF16)
    hv = hidden(rv_ref, pv_ref, vw1_ref)
    vt = lax.dot_general(vw2t_ref[...], hv, (((1,), (1,)), ((), ())), preferred_element_type=F32)
    vo_ref[0, 0] = vt.astype(BF16)


def _compress(rk, rv, pk, pv, kw1, kw2, vw1, vw2t, cos_c, sin_c):
    b, g, nrow, half = rk.shape
    blk = pl.BlockSpec((1, 1, nrow, half), lambda bi, gi: (bi, gi, 0, 0))

    def whole(a):
        return pl.BlockSpec(a.shape, lambda bi, gi: (0,) * a.ndim)

    return pl.pallas_call(
        _cmp_kernel,
        out_shape=(jax.ShapeDtypeStruct((b, g, nrow, HEAD_DIM), BF16),
                   jax.ShapeDtypeStruct((b, g, HEAD_DIM, nrow), BF16)),
        grid=(b, g),
        in_specs=[blk, blk, whole(pk), whole(pv), whole(kw1), whole(kw2), whole(vw1), whole(vw2t),
                  whole(cos_c), whole(sin_c)],
        out_specs=(pl.BlockSpec((1, 1, nrow, HEAD_DIM), lambda bi, gi: (bi, gi, 0, 0)),
                   pl.BlockSpec((1, 1, HEAD_DIM, nrow), lambda bi, gi: (bi, gi, 0, 0))),
        compiler_params=_cparams(("arbitrary", "arbitrary")),
        name="compress_kv",
    )(rk, rv, pk, pv, kw1, kw2, vw1, vw2t, cos_c, sin_c)


def _krope_kernel(ks_ref, kw_ref, cos_ref, sin_ref, kso_ref, kwo_ref):
    s = ks_ref.shape[2]
    cos = cos_ref[...]
    sin = sin_ref[...]
    ks = ks_ref[0, 0]
    kso_ref[0, 0, :, :HEAD_DIM] = (ks * cos + _swap_halves(ks) * sin).astype(BF16)
    blk = lax.broadcasted_iota(I32, (s, HEAD_DIM), 0) // SEL_BLOCK
    col = lax.broadcasted_iota(I32, (s, HEAD_DIM), 1)
    kso_ref[0, 0, :, HEAD_DIM:] = jnp.where(blk == col, 1.0, 0.0).astype(BF16)
    kw = kw_ref[0, 0]
    kwo_ref[0, 0] = (kw * cos + _swap_halves(kw) * sin).astype(BF16)


def _key_rope(ks, kw, cos_f, sin_s):
    b, g, s, hd = ks.shape
    blk = pl.BlockSpec((1, 1, s, hd), lambda bi, gi: (bi, gi, 0, 0))
    tab = pl.BlockSpec((s, hd), lambda bi, gi: (0, 0))
    return pl.pallas_call(
        _krope_kernel,
        out_shape=(jax.ShapeDtypeStruct((b, g, s, 2 * hd), BF16), jax.ShapeDtypeStruct((b, g, s, hd), BF16)),
        grid=(b, g),
        in_specs=[blk, blk, tab, tab],
        out_specs=(pl.BlockSpec((1, 1, s, 2 * hd), lambda bi, gi: (bi, gi, 0, 0)), blk),
        compiler_params=_cparams(("arbitrary", "arbitrary")),
        name="key_rope",
    )(ks, kw, cos_f, sin_s)


def _attn_kernel(q_ref, cos_ref, sin_ref, gate_ref, kc_ref, vct_ref, ks_ref, vst_ref, kw_ref, vwt_ref, ovt_ref,
                 o_ref, qa_s, g_s, imp_s, m_s, l_s, acc_s, out_s, *, hg, n_heads):
    g = pl.program_id(1)
    i = pl.program_id(2)
    tq = q_ref.shape[0]
    tk = tq
    ncp = kc_ref.shape[2]
    nsel = ovt_ref.shape[0]
    scale = HEAD_DIM ** -0.5

    qt = q_ref[...].T
    cos = cos_ref[...]
    sin = sin_ref[...]
    for h in range(hg):
        x1 = qt[h * HEAD_DIM:h * HEAD_DIM + HALF_DIM]
        x2 = qt[h * HEAD_DIM + HALF_DIM:(h + 1) * HEAD_DIM]
        qa_s[h, :HALF_DIM] = ((x1 * cos - x2 * sin) * scale).astype(BF16)
        qa_s[h, HALF_DIM:HEAD_DIM] = ((x2 * cos + x1 * sin) * scale).astype(BF16)
    g_s[...] = gate_ref[...].T

    def gate(branch, h):
        row = g_s[pl.ds(branch * n_heads + g * hg + h, 1), :]
        return 1.0 / (1.0 + jnp.exp(-row))

    t_c = i * tq + lax.broadcasted_iota(I32, (ncp, tq), 1)
    c_end = lax.broadcasted_iota(I32, (ncp, tq), 0) * CMP_STRIDE + (CMP_BLOCK - 1)
    cmask = c_end <= t_c
    psum = jnp.zeros((ncp, tq), F32)
    for h in range(hg):
        s = jnp.dot(kc_ref[0, 0], qa_s[h, :HEAD_DIM], preferred_element_type=F32)
        sm = jnp.where(cmask, s, NEG)
        m = jnp.max(sm, axis=0, keepdims=True)
        p = jnp.where(cmask, jnp.exp(sm - m), 0.0)
        den = jnp.maximum(jnp.sum(p, axis=0, keepdims=True), 1e-30)
        pn = p / den
        psum = psum + pn
        oc = jnp.dot(vct_ref[0, 0], pn.astype(BF16), preferred_element_type=F32)
        out_s[h * HEAD_DIM:(h + 1) * HEAD_DIM, :] = oc * gate(0, h)

    p_hi = psum.astype(BF16)
    p_lo = (psum - p_hi.astype(F32)).astype(BF16)
    imp = (jnp.dot(ovt_ref[...], p_hi, preferred_element_type=F32)
           + jnp.dot(ovt_ref[...], p_lo, preferred_element_type=F32))
    jb = lax.broadcasted_iota(I32, (nsel, tq), 0)
    cur = (i * tq + lax.broadcasted_iota(I32, (nsel, tq), 1)) // SEL_BLOCK
    forced = (jb == 0) | (jb == cur) | (jb == cur - 1)
    imp = jnp.where(forced, FORCE_SCORE, jnp.where(jb > cur, -FORCE_SCORE, imp))
    imp_s[...] = imp
    nrb = nsel // SUBLANES
    blocks = [imp[r * SUBLANES:(r + 1) * SUBLANES] for r in range(nrb)]
    counts = [jnp.zeros((SUBLANES, tq), F32) for _ in range(nrb)]
    sub = lax.broadcasted_iota(I32, (SUBLANES, tq), 0)
    for jp in range(nsel):
        row = jnp.broadcast_to(imp_s[jp:jp + 1, :], (SUBLANES, tq))
        for r in range(nrb):
            if r * SUBLANES > jp:
                inc = jnp.where(row >= blocks[r], 1.0, 0.0)
            elif r * SUBLANES + SUBLANES - 1 <= jp:
                inc = jnp.where(row > blocks[r], 1.0, 0.0)
            else:
                tie = jnp.where(sub > jp - r * SUBLANES, 1.0, 0.0)
                inc = jnp.where(row > blocks[r], 1.0, jnp.where(row >= blocks[r], tie, 0.0))
            counts[r] = counts[r] + inc
    top_n = min(SEL_TOP_N, nsel)
    bias = jnp.concatenate([jnp.where(cnt < top_n, 0.0, NEG) for cnt in counts], axis=0).astype(BF16)
    for h in range(hg):
        qa_s[h, HEAD_DIM:HEAD_DIM + nsel] = bias
        if nsel < HEAD_DIM:
            qa_s[h, HEAD_DIM + nsel:] = jnp.zeros((HEAD_DIM - nsel, tq), BF16)

    def reset():
        m_s[...] = jnp.full(m_s.shape, NEG, F32)
        l_s[...] = jnp.zeros(l_s.shape, F32)
        acc_s[...] = jnp.zeros(acc_s.shape, F32)

    def step(kx, vt, qrows, mask):
        for h in range(hg):
            s = jnp.dot(kx, qa_s[h, :qrows], preferred_element_type=F32)
            if mask is not None:
                s = jnp.where(mask, s, NEG)
            m_old = m_s[h]
            m_new = jnp.maximum(m_old, jnp.max(s, axis=0, keepdims=True))
            p = jnp.exp(s - m_new)
            corr = jnp.exp(m_old - m_new)
            l_s[h] = l_s[h] * corr + jnp.sum(p, axis=0, keepdims=True)
            acc_s[h] = acc_s[h] * corr + jnp.dot(vt, p.astype(BF16), preferred_element_type=F32)
            m_s[h] = m_new

    def finish(branch):
        for h in range(hg):
            rows = slice(h * HEAD_DIM, (h + 1) * HEAD_DIM)
            out_s[rows, :] = out_s[rows, :] + acc_s[h] * (gate(branch, h) / l_s[h])

    krow = lax.broadcasted_iota(I32, (tk, tq), 0)
    qcol = lax.broadcasted_iota(I32, (tk, tq), 1)
    causal = krow <= qcol

    reset()

    def sel_body(kt, carry):
        step(ks_ref[0, 0, kt], vst_ref[0, 0, kt], 2 * HEAD_DIM, None)
        return carry

    lax.fori_loop(0, i, sel_body, 0)
    step(ks_ref[0, 0, i], vst_ref[0, 0, i], 2 * HEAD_DIM, causal)
    finish(1)

    reset()
    nwt = WINDOW // tk
    @pl.when(i >= nwt)
    def _():
        kt = i - nwt
        step(kw_ref[0, 0, kt], vwt_ref[0, 0, kt], HEAD_DIM, krow > qcol)

    for d in range(nwt - 1, 0, -1):
        @pl.when(i >= d)
        def _(d=d):
            kt = i - d
            step(kw_ref[0, 0, kt], vwt_ref[0, 0, kt], HEAD_DIM, None)

    step(kw_ref[0, 0, i], vwt_ref[0, 0, i], HEAD_DIM, causal)
    finish(2)

    o_ref[...] = out_s[...].T.astype(o_ref.dtype)


def _attention(proj, cos_t, sin_t, kcmp, vcmp_t, ksx, vs_t, kwx, vw_t, ov_t, *, b, s, pw, qw, gate_col):
    tq = ATTN_TILE
    nq = s // tq
    g = N_KV_GROUPS
    gw = qw // g
    hg = gw // HEAD_DIM
    n_heads = qw // HEAD_DIM
    ncp = kcmp.shape[2]
    nsel = ov_t.shape[0]
    q_blk0 = pw // gw
    kern = functools.partial(_attn_kernel, hg=hg, n_heads=n_heads)
    kv5 = lambda bi, gi, i: (bi, gi, 0, 0, 0)
    kv4 = lambda bi, gi, i: (bi, gi, 0, 0)
    return pl.pallas_call(
        kern,
        out_shape=jax.ShapeDtypeStruct((b * s, qw), BF16),
        grid=(b, g, nq),
        in_specs=[pl.BlockSpec((tq, gw), lambda bi, gi, i: (bi * nq + i, q_blk0 + gi)),
                  pl.BlockSpec((HALF_DIM, tq), lambda bi, gi, i: (0, i)),
                  pl.BlockSpec((HALF_DIM, tq), lambda bi, gi, i: (0, i)),
                  pl.BlockSpec((tq, LANES), lambda bi, gi, i: (bi * nq + i, gate_col // LANES)),
                  pl.BlockSpec((1, 1, ncp, HEAD_DIM), kv4),
                  pl.BlockSpec((1, 1, HEAD_DIM, ncp), kv4),
                  pl.BlockSpec((1, 1, nq, tq, 2 * HEAD_DIM), kv5),
                  pl.BlockSpec((1, 1, nq, HEAD_DIM, tq), kv5),
                  pl.BlockSpec((1, 1, nq, tq, HEAD_DIM), kv5),
                  pl.BlockSpec((1, 1, nq, HEAD_DIM, tq), kv5),
                  pl.BlockSpec(ov_t.shape, lambda bi, gi, i: (0, 0))],
        out_specs=pl.BlockSpec((tq, gw), lambda bi, gi, i: (bi * nq + i, gi)),
        scratch_shapes=[pltpu.VMEM((hg, 2 * HEAD_DIM, tq), BF16),
                        pltpu.VMEM((LANES, tq), F32),
                        pltpu.VMEM((nsel, tq), F32),
                        pltpu.VMEM((hg, 1, tq), F32),
                        pltpu.VMEM((hg, 1, tq), F32),
                        pltpu.VMEM((hg, HEAD_DIM, tq), F32),
                        pltpu.VMEM((gw, tq), F32)],
        compiler_params=_cparams(("arbitrary", "arbitrary", "arbitrary")),
        name="nsa_attention",
    )(proj, cos_t, sin_t, proj, kcmp, vcmp_t, ksx, vs_t, kwx, vw_t, ov_t)


def _layer_norm(r, g, b):
    mu = jnp.mean(r, axis=-1, keepdims=True)
    d = r - mu
    var = jnp.mean(d * d, axis=-1, keepdims=True)
    return d * lax.rsqrt(var + LN_EPS) * g + b


def _sigmoid(x):
    return 1.0 / (1.0 + jnp.exp(-x))


def _merge_kernel(x_ref, pm_ref, at_ref, wpp_ref, wnp_ref, wgp_ref, wga_ref, wo_ref, g_ref, b_ref, o_ref,
                  xb_ref, acc_ref, *, alpha):
    c = pl.program_id(1)

    @pl.when(c == 0)
    def _():
        xb_ref[...] = x_ref[...].astype(BF16)
        acc_ref[...] = jnp.zeros(acc_ref.shape, F32)

    xb = xb_ref[...]
    y_pool = jnp.dot(pm_ref[...], wpp_ref[...], preferred_element_type=F32)
    y_attn = jnp.dot(at_ref[...], wnp_ref[...], preferred_element_type=F32)
    g_pool = _sigmoid(jnp.dot(xb, wgp_ref[...], preferred_element_type=F32))
    g_attn = _sigmoid(jnp.dot(xb, wga_ref[...], preferred_element_type=F32))
    z = g_pool * y_pool + g_attn * y_attn
    acc_ref[...] += jnp.dot(z.astype(BF16), wo_ref[...], preferred_element_type=F32)

    @pl.when(c == pl.num_programs(1) - 1)
    def _():
        o_ref[...] = _layer_norm(alpha * x_ref[...] + acc_ref[...], g_ref[...], b_ref[...])


def _merge_out(x, pm, at, wpp, wnp, wgp, wga, wo, ln_g, ln_b, alpha, tm, ck):
    n, d = x.shape
    pw = pm.shape[1]
    qw = at.shape[1]
    row = lambda i, c: (i, 0)
    colc = lambda i, c: (0, c)
    return pl.pallas_call(
        functools.partial(_merge_kernel, alpha=alpha),
        out_shape=jax.ShapeDtypeStruct((n, d), F32),
        grid=(n // tm, d // ck),
        in_specs=[pl.BlockSpec((tm, d), row), pl.BlockSpec((tm, pw), row), pl.BlockSpec((tm, qw), row),
                  pl.BlockSpec((pw, ck), colc), pl.BlockSpec((qw, ck), colc),
                  pl.BlockSpec((d, ck), colc), pl.BlockSpec((d, ck), colc),
                  pl.BlockSpec((ck, d), lambda i, c: (c, 0)),
                  pl.BlockSpec((1, d), lambda i, c: (0, 0)), pl.BlockSpec((1, d), lambda i, c: (0, 0))],
        out_specs=pl.BlockSpec((tm, d), row),
        scratch_shapes=[pltpu.VMEM((tm, d), BF16), pltpu.VMEM((tm, d), F32)],
        compiler_params=_cparams(("arbitrary", "arbitrary")),
        name="merge_out_ln",
    )(x, pm, at, wpp, wnp, wgp, wga, wo, ln_g, ln_b)


def _router_kernel(h_ref, whi_ref, wlo_ref, b_ref, e_ref, w_ref, *, n_groups, per_group):
    h = h_ref[...]
    h_hi = h.astype(BF16)
    h_lo = (h - h_hi.astype(F32)).astype(BF16)
    logits = (jnp.dot(h_hi, whi_ref[...], preferred_element_type=F32)
              + jnp.dot(h_hi, wlo_ref[...], preferred_element_type=F32)
              + jnp.dot(h_lo, whi_ref[...], preferred_element_type=F32)) + b_ref[...]
    lane = lax.broadcasted_iota(I32, logits.shape, 1)
    far = LANES

    def first_argmax(v, vmax):
        return jnp.min(jnp.where(v == vmax, lane, far), axis=-1, keepdims=True)

    gl = jnp.where(lane < n_groups, logits, NEG)
    gmax = jnp.max(gl, axis=-1, keepdims=True)
    grp = first_argmax(gl, gmax)
    gsum = jnp.sum(jnp.where(lane < n_groups, jnp.exp(gl - gmax), 0.0), axis=-1, keepdims=True)
    g_gate = 1.0 / gsum
    lo = n_groups + grp * per_group
    el = jnp.where((lane >= lo) & (lane < lo + per_group), logits, NEG)
    v1 = jnp.max(el, axis=-1, keepdims=True)
    i1 = first_argmax(el, v1)
    el2 = jnp.where(lane == i1, NEG, el)
    v2 = jnp.max(el2, axis=-1, keepdims=True)
    i2 = first_argmax(el2, v2)
    e21 = jnp.exp(v2 - v1)
    w1 = g_gate / (1.0 + e21)
    w2 = g_gate * e21 / (1.0 + e21)
    e_ref[...] = jnp.where(lane == 0, i1 - n_groups, jnp.where(lane == 1, i2 - n_groups, 0))
    w_ref[...] = jnp.where(lane == 0, w1, jnp.where(lane == 1, w2, 0.0))


def _router(h, w_hi, w_lo, bias, n_groups, per_group, tm):
    n, d = h.shape
    return pl.pallas_call(
        functools.partial(_router_kernel, n_groups=n_groups, per_group=per_group),
        out_shape=(jax.ShapeDtypeStruct((n, LANES), I32), jax.ShapeDtypeStruct((n, LANES), F32)),
        grid=(n // tm,),
        in_specs=[pl.BlockSpec((tm, d), lambda i: (i, 0)),
                  pl.BlockSpec((d, LANES), lambda i: (0, 0)), pl.BlockSpec((d, LANES), lambda i: (0, 0)),
                  pl.BlockSpec((1, LANES), lambda i: (0, 0))],
        out_specs=(pl.BlockSpec((tm, LANES), lambda i: (i, 0)), pl.BlockSpec((tm, LANES), lambda i: (i, 0))),
        compiler_params=_cparams(("arbitrary",)),
        name="moe_router",
    )(h, w_hi, w_lo, bias)


def _expert_kernel(blk_e_ref, n_used_ref, row_tok_ref, h_hbm, rw_ref, wg_ref, wu_ref, wd_ref, y_ref,
                   xbuf, wgb, wub, wdb, sem):
    bi = pl.program_id(0)
    rows = xbuf.shape[1]
    n_used = n_used_ref[0]
    slot = bi % 2

    def row_copy(blk, slot_, r):
        tok = row_tok_ref[blk * rows + r]
        return pltpu.make_async_copy(h_hbm.at[pl.ds(tok, 1)], xbuf.at[slot_, pl.ds(r, 1)], sem.at[slot_])

    def start_gather(blk, slot_):
        def body(r, c):
            row_copy(blk, slot_, r).start()
            return c
        lax.fori_loop(0, rows, body, 0, unroll=8)

    def wait_gather(blk, slot_):
        def body(r, c):
            row_copy(blk, slot_, r).wait()
            return c
        lax.fori_loop(0, rows, body, 0, unroll=8)

    @pl.when((bi == 0) & (n_used > 0))
    def _():
        start_gather(0, 0)

    @pl.when(bi + 1 < n_used)
    def _():
        start_gather(bi + 1, 1 - slot)

    @pl.when(bi < n_used)
    def _():
        new_expert = (bi == 0) | (blk_e_ref[bi] != blk_e_ref[jnp.maximum(bi - 1, 0)])

        @pl.when(new_expert)
        def _():
            wgb[...] = wg_ref[0].astype(BF16)
            wub[...] = wu_ref[0].astype(BF16)
            wdb[...] = wd_ref[0].astype(BF16)

        wait_gather(bi, slot)
        x = xbuf[slot].astype(BF16)
        hgate = jnp.dot(x, wgb[...], preferred_element_type=F32)
        hup = jnp.dot(x, wub[...], preferred_element_type=F32)
        a = (hgate * _sigmoid(hgate) * hup).astype(BF16)
        y_ref[...] = jnp.dot(a, wdb[...], preferred_element_type=F32) * rw_ref[...]

    @pl.when(bi >= n_used)
    def _():
        y_ref[...] = jnp.zeros(y_ref.shape, y_ref.dtype)


def _experts(blk_e, n_used, row_tok, h, row_w, w_gate, w_up, w_down):
    n_blk = blk_e.shape[0]
    rows = EXPERT_ROWS
    d = h.shape[1]
    hid = w_gate.shape[2]
    grid_spec = pltpu.PrefetchScalarGridSpec(
        num_scalar_prefetch=3,
        grid=(n_blk,),
        in_specs=[pl.BlockSpec(memory_space=pl.ANY),
                  pl.BlockSpec((rows, 1), lambda i, be, nu, rt: (i, 0)),
                  pl.BlockSpec((1, d, hid), lambda i, be, nu, rt: (be[i], 0, 0)),
                  pl.BlockSpec((1, d, hid), lambda i, be, nu, rt: (be[i], 0, 0)),
                  pl.BlockSpec((1, hid, d), lambda i, be, nu, rt: (be[i], 0, 0))],
        out_specs=pl.BlockSpec((rows, d), lambda i, be, nu, rt: (i, 0)),
        scratch_shapes=[pltpu.VMEM((2, rows, d), F32),
                        pltpu.VMEM((d, hid), BF16), pltpu.VMEM((d, hid), BF16), pltpu.VMEM((hid, d), BF16),
                        pltpu.SemaphoreType.DMA((2,))],
    )
    return pl.pallas_call(
        _expert_kernel,
        out_shape=jax.ShapeDtypeStruct((n_blk * rows, d), F32),
        grid_spec=grid_spec,
        compiler_params=_cparams(("arbitrary",)),
        name="moe_experts",
    )(blk_e, n_used, row_tok, h, row_w, w_gate, w_up, w_down)


def _combine_kernel(dest_ref, h_ref, y_hbm, g_ref, b_ref, o_ref, ybuf, sem, *, alpha):
    i = pl.program_id(0)
    n_steps = pl.num_programs(0)
    tm = h_ref.shape[0]
    slot = i % 2

    def row_copy(step_, slot_, r, k):
        src = dest_ref[(step_ * tm + r) * TOP_K + k]
        return pltpu.make_async_copy(y_hbm.at[pl.ds(src, 1)], ybuf.at[slot_, k, pl.ds(r, 1)], sem.at[slot_])

    def start_gather(step_, slot_):
        def body(r, c):
            for k in range(TOP_K):
                row_copy(step_, slot_, r, k).start()
            return c
        lax.fori_loop(0, tm, body, 0, unroll=8)

    def wait_gather(step_, slot_):
        def body(r, c):
            for k in range(TOP_K):
                row_copy(step_, slot_, r, k).wait()
            return c
        lax.fori_loop(0, tm, body, 0, unroll=8)

    @pl.when(i == 0)
    def _():
        start_gather(0, 0)

    @pl.when(i + 1 < n_steps)
    def _():
        start_gather(i + 1, 1 - slot)

    wait_gather(i, slot)
    y = ybuf[slot, 0]
    for k in range(1, TOP_K):
        y = y + ybuf[slot, k]
    o_ref[...] = _layer_norm(alpha * h_ref[...] + y, g_ref[...], b_ref[...])


def _combine(dest, h, ybuf, ln_g, ln_b, alpha, tm):
    n, d = h.shape
    grid_spec = pltpu.PrefetchScalarGridSpec(
        num_scalar_prefetch=1,
        grid=(n // tm,),
        in_specs=[pl.BlockSpec((tm, d), lambda i, ds: (i, 0)),
                  pl.BlockSpec(memory_space=pl.ANY),
                  pl.BlockSpec((1, d), lambda i, ds: (0, 0)), pl.BlockSpec((1, d), lambda i, ds: (0, 0))],
        out_specs=pl.BlockSpec((tm, d), lambda i, ds: (i, 0)),
        scratch_shapes=[pltpu.VMEM((2, TOP_K, tm, d), F32), pltpu.SemaphoreType.DMA((2,))],
    )
    return pl.pallas_call(
        functools.partial(_combine_kernel, alpha=alpha),
        out_shape=jax.ShapeDtypeStruct((n, d), F32),
        grid_spec=grid_spec,
        compiler_params=_cparams(("arbitrary",)),
        name="moe_combine_ln",
    )(dest, h, ybuf, ln_g, ln_b)


def _rope_tables(s):
    inv_freq = ROPE_THETA ** (-2.0 * jnp.arange(HALF_DIM, dtype=F32) / HEAD_DIM)

    def tables(pos):
        ang = pos.astype(F32)[:, None] * inv_freq[None, :]
        return jnp.cos(ang), jnp.sin(ang)

    cos, sin = tables(jnp.arange(s))
    n_rows = s // CMP_STRIDE
    c_end = jnp.arange(n_rows) * CMP_STRIDE + (CMP_BLOCK - 1)
    cos_c, sin_c = tables(c_end)
    full = lambda c: jnp.concatenate([c, c], axis=1)
    signed = lambda sn: jnp.concatenate([-sn, sn], axis=1)
    return cos.T, sin.T, full(cos), signed(sin), full(cos_c), signed(sin_c)


def _overlap_t(s):
    n_rows = s // CMP_STRIDE
    n_cmp = (s - CMP_BLOCK) // CMP_STRIDE + 1
    n_sel = s // SEL_BLOCK
    c_start = np.arange(n_rows) * CMP_STRIDE
    s_start = np.arange(n_sel) * SEL_BLOCK
    ov = ((c_start[None, :] + CMP_BLOCK - 1 >= s_start[:, None])
          & (c_start[None, :] <= s_start[:, None] + SEL_BLOCK - 1)
          & (np.arange(n_rows)[None, :] < n_cmp))
    return jnp.asarray(ov.astype(np.float32), dtype=BF16)


def _to_heads(t, b, s):
    return t.reshape(b, s, N_KV_GROUPS, HEAD_DIM).transpose(0, 2, 1, 3)


def _dispatch_plan(eid, wts, n_experts):
    n = eid.shape[0]
    m = n * TOP_K
    rows_per = EXPERT_ROWS
    flat_e = eid.reshape(-1)
    onehot = (flat_e[:, None] == jnp.arange(n_experts)[None, :]).astype(I32)
    pos = jnp.take_along_axis(jnp.cumsum(onehot, axis=0), flat_e[:, None], axis=1)[:, 0] - 1
    sizes = jnp.sum(onehot, axis=0)
    padded = (sizes + rows_per - 1) // rows_per * rows_per
    ends = jnp.cumsum(padded)
    dest = (ends - padded)[flat_e] + pos
    n_blk = -(-(m + n_experts * (rows_per - 1)) // rows_per)
    rows = n_blk * rows_per
    row_tok = jnp.zeros((rows,), I32).at[dest].set(jnp.arange(m, dtype=I32) // TOP_K)
    row_w = jnp.zeros((rows,), F32).at[dest].set(wts.reshape(-1))
    blk_e = jnp.minimum(jnp.searchsorted(ends, jnp.arange(n_blk) * rows_per, side='right'), n_experts - 1)
    n_used = (ends[-1] // rows_per).reshape(1)
    return dest.astype(I32), row_tok, row_w.reshape(rows, 1), blk_e.astype(I32), n_used.astype(I32)


def kernel(x, w_in, pool_mix, pool_scale, w_pool_proj, w_nsa_proj, cmp_pos_k, cmp_pos_v, cmp_k_w1, cmp_k_w2,
           cmp_v_w1, cmp_v_w2, w_out, ln1_g, ln1_b, router_group_w, router_group_b, router_expert_w,
           router_expert_b, w_gate, w_up, w_down, ln2_g, ln2_b):
    b, s, d = x.shape
    n = b * s
    depth = w_in.shape[0]
    alpha = (2.0 * depth) ** 0.25
    pw = pool_mix.shape[1] * pool_mix.shape[2]
    qw = w_nsa_proj.shape[1]
    kvw = N_KV_GROUPS * HEAD_DIM
    n_groups, _, per_group = router_expert_w.shape[1:]
    n_experts = n_groups * per_group
    gate_w = 3 * (qw // HEAD_DIM)
    assert s % ATTN_TILE == 0 and WINDOW % ATTN_TILE == 0 and ATTN_TILE % SEL_BLOCK == 0
    assert gate_w <= LANES and n_groups + n_experts <= LANES and s // SEL_BLOCK <= HEAD_DIM

    c_q = pw
    c_kv = pw + qw
    c_gate = c_kv + 6 * kvw
    c_merge = c_gate + gate_w
    tn = 768
    width_a = -(-(c_gate + LANES) // tn) * tn
    assert c_gate % LANES == 0

    cos_t, sin_t, cos_f, sin_s, cos_c, sin_c = _rope_tables(s)
    ov_t = _overlap_t(s)
    half = CMP_STRIDE * HEAD_DIM

    h = x.reshape(n, d)
    for l in range(depth):
        wl = w_in[l]
        w_a = jnp.pad(wl[:, :c_merge], ((0, 0), (0, width_a - c_merge))).astype(BF16)
        w_gp = wl[:, c_merge:c_merge + d].astype(BF16)
        w_ga = wl[:, c_merge + d:].astype(BF16)

        proj = _project(h, w_a, _tile(n, 1024), tn)

        mixed = _pool_mixer(proj, pool_mix[l].astype(BF16), pool_scale[l].reshape(1, pw), b, s, pw, _tile(s, 512))

        def kv(j):
            return _to_heads(proj[:, c_kv + j * kvw:c_kv + (j + 1) * kvw], b, s)

        n_rows = s // CMP_STRIDE
        kcmp, vcmp_t = _compress(
            kv(0).reshape(b, N_KV_GROUPS, n_rows, half), kv(1).reshape(b, N_KV_GROUPS, n_rows, half),
            cmp_pos_k[l].reshape(1, CMP_BLOCK * HEAD_DIM), cmp_pos_v[l].reshape(1, CMP_BLOCK * HEAD_DIM),
            cmp_k_w1[l].astype(BF16), cmp_k_w2[l].astype(BF16), cmp_v_w1[l].astype(BF16),
            cmp_v_w2[l].T.astype(BF16), cos_c, sin_c)
        ksx, kwx = _key_rope(kv(2), kv(4), cos_f, sin_s)
        nq = s // ATTN_TILE

        def v_tiles(v):
            return v.reshape(b, N_KV_GROUPS, nq, ATTN_TILE, HEAD_DIM).transpose(0, 1, 2, 4, 3).astype(BF16)

        attn = _attention(
            proj, cos_t, sin_t, kcmp, vcmp_t,
            ksx.reshape(b, N_KV_GROUPS, nq, ATTN_TILE, 2 * HEAD_DIM), v_tiles(kv(3)),
            kwx.reshape(b, N_KV_GROUPS, nq, ATTN_TILE, HEAD_DIM), v_tiles(kv(5)),
            ov_t, b=b, s=s, pw=pw, qw=qw, gate_col=c_gate)

        h = _merge_out(h, mixed, attn, w_pool_proj[l].astype(BF16), w_nsa_proj[l].astype(BF16), w_gp, w_ga,
                       w_out[l].astype(BF16), ln1_g[l].reshape(1, d), ln1_b[l].reshape(1, d), alpha,
                       _tile(n, 512), _tile(d, 512))

        w_r = jnp.concatenate([router_group_w[l], router_expert_w[l].transpose(1, 0, 2).reshape(d, n_experts)], axis=1)
        w_r = jnp.pad(w_r, ((0, 0), (0, LANES - w_r.shape[1])))
        w_r_hi = w_r.astype(BF16)
        w_r_lo = (w_r - w_r_hi.astype(F32)).astype(BF16)
        b_r = jnp.pad(jnp.concatenate([router_group_b[l], router_expert_b[l].reshape(-1)]),
                      (0, LANES - n_groups - n_experts)).reshape(1, LANES)
        eid_l, wts_l = _router(h, w_r_hi, w_r_lo, b_r, n_groups, per_group, _tile(n, 512))
        dest, row_tok, row_w, blk_e, n_used = _dispatch_plan(eid_l[:, :TOP_K], wts_l[:, :TOP_K], n_experts)

        ybuf = _experts(blk_e, n_used, row_tok, h, row_w, w_gate[l], w_up[l], w_down[l])
        h = _combine(dest, h, ybuf, ln2_g[l].reshape(1, d), ln2_b[l].reshape(1, d), alpha, _tile(n, 256))
    return h.reshape(b, s, d)
```

```python
import functools
import math

import numpy as np
import jax
import jax.numpy as jnp
from jax import lax
from jax.experimental import pallas as pl
from jax.experimental.pallas import tpu as pltpu

F32 = jnp.float32
BF16 = jnp.bfloat16
I32 = jnp.int32

POOL_WINDOWS = (2, 4, 8, 16)
POOL_GROUPS = 4
POOL_HALO = 16
HEAD_DIM = 64
HALF_DIM = HEAD_DIM // 2
N_KV_GROUPS = 4
CMP_BLOCK = 32
CMP_STRIDE = 16
SEL_BLOCK = 64
SEL_TOP_N = 16
WINDOW = 512
ROPE_THETA = 10000.0
FORCE_SCORE = 1e6
TOP_K = 2
LN_EPS = 1e-5
NEG = -1e30

LANES = 128
SUBLANES = 8
VMEM_BYTES_V7X = 64 * 1024 * 1024
VMEM_LIMIT = VMEM_BYTES_V7X - 8 * 1024 * 1024

ATTN_TILE = 256
EXPERT_ROWS = 256


def _cparams(sem, vmem=VMEM_LIMIT):
    return pltpu.CompilerParams(dimension_semantics=sem, vmem_limit_bytes=vmem)


def _tile(n, pref):
    t = min(n, pref)
    while n % t:
        t //= 2
    return t


def _proj_kernel(x_ref, w_ref, o_ref, xb_ref):
    @pl.when(pl.program_id(1) == 0)
    def _():
        xb_ref[...] = x_ref[...].astype(BF16)

    o_ref[...] = jnp.dot(xb_ref[...], w_ref[...], preferred_element_type=F32)


def _project(x, w, tm, tn):
    n, d = x.shape
    cols = w.shape[1]
    return pl.pallas_call(
        _proj_kernel,
        out_shape=jax.ShapeDtypeStruct((n, cols), F32),
        grid=(n // tm, cols // tn),
        in_specs=[pl.BlockSpec((tm, d), lambda i, j: (i, 0)),
                  pl.BlockSpec((d, tn), lambda i, j: (0, j))],
        out_specs=pl.BlockSpec((tm, tn), lambda i, j: (i, j)),
        scratch_shapes=[pltpu.VMEM((tm, d), BF16)],
        compiler_params=_cparams(("arbitrary", "arbitrary")),
        name="in_proj",
    )(x, w)


def _pool_kernel(u_ref, halo_ref, mix_ref, scale_ref, o_ref, ext_ref):
    i = pl.program_id(1)
    ts = u_ref.shape[0]
    gd = mix_ref.shape[1]
    ext_ref[POOL_HALO:, :] = u_ref[...]
    ext_ref[:POOL_HALO, :] = jnp.where(i == 0, 0.0, halo_ref[...])
    t = i * ts + lax.broadcasted_iota(I32, (ts, gd), 0)
    for g, w in enumerate(POOL_WINDOWS):
        cols = slice(g * gd, (g + 1) * gd)
        s = ext_ref[:, cols]
        k = 1
        while k < w:
            s = s + pltpu.roll(s, k, axis=0)
            k *= 2
        cnt = jnp.minimum(t + 1, w).astype(F32)
        pooled = s[POOL_HALO:, :] / cnt - u_ref[:, cols]
        mixed = jnp.dot(pooled.astype(BF16), mix_ref[g], preferred_element_type=F32)
        o_ref[:, cols] = (mixed * scale_ref[:, cols]).astype(BF16)


def _pool_mixer(proj, pool_mix_b, pool_scale, b, s, pw, ts):
    ns = s // ts
    hb = ts // POOL_HALO
    return pl.pallas_call(
        _pool_kernel,
        out_shape=jax.ShapeDtypeStruct((b * s, pw), BF16),
        grid=(b, ns),
        in_specs=[pl.BlockSpec((ts, pw), lambda bi, i: (bi * ns + i, 0)),
                  pl.BlockSpec((POOL_HALO, pw), lambda bi, i: (jnp.maximum((bi * ns + i) * hb - 1, 0), 0)),
                  pl.BlockSpec(pool_mix_b.shape, lambda bi, i: (0, 0, 0)),
                  pl.BlockSpec((1, pw), lambda bi, i: (0, 0))],
        out_specs=pl.BlockSpec((ts, pw), lambda bi, i: (bi * ns + i, 0)),
        scratch_shapes=[pltpu.VMEM((POOL_HALO + ts, pw), F32)],
        compiler_params=_cparams(("arbitrary", "arbitrary")),
        name="pool_mixer",
    )(proj, proj, pool_mix_b, pool_scale)


def _gelu_tanh(x):
    return 0.5 * x * (1.0 + jnp.tanh(math.sqrt(2.0 / math.pi) * (x + 0.044715 * (x * x * x))))


def _swap_halves(x):
    return jnp.concatenate([x[:, HALF_DIM:], x[:, :HALF_DIM]], axis=1)


def _cmp_kernel(rk_ref, rv_ref, pk_ref, pv_ref, kw1_ref, kw2_ref, vw1_ref, vw2t_ref, cos_ref, sin_ref,
                ko_ref, vo_ref):
    half = rk_ref.shape[3]
    nrow = rk_ref.shape[2]

    def hidden(r_ref, p_ref, w1_ref):
        r = r_ref[0, 0]
        r_next = pltpu.roll(r, nrow - 1, axis=0)
        a = (r + p_ref[:, :half]).astype(BF16)
        bb = (r_next + p_ref[:, half:]).astype(BF16)
        h = (jnp.dot(a, w1_ref[:half, :], preferred_element_type=F32)
             + jnp.dot(bb, w1_ref[half:, :], preferred_element_type=F32))
        return _gelu_tanh(h).astype(BF16)

    k = jnp.dot(hidden(rk_ref, pk_ref, kw1_ref), kw2_ref[...], preferred_element_type=F32)
    k = k * cos_ref[...] + _swap_halves(k) * sin_ref[...]
    ko_ref[0, 0] = k.astype(BF16)
    hv = hidden(rv_ref, pv_ref, vw1_ref)
    vt = lax.dot_general(vw2t_ref[...], hv, (((1,), (1,)), ((), ())), preferred_element_type=F32)
    vo_ref[0, 0] = vt.astype(BF16)


def _compress(rk, rv, pk, pv, kw1, kw2, vw1, vw2t, cos_c, sin_c):
    b, g, nrow, half = rk.shape
    blk = pl.BlockSpec((1, 1, nrow, half), lambda bi, gi: (bi, gi, 0, 0))

    def whole(a):
        return pl.BlockSpec(a.shape, lambda bi, gi: (0,) * a.ndim)

    return pl.pallas_call(
        _cmp_kernel,
        out_shape=(jax.ShapeDtypeStruct((b, g, nrow, HEAD_DIM), BF16),
                   jax.ShapeDtypeStruct((b, g, HEAD_DIM, nrow), BF16)),
        grid=(b, g),
        in_specs=[blk, blk, whole(pk), whole(pv), whole(kw1), whole(kw2), whole(vw1), whole(vw2t),
                  whole(cos_c), whole(sin_c)],
        out_specs=(pl.BlockSpec((1, 1, nrow, HEAD_DIM), lambda bi, gi: (bi, gi, 0, 0)),
                   pl.BlockSpec((1, 1, HEAD_DIM, nrow), lambda bi, gi: (bi, gi, 0, 0))),
        compiler_params=_cparams(("arbitrary", "arbitrary")),
        name="compress_kv",
    )(rk, rv, pk, pv, kw1, kw2, vw1, vw2t, cos_c, sin_c)


def _krope_kernel(ks_ref, kw_ref, cos_ref, sin_ref, kso_ref, kwo_ref):
    s = ks_ref.shape[2]
    cos = cos_ref[...]
    sin = sin_ref[...]
    ks = ks_ref[0, 0]
    kso_ref[0, 0, :, :HEAD_DIM] = (ks * cos + _swap_halves(ks) * sin).astype(BF16)
    blk = lax.broadcasted_iota(I32, (s, HEAD_DIM), 0) // SEL_BLOCK
    col = lax.broadcasted_iota(I32, (s, HEAD_DIM), 1)
    kso_ref[0, 0, :, HEAD_DIM:] = jnp.where(blk == col, 1.0, 0.0).astype(BF16)
    kw = kw_ref[0, 0]
    kwo_ref[0, 0] = (kw * cos + _swap_halves(kw) * sin).astype(BF16)


def _key_rope(ks, kw, cos_f, sin_s):
    b, g, s, hd = ks.shape
    blk = pl.BlockSpec((1, 1, s, hd), lambda bi, gi: (bi, gi, 0, 0))
    tab = pl.BlockSpec((s, hd), lambda bi, gi: (0, 0))
    return pl.pallas_call(
        _krope_kernel,
        out_shape=(jax.ShapeDtypeStruct((b, g, s, 2 * hd), BF16), jax.ShapeDtypeStruct((b, g, s, hd), BF16)),
        grid=(b, g),
        in_specs=[blk, blk, tab, tab],
        out_specs=(pl.BlockSpec((1, 1, s, 2 * hd), lambda bi, gi: (bi, gi, 0, 0)), blk),
        compiler_params=_cparams(("arbitrary", "arbitrary")),
        name="key_rope",
    )(ks, kw, cos_f, sin_s)


def _attn_kernel(q_ref, cos_ref, sin_ref, gate_ref, kc_ref, vct_ref, ks_ref, vst_ref, kw_ref, vwt_ref, ovt_ref,
                 o_ref, qa_s, g_s, imp_s, m_s, l_s, c_s, acc_s, out_s, sc_s, pc_s, psum_s, s_s, p_s, *, hg, n_heads):
    g = pl.program_id(1)
    i = pl.program_id(2)
    tq = q_ref.shape[0]
    tk = tq
    ncp = kc_ref.shape[2]
    nsel = ovt_ref.shape[0]
    scale = HEAD_DIM ** -0.5

    qt = q_ref[...].T
    cos = cos_ref[...]
    sin = sin_ref[...]
    for h in range(hg):
        x1 = qt[h * HEAD_DIM:h * HEAD_DIM + HALF_DIM]
        x2 = qt[h * HEAD_DIM + HALF_DIM:(h + 1) * HEAD_DIM]
        qa_s[h, :HALF_DIM] = ((x1 * cos - x2 * sin) * scale).astype(BF16)
        qa_s[h, HALF_DIM:HEAD_DIM] = ((x2 * cos + x1 * sin) * scale).astype(BF16)
    g_s[...] = gate_ref[...].T

    def gate(branch, h):
        row = g_s[pl.ds(branch * n_heads + g * hg + h, 1), :]
        return 1.0 / (1.0 + jnp.exp(-row))

    chunks = [slice(c * LANES, (c + 1) * LANES) for c in range(tq // LANES)]

    def qk(dst, kx, qrows):
        for h in range(hg):
            dst[h] = jnp.dot(kx, qa_s[h, :qrows], preferred_element_type=F32)

    qk(sc_s, kc_ref[0, 0], HEAD_DIM)
    c_end = lax.broadcasted_iota(I32, (ncp, LANES), 0) * CMP_STRIDE + (CMP_BLOCK - 1)
    for c, cols in enumerate(chunks):
        cmask = c_end <= i * tq + c * LANES + lax.broadcasted_iota(I32, (ncp, LANES), 1)
        for h in range(hg):
            sm = jnp.where(cmask, sc_s[h, :, cols], NEG)
            m = jnp.max(sm, axis=0, keepdims=True)
            p = jnp.where(cmask, jnp.exp(sm - m), 0.0)
            den = jnp.maximum(jnp.sum(p, axis=0, keepdims=True), 1e-30)
            pn = p / den
            psum_s[:, cols] = pn if h == 0 else psum_s[:, cols] + pn
            pc_s[h, :, cols] = pn.astype(BF16)
    for h in range(hg):
        oc = jnp.dot(vct_ref[0, 0], pc_s[h], preferred_element_type=F32)
        out_s[h * HEAD_DIM:(h + 1) * HEAD_DIM, :] = oc * gate(0, h)

    psum = psum_s[...]
    p_hi = psum.astype(BF16)
    p_lo = (psum - p_hi.astype(F32)).astype(BF16)
    imp = (jnp.dot(ovt_ref[...], p_hi, preferred_element_type=F32)
           + jnp.dot(ovt_ref[...], p_lo, preferred_element_type=F32))
    jb = lax.broadcasted_iota(I32, (nsel, tq), 0)
    cur = (i * tq + lax.broadcasted_iota(I32, (nsel, tq), 1)) // SEL_BLOCK
    forced = (jb == 0) | (jb == cur) | (jb == cur - 1)
    imp = jnp.where(forced, FORCE_SCORE, jnp.where(jb > cur, -FORCE_SCORE, imp))
    imp_s[...] = imp
    nrb = nsel // SUBLANES
    blocks = [imp[r * SUBLANES:(r + 1) * SUBLANES] for r in range(nrb)]
    counts = [jnp.zeros((SUBLANES, tq), F32) for _ in range(nrb)]
    sub = lax.broadcasted_iota(I32, (SUBLANES, tq), 0)
    for jp in range(nsel):
        row = jnp.broadcast_to(imp_s[jp:jp + 1, :], (SUBLANES, tq))
        for r in range(nrb):
            if r * SUBLANES > jp:
                inc = jnp.where(row >= blocks[r], 1.0, 0.0)
            elif r * SUBLANES + SUBLANES - 1 <= jp:
                inc = jnp.where(row > blocks[r], 1.0, 0.0)
            else:
                tie = jnp.where(sub > jp - r * SUBLANES, 1.0, 0.0)
                inc = jnp.where(row > blocks[r], 1.0, jnp.where(row >= blocks[r], tie, 0.0))
            counts[r] = counts[r] + inc
    top_n = min(SEL_TOP_N, nsel)
    bias = jnp.concatenate([jnp.where(cnt < top_n, 0.0, NEG) for cnt in counts], axis=0).astype(BF16)
    for h in range(hg):
        qa_s[h, HEAD_DIM:HEAD_DIM + nsel] = bias
        if nsel < HEAD_DIM:
            qa_s[h, HEAD_DIM + nsel:] = jnp.zeros((HEAD_DIM - nsel, tq), BF16)

    def reset():
        m_s[...] = jnp.full(m_s.shape, NEG, F32)
        l_s[...] = jnp.zeros(l_s.shape, F32)
        acc_s[...] = jnp.zeros(acc_s.shape, F32)

    krow = lax.broadcasted_iota(I32, (tk, LANES), 0)
    lane = lax.broadcasted_iota(I32, (tk, LANES), 1)

    def softmax_pv(src, vt, kind, skip=None):
        off = None if skip is None else jnp.where(skip, tk, 0)
        for c, cols in enumerate(chunks):
            qcol = lane + c * LANES
            if kind == 'causal':
                mask = krow <= qcol
            elif kind == 'older':
                mask = krow > (qcol if off is None else qcol + off)
            else:
                mask = None if off is None else krow >= off
            for h in range(hg):
                s = src[h, :, cols]
                if mask is not None:
                    s = jnp.where(mask, s, NEG)
                m_old = m_s[h, :, cols]
                m_new = jnp.maximum(m_old, jnp.max(s, axis=0, keepdims=True))
                p = jnp.exp(s - m_new)
                corr = jnp.exp(m_old - m_new)
                l_s[h, :, cols] = l_s[h, :, cols] * corr + jnp.sum(p, axis=0, keepdims=True)
                m_s[h, :, cols] = m_new
                c_s[h, :, cols] = corr
                p_s[h, :, cols] = p.astype(BF16)
        for h in range(hg):
            acc_s[h] = acc_s[h] * c_s[h] + jnp.dot(vt, p_s[h], preferred_element_type=F32)

    def finish(branch):
        for h in range(hg):
            rows = slice(h * HEAD_DIM, (h + 1) * HEAD_DIM)
            out_s[rows, :] = out_s[rows, :] + acc_s[h] * (gate(branch, h) / l_s[h])

    s_a, s_b, s_c = s_s.at[0], s_s.at[1], s_s.at[2]
    sel_q = 2 * HEAD_DIM

    reset()
    qk(s_a, ks_ref[0, 0, 0], sel_q)

    def sel_pair(pi, carry):
        kt = 2 * pi
        qk(s_b, ks_ref[0, 0, kt + 1], sel_q)
        softmax_pv(s_a, vst_ref[0, 0, kt], None)
        qk(s_a, ks_ref[0, 0, kt + 2], sel_q)
        softmax_pv(s_b, vst_ref[0, 0, kt + 1], None)
        return carry

    lax.fori_loop(0, i // 2, sel_pair, 0)

    @pl.when(i % 2 == 0)
    def _():
        softmax_pv(s_a, vst_ref[0, 0, i], 'causal')

    @pl.when(i % 2 == 1)
    def _():
        qk(s_b, ks_ref[0, 0, i], sel_q)
        softmax_pv(s_a, vst_ref[0, 0, i - 1], None)
        softmax_pv(s_b, vst_ref[0, 0, i], 'causal')

    finish(1)

    reset()
    nwt = WINDOW // tk
    assert nwt == 2
    t_old = jnp.maximum(i - 2, 0)
    t_mid = jnp.maximum(i - 1, 0)
    qk(s_a, kw_ref[0, 0, t_old], HEAD_DIM)
    qk(s_b, kw_ref[0, 0, t_mid], HEAD_DIM)
    qk(s_c, kw_ref[0, 0, i], HEAD_DIM)
    softmax_pv(s_a, vwt_ref[0, 0, t_old], 'older', skip=i < 2)
    softmax_pv(s_b, vwt_ref[0, 0, t_mid], None, skip=i < 1)
    softmax_pv(s_c, vwt_ref[0, 0, i], 'causal')
    finish(2)

    o_ref[...] = out_s[...].T.astype(o_ref.dtype)


def _attention(proj, cos_t, sin_t, kcmp, vcmp_t, ksx, vs_t, kwx, vw_t, ov_t, *, b, s, pw, qw, gate_col):
    tq = ATTN_TILE
    nq = s // tq
    g = N_KV_GROUPS
    gw = qw // g
    hg = gw // HEAD_DIM
    n_heads = qw // HEAD_DIM
    ncp = kcmp.shape[2]
    nsel = ov_t.shape[0]
    q_blk0 = pw // gw
    kern = functools.partial(_attn_kernel, hg=hg, n_heads=n_heads)
    kv5 = lambda bi, gi, i: (bi, gi, 0, 0, 0)
    kv4 = lambda bi, gi, i: (bi, gi, 0, 0)
    return pl.pallas_call(
        kern,
        out_shape=jax.ShapeDtypeStruct((b * s, qw), BF16),
        grid=(b, g, nq),
        in_specs=[pl.BlockSpec((tq, gw), lambda bi, gi, i: (bi * nq + i, q_blk0 + gi)),
                  pl.BlockSpec((HALF_DIM, tq), lambda bi, gi, i: (0, i)),
                  pl.BlockSpec((HALF_DIM, tq), lambda bi, gi, i: (0, i)),
                  pl.BlockSpec((tq, LANES), lambda bi, gi, i: (bi * nq + i, gate_col // LANES)),
                  pl.BlockSpec((1, 1, ncp, HEAD_DIM), kv4),
                  pl.BlockSpec((1, 1, HEAD_DIM, ncp), kv4),
                  pl.BlockSpec((1, 1, nq, tq, 2 * HEAD_DIM), kv5),
                  pl.BlockSpec((1, 1, nq, HEAD_DIM, tq), kv5),
                  pl.BlockSpec((1, 1, nq, tq, HEAD_DIM), kv5),
                  pl.BlockSpec((1, 1, nq, HEAD_DIM, tq), kv5),
                  pl.BlockSpec(ov_t.shape, lambda bi, gi, i: (0, 0))],
        out_specs=pl.BlockSpec((tq, gw), lambda bi, gi, i: (bi * nq + i, gi)),
        scratch_shapes=[pltpu.VMEM((hg, 2 * HEAD_DIM, tq), BF16),
                        pltpu.VMEM((LANES, tq), F32),
                        pltpu.VMEM((nsel, tq), F32),
                        pltpu.VMEM((hg, 1, tq), F32),
                        pltpu.VMEM((hg, 1, tq), F32),
                        pltpu.VMEM((hg, 1, tq), F32),
                        pltpu.VMEM((hg, HEAD_DIM, tq), F32),
                        pltpu.VMEM((gw, tq), F32),
                        pltpu.VMEM((hg, ncp, tq), F32),
                        pltpu.VMEM((hg, ncp, tq), BF16),
                        pltpu.VMEM((ncp, tq), F32),
                        pltpu.VMEM((3, hg, tq, tq), F32),
                        pltpu.VMEM((hg, tq, tq), BF16)],
        compiler_params=_cparams(("arbitrary", "arbitrary", "arbitrary")),
        name="nsa_attention",
    )(proj, cos_t, sin_t, proj, kcmp, vcmp_t, ksx, vs_t, kwx, vw_t, ov_t)


def _layer_norm(r, g, b):
    mu = jnp.mean(r, axis=-1, keepdims=True)
    d = r - mu
    var = jnp.mean(d * d, axis=-1, keepdims=True)
    return d * lax.rsqrt(var + LN_EPS) * g + b


def _sigmoid(x):
    return 1.0 / (1.0 + jnp.exp(-x))


def _merge_kernel(x_ref, pm_ref, at_ref, wpp_ref, wnp_ref, wgp_ref, wga_ref, wo_ref, g_ref, b_ref, o_ref,
                  xb_ref, acc_ref, *, alpha):
    c = pl.program_id(1)

    @pl.when(c == 0)
    def _():
        xb_ref[...] = x_ref[...].astype(BF16)
        acc_ref[...] = jnp.zeros(acc_ref.shape, F32)

    xb = xb_ref[...]
    y_pool = jnp.dot(pm_ref[...], wpp_ref[...], preferred_element_type=F32)
    y_attn = jnp.dot(at_ref[...], wnp_ref[...], preferred_element_type=F32)
    g_pool = _sigmoid(jnp.dot(xb, wgp_ref[...], preferred_element_type=F32))
    g_attn = _sigmoid(jnp.dot(xb, wga_ref[...], preferred_element_type=F32))
    z = g_pool * y_pool + g_attn * y_attn
    acc_ref[...] += jnp.dot(z.astype(BF16), wo_ref[...], preferred_element_type=F32)

    @pl.when(c == pl.num_programs(1) - 1)
    def _():
        o_ref[...] = _layer_norm(alpha * x_ref[...] + acc_ref[...], g_ref[...], b_ref[...])


def _merge_out(x, pm, at, wpp, wnp, wgp, wga, wo, ln_g, ln_b, alpha, tm, ck):
    n, d = x.shape
    pw = pm.shape[1]
    qw = at.shape[1]
    row = lambda i, c: (i, 0)
    colc = lambda i, c: (0, c)
    return pl.pallas_call(
        functools.partial(_merge_kernel, alpha=alpha),
        out_shape=jax.ShapeDtypeStruct((n, d), F32),
        grid=(n // tm, d // ck),
        in_specs=[pl.BlockSpec((tm, d), row), pl.BlockSpec((tm, pw), row), pl.BlockSpec((tm, qw), row),
                  pl.BlockSpec((pw, ck), colc), pl.BlockSpec((qw, ck), colc),
                  pl.BlockSpec((d, ck), colc), pl.BlockSpec((d, ck), colc),
                  pl.BlockSpec((ck, d), lambda i, c: (c, 0)),
                  pl.BlockSpec((1, d), lambda i, c: (0, 0)), pl.BlockSpec((1, d), lambda i, c: (0, 0))],
        out_specs=pl.BlockSpec((tm, d), row),
        scratch_shapes=[pltpu.VMEM((tm, d), BF16), pltpu.VMEM((tm, d), F32)],
        compiler_params=_cparams(("arbitrary", "arbitrary")),
        name="merge_out_ln",
    )(x, pm, at, wpp, wnp, wgp, wga, wo, ln_g, ln_b)


def _router_kernel(h_ref, whi_ref, wlo_ref, b_ref, e_ref, w_ref, *, n_groups, per_group):
    h = h_ref[...]
    h_hi = h.astype(BF16)
    h_lo = (h - h_hi.astype(F32)).astype(BF16)
    logits = (jnp.dot(h_hi, whi_ref[...], preferred_element_type=F32)
              + jnp.dot(h_hi, wlo_ref[...], preferred_element_type=F32)
              + jnp.dot(h_lo, whi_ref[...], preferred_element_type=F32)) + b_ref[...]
    lane = lax.broadcasted_iota(I32, logits.shape, 1)
    far = LANES

    def first_argmax(v, vmax):
        return jnp.min(jnp.where(v == vmax, lane, far), axis=-1, keepdims=True)

    gl = jnp.where(lane < n_groups, logits, NEG)
    gmax = jnp.max(gl, axis=-1, keepdims=True)
    grp = first_argmax(gl, gmax)
    gsum = jnp.sum(jnp.where(lane < n_groups, jnp.exp(gl - gmax), 0.0), axis=-1, keepdims=True)
    g_gate = 1.0 / gsum
    lo = n_groups + grp * per_group
    el = jnp.where((lane >= lo) & (lane < lo + per_group), logits, NEG)
    v1 = jnp.max(el, axis=-1, keepdims=True)
    i1 = first_argmax(el, v1)
    el2 = jnp.where(lane == i1, NEG, el)
    v2 = jnp.max(el2, axis=-1, keepdims=True)
    i2 = first_argmax(el2, v2)
    e21 = jnp.exp(v2 - v1)
    w1 = g_gate / (1.0 + e21)
    w2 = g_gate * e21 / (1.0 + e21)
    e_ref[...] = jnp.where(lane == 0, i1 - n_groups, jnp.where(lane == 1, i2 - n_groups, 0))
    w_ref[...] = jnp.where(lane == 0, w1, jnp.where(lane == 1, w2, 0.0))


def _router(h, w_hi, w_lo, bias, n_groups, per_group, tm):
    n, d = h.shape
    return pl.pallas_call(
        functools.partial(_router_kernel, n_groups=n_groups, per_group=per_group),
        out_shape=(jax.ShapeDtypeStruct((n, LANES), I32), jax.ShapeDtypeStruct((n, LANES), F32)),
        grid=(n // tm,),
        in_specs=[pl.BlockSpec((tm, d), lambda i: (i, 0)),
                  pl.BlockSpec((d, LANES), lambda i: (0, 0)), pl.BlockSpec((d, LANES), lambda i: (0, 0)),
                  pl.BlockSpec((1, LANES), lambda i: (0, 0))],
        out_specs=(pl.BlockSpec((tm, LANES), lambda i: (i, 0)), pl.BlockSpec((tm, LANES), lambda i: (i, 0))),
        compiler_params=_cparams(("arbitrary",)),
        name="moe_router",
    )(h, w_hi, w_lo, bias)


def _expert_kernel(blk_e_ref, n_used_ref, row_tok_ref, h_hbm, rw_ref, wg_ref, wu_ref, wd_ref, y_ref,
                   xbuf, wgb, wub, wdb, sem):
    bi = pl.program_id(0)
    rows = xbuf.shape[1]
    n_used = n_used_ref[0]
    slot = bi % 2

    def row_copy(blk, slot_, r):
        tok = row_tok_ref[blk * rows + r]
        return pltpu.make_async_copy(h_hbm.at[pl.ds(tok, 1)], xbuf.at[slot_, pl.ds(r, 1)], sem.at[slot_])

    def start_gather(blk, slot_):
        def body(r, c):
            row_copy(blk, slot_, r).start()
            return c
        lax.fori_loop(0, rows, body, 0, unroll=8)

    def wait_gather(blk, slot_):
        def body(r, c):
            row_copy(blk, slot_, r).wait()
            return c
        lax.fori_loop(0, rows, body, 0, unroll=8)

    @pl.when((bi == 0) & (n_used > 0))
    def _():
        start_gather(0, 0)

    @pl.when(bi + 1 < n_used)
    def _():
        start_gather(bi + 1, 1 - slot)

    @pl.when(bi < n_used)
    def _():
        new_expert = (bi == 0) | (blk_e_ref[bi] != blk_e_ref[jnp.maximum(bi - 1, 0)])

        @pl.when(new_expert)
        def _():
            wgb[...] = wg_ref[0].astype(BF16)
            wub[...] = wu_ref[0].astype(BF16)
            wdb[...] = wd_ref[0].astype(BF16)

        wait_gather(bi, slot)
        x = xbuf[slot].astype(BF16)
        hgate = jnp.dot(x, wgb[...], preferred_element_type=F32)
        hup = jnp.dot(x, wub[...], preferred_element_type=F32)
        a = (hgate * _sigmoid(hgate) * hup).astype(BF16)
        y_ref[...] = jnp.dot(a, wdb[...], preferred_element_type=F32) * rw_ref[...]

    @pl.when(bi >= n_used)
    def _():
        y_ref[...] = jnp.zeros(y_ref.shape, y_ref.dtype)


def _experts(blk_e, n_used, row_tok, h, row_w, w_gate, w_up, w_down):
    n_blk = blk_e.shape[0]
    rows = EXPERT_ROWS
    d = h.shape[1]
    hid = w_gate.shape[2]
    grid_spec = pltpu.PrefetchScalarGridSpec(
        num_scalar_prefetch=3,
        grid=(n_blk,),
        in_specs=[pl.BlockSpec(memory_space=pl.ANY),
                  pl.BlockSpec((rows, 1), lambda i, be, nu, rt: (i, 0)),
                  pl.BlockSpec((1, d, hid), lambda i, be, nu, rt: (be[i], 0, 0)),
                  pl.BlockSpec((1, d, hid), lambda i, be, nu, rt: (be[i], 0, 0)),
                  pl.BlockSpec((1, hid, d), lambda i, be, nu, rt: (be[i], 0, 0))],
        out_specs=pl.BlockSpec((rows, d), lambda i, be, nu, rt: (i, 0)),
        scratch_shapes=[pltpu.VMEM((2, rows, d), F32),
                        pltpu.VMEM((d, hid), BF16), pltpu.VMEM((d, hid), BF16), pltpu.VMEM((hid, d), BF16),
                        pltpu.SemaphoreType.DMA((2,))],
    )
    return pl.pallas_call(
        _expert_kernel,
        out_shape=jax.ShapeDtypeStruct((n_blk * rows, d), F32),
        grid_spec=grid_spec,
        compiler_params=_cparams(("arbitrary",)),
        name="moe_experts",
    )(blk_e, n_used, row_tok, h, row_w, w_gate, w_up, w_down)


def _combine_kernel(dest_ref, h_ref, y_hbm, g_ref, b_ref, o_ref, ybuf, sem, *, alpha):
    i = pl.program_id(0)
    n_steps = pl.num_programs(0)
    tm = h_ref.shape[0]
    slot = i % 2

    def row_copy(step_, slot_, r, k):
        src = dest_ref[(step_ * tm + r) * TOP_K + k]
        return pltpu.make_async_copy(y_hbm.at[pl.ds(src, 1)], ybuf.at[slot_, k, pl.ds(r, 1)], sem.at[slot_])

    def start_gather(step_, slot_):
        def body(r, c):
            for k in range(TOP_K):
                row_copy(step_, slot_, r, k).start()
            return c
        lax.fori_loop(0, tm, body, 0, unroll=8)

    def wait_gather(step_, slot_):
        def body(r, c):
            for k in range(TOP_K):
                row_copy(step_, slot_, r, k).wait()
            return c
        lax.fori_loop(0, tm, body, 0, unroll=8)

    @pl.when(i == 0)
    def _():
        start_gather(0, 0)

    @pl.when(i + 1 < n_steps)
    def _():
        start_gather(i + 1, 1 - slot)

    wait_gather(i, slot)
    y = ybuf[slot, 0]
    for k in range(1, TOP_K):
        y = y + ybuf[slot, k]
    o_ref[...] = _layer_norm(alpha * h_ref[...] + y, g_ref[...], b_ref[...])


def _combine(dest, h, ybuf, ln_g, ln_b, alpha, tm):
    n, d = h.shape
    grid_spec = pltpu.PrefetchScalarGridSpec(
        num_scalar_prefetch=1,
        grid=(n // tm,),
        in_specs=[pl.BlockSpec((tm, d), lambda i, ds: (i, 0)),
                  pl.BlockSpec(memory_space=pl.ANY),
                  pl.BlockSpec((1, d), lambda i, ds: (0, 0)), pl.BlockSpec((1, d), lambda i, ds: (0, 0))],
        out_specs=pl.BlockSpec((tm, d), lambda i, ds: (i, 0)),
        scratch_shapes=[pltpu.VMEM((2, TOP_K, tm, d), F32), pltpu.SemaphoreType.DMA((2,))],
    )
    return pl.pallas_call(
        functools.partial(_combine_kernel, alpha=alpha),
        out_shape=jax.ShapeDtypeStruct((n, d), F32),
        grid_spec=grid_spec,
        compiler_params=_cparams(("arbitrary",)),
        name="moe_combine_ln",
    )(dest, h, ybuf, ln_g, ln_b)


def _rope_tables(s):
    inv_freq = ROPE_THETA ** (-2.0 * jnp.arange(HALF_DIM, dtype=F32) / HEAD_DIM)

    def tables(pos):
        ang = pos.astype(F32)[:, None] * inv_freq[None, :]
        return jnp.cos(ang), jnp.sin(ang)

    cos, sin = tables(jnp.arange(s))
    n_rows = s // CMP_STRIDE
    c_end = jnp.arange(n_rows) * CMP_STRIDE + (CMP_BLOCK - 1)
    cos_c, sin_c = tables(c_end)
    full = lambda c: jnp.concatenate([c, c], axis=1)
    signed = lambda sn: jnp.concatenate([-sn, sn], axis=1)
    return cos.T, sin.T, full(cos), signed(sin), full(cos_c), signed(sin_c)


def _overlap_t(s):
    n_rows = s // CMP_STRIDE
    n_cmp = (s - CMP_BLOCK) // CMP_STRIDE + 1
    n_sel = s // SEL_BLOCK
    c_start = np.arange(n_rows) * CMP_STRIDE
    s_start = np.arange(n_sel) * SEL_BLOCK
    ov = ((c_start[None, :] + CMP_BLOCK - 1 >= s_start[:, None])
          & (c_start[None, :] <= s_start[:, None] + SEL_BLOCK - 1)
          & (np.arange(n_rows)[None, :] < n_cmp))
    return jnp.asarray(ov.astype(np.float32), dtype=BF16)


def _to_heads(t, b, s):
    return t.reshape(b, s, N_KV_GROUPS, HEAD_DIM).transpose(0, 2, 1, 3)


def _dispatch_plan(eid, wts, n_experts):
    n = eid.shape[0]
    m = n * TOP_K
    rows_per = EXPERT_ROWS
    flat_e = eid.reshape(-1)
    onehot = (flat_e[:, None] == jnp.arange(n_experts)[None, :]).astype(I32)
    pos = jnp.take_along_axis(jnp.cumsum(onehot, axis=0), flat_e[:, None], axis=1)[:, 0] - 1
    sizes = jnp.sum(onehot, axis=0)
    padded = (sizes + rows_per - 1) // rows_per * rows_per
    ends = jnp.cumsum(padded)
    dest = (ends - padded)[flat_e] + pos
    n_blk = -(-(m + n_experts * (rows_per - 1)) // rows_per)
    rows = n_blk * rows_per
    row_tok = jnp.zeros((rows,), I32).at[dest].set(jnp.arange(m, dtype=I32) // TOP_K)
    row_w = jnp.zeros((rows,), F32).at[dest].set(wts.reshape(-1))
    blk_e = jnp.minimum(jnp.searchsorted(ends, jnp.arange(n_blk) * rows_per, side='right'), n_experts - 1)
    n_used = (ends[-1] // rows_per).reshape(1)
    return dest.astype(I32), row_tok, row_w.reshape(rows, 1), blk_e.astype(I32), n_used.astype(I32)


def kernel(x, w_in, pool_mix, pool_scale, w_pool_proj, w_nsa_proj, cmp_pos_k, cmp_pos_v, cmp_k_w1, cmp_k_w2,
           cmp_v_w1, cmp_v_w2, w_out, ln1_g, ln1_b, router_group_w, router_group_b, router_expert_w,
           router_expert_b, w_gate, w_up, w_down, ln2_g, ln2_b):
    b, s, d = x.shape
    n = b * s
    depth = w_in.shape[0]
    alpha = (2.0 * depth) ** 0.25
    pw = pool_mix.shape[1] * pool_mix.shape[2]
    qw = w_nsa_proj.shape[1]
    kvw = N_KV_GROUPS * HEAD_DIM
    n_groups, _, per_group = router_expert_w.shape[1:]
    n_experts = n_groups * per_group
    gate_w = 3 * (qw // HEAD_DIM)
    assert s % ATTN_TILE == 0 and WINDOW % ATTN_TILE == 0 and ATTN_TILE % SEL_BLOCK == 0
    assert gate_w <= LANES and n_groups + n_experts <= LANES and s // SEL_BLOCK <= HEAD_DIM

    c_q = pw
    c_kv = pw + qw
    c_gate = c_kv + 6 * kvw
    c_merge = c_gate + gate_w
    tn = 768
    width_a = -(-(c_gate + LANES) // tn) * tn
    assert c_gate % LANES == 0

    cos_t, sin_t, cos_f, sin_s, cos_c, sin_c = _rope_tables(s)
    ov_t = _overlap_t(s)
    half = CMP_STRIDE * HEAD_DIM

    h = x.reshape(n, d)
    for l in range(depth):
        wl = w_in[l]
        w_a = jnp.pad(wl[:, :c_merge], ((0, 0), (0, width_a - c_merge))).astype(BF16)
        w_gp = wl[:, c_merge:c_merge + d].astype(BF16)
        w_ga = wl[:, c_merge + d:].astype(BF16)

        proj = _project(h, w_a, _tile(n, 1024), tn)

        mixed = _pool_mixer(proj, pool_mix[l].astype(BF16), pool_scale[l].reshape(1, pw), b, s, pw, _tile(s, 512))

        def kv(j):
            return _to_heads(proj[:, c_kv + j * kvw:c_kv + (j + 1) * kvw], b, s)

        n_rows = s // CMP_STRIDE
        kcmp, vcmp_t = _compress(
            kv(0).reshape(b, N_KV_GROUPS, n_rows, half), kv(1).reshape(b, N_KV_GROUPS, n_rows, half),
            cmp_pos_k[l].reshape(1, CMP_BLOCK * HEAD_DIM), cmp_pos_v[l].reshape(1, CMP_BLOCK * HEAD_DIM),
            cmp_k_w1[l].astype(BF16), cmp_k_w2[l].astype(BF16), cmp_v_w1[l].astype(BF16),
            cmp_v_w2[l].T.astype(BF16), cos_c, sin_c)
        ksx, kwx = _key_rope(kv(2), kv(4), cos_f, sin_s)
        nq = s // ATTN_TILE

        def v_tiles(v):
            return v.reshape(b, N_KV_GROUPS, nq, ATTN_TILE, HEAD_DIM).transpose(0, 1, 2, 4, 3).astype(BF16)

        attn = _attention(
            proj, cos_t, sin_t, kcmp, vcmp_t,
            ksx.reshape(b, N_KV_GROUPS, nq, ATTN_TILE, 2 * HEAD_DIM), v_tiles(kv(3)),
            kwx.reshape(b, N_KV_GROUPS, nq, ATTN_TILE, HEAD_DIM), v_tiles(kv(5)),
            ov_t, b=b, s=s, pw=pw, qw=qw, gate_col=c_gate)

        h = _merge_out(h, mixed, attn, w_pool_proj[l].astype(BF16), w_nsa_proj[l].astype(BF16), w_gp, w_ga,
                       w_out[l].astype(BF16), ln1_g[l].reshape(1, d), ln1_b[l].reshape(1, d), alpha,
                       _tile(n, 512), _tile(d, 512))

        w_r = jnp.concatenate([router_group_w[l], router_expert_w[l].transpose(1, 0, 2).reshape(d, n_experts)], axis=1)
        w_r = jnp.pad(w_r, ((0, 0), (0, LANES - w_r.shape[1])))
        w_r_hi = w_r.astype(BF16)
        w_r_lo = (w_r - w_r_hi.astype(F32)).astype(BF16)
        b_r = jnp.pad(jnp.concatenate([router_group_b[l], router_expert_b[l].reshape(-1)]),
                      (0, LANES - n_groups - n_experts)).reshape(1, LANES)
        eid_l, wts_l = _router(h, w_r_hi, w_r_lo, b_r, n_groups, per_group, _tile(n, 512))
        dest, row_tok, row_w, blk_e, n_used = _dispatch_plan(eid_l[:, :TOP_K], wts_l[:, :TOP_K], n_experts)

        ybuf = _experts(blk_e, n_used, row_tok, h, row_w, w_gate[l], w_up[l], w_down[l])
        h = _combine(dest, h, ybuf, ln2_g[l].reshape(1, d), ln2_b[l].reshape(1, d), alpha, _tile(n, 256))
    return h.reshape(b, s, d)
```

```python
import functools
import math

import numpy as np
import jax
import jax.numpy as jnp
from jax import lax
from jax.experimental import pallas as pl
from jax.experimental.pallas import tpu as pltpu

F32 = jnp.float32
BF16 = jnp.bfloat16
I32 = jnp.int32

POOL_WINDOWS = (2, 4, 8, 16)
POOL_GROUPS = 4
POOL_HALO = 16
HEAD_DIM = 64
HALF_DIM = HEAD_DIM // 2
N_KV_GROUPS = 4
CMP_BLOCK = 32
CMP_STRIDE = 16
SEL_BLOCK = 64
SEL_TOP_N = 16
WINDOW = 512
ROPE_THETA = 10000.0
FORCE_SCORE = 1e6
TOP_K = 2
LN_EPS = 1e-5
NEG = -1e30

LANES = 128
SUBLANES = 8
VMEM_BYTES_V7X = 64 * 1024 * 1024
VMEM_LIMIT = VMEM_BYTES_V7X - 8 * 1024 * 1024

ATTN_TILE = 256
EXPERT_ROWS = 256


def _cparams(sem, vmem=VMEM_LIMIT, row_dma=False):
    return pltpu.CompilerParams(dimension_semantics=sem, vmem_limit_bytes=vmem, disable_bounds_checks=row_dma)


def _tile(n, pref):
    t = min(n, pref)
    while n % t:
        t //= 2
    return t


def _proj_kernel(x_ref, w_ref, o_ref, xb_ref):
    @pl.when(pl.program_id(1) == 0)
    def _():
        xb_ref[...] = x_ref[...].astype(BF16)

    o_ref[...] = jnp.dot(xb_ref[...], w_ref[...], preferred_element_type=F32)


def _project(x, w, tm, tn):
    n, d = x.shape
    cols = w.shape[1]
    return pl.pallas_call(
        _proj_kernel,
        out_shape=jax.ShapeDtypeStruct((n, cols), F32),
        grid=(n // tm, cols // tn),
        in_specs=[pl.BlockSpec((tm, d), lambda i, j: (i, 0)),
                  pl.BlockSpec((d, tn), lambda i, j: (0, j))],
        out_specs=pl.BlockSpec((tm, tn), lambda i, j: (i, j)),
        scratch_shapes=[pltpu.VMEM((tm, d), BF16)],
        compiler_params=_cparams(("arbitrary", "arbitrary")),
        name="in_proj",
    )(x, w)


def _pool_kernel(u_ref, halo_ref, mix_ref, scale_ref, o_ref, ext_ref):
    i = pl.program_id(1)
    ts = u_ref.shape[0]
    gd = mix_ref.shape[1]
    ext_ref[POOL_HALO:, :] = u_ref[...]
    ext_ref[:POOL_HALO, :] = jnp.where(i == 0, 0.0, halo_ref[...])
    t = i * ts + lax.broadcasted_iota(I32, (ts, gd), 0)
    for g, w in enumerate(POOL_WINDOWS):
        cols = slice(g * gd, (g + 1) * gd)
        s = ext_ref[:, cols]
        k = 1
        while k < w:
            s = s + pltpu.roll(s, k, axis=0)
            k *= 2
        cnt = jnp.minimum(t + 1, w).astype(F32)
        pooled = s[POOL_HALO:, :] / cnt - u_ref[:, cols]
        mixed = jnp.dot(pooled.astype(BF16), mix_ref[g], preferred_element_type=F32)
        o_ref[:, cols] = (mixed * scale_ref[:, cols]).astype(BF16)


def _pool_mixer(proj, pool_mix_b, pool_scale, b, s, pw, ts):
    ns = s // ts
    hb = ts // POOL_HALO
    return pl.pallas_call(
        _pool_kernel,
        out_shape=jax.ShapeDtypeStruct((b * s, pw), BF16),
        grid=(b, ns),
        in_specs=[pl.BlockSpec((ts, pw), lambda bi, i: (bi * ns + i, 0)),
                  pl.BlockSpec((POOL_HALO, pw), lambda bi, i: (jnp.maximum((bi * ns + i) * hb - 1, 0), 0)),
                  pl.BlockSpec(pool_mix_b.shape, lambda bi, i: (0, 0, 0)),
                  pl.BlockSpec((1, pw), lambda bi, i: (0, 0))],
        out_specs=pl.BlockSpec((ts, pw), lambda bi, i: (bi * ns + i, 0)),
        scratch_shapes=[pltpu.VMEM((POOL_HALO + ts, pw), F32)],
        compiler_params=_cparams(("arbitrary", "arbitrary")),
        name="pool_mixer",
    )(proj, proj, pool_mix_b, pool_scale)


def _gelu_tanh(x):
    return 0.5 * x * (1.0 + jnp.tanh(math.sqrt(2.0 / math.pi) * (x + 0.044715 * (x * x * x))))


def _swap_halves(x):
    return jnp.concatenate([x[:, HALF_DIM:], x[:, :HALF_DIM]], axis=1)


def _cmp_kernel(kc0_ref, kc1_ref, vc0_ref, vc1_ref, pk_ref, pv_ref, kw1_ref, kw2_ref, vw1_ref, vw2t_ref,
                cos_ref, sin_ref, ko_ref, vo_ref, r_s):
    kc_refs = (kc0_ref, kc1_ref)
    vc_refs = (vc0_ref, vc1_ref)
    s = kc0_ref.shape[0]
    nrow = s // CMP_STRIDE
    half = CMP_STRIDE * HEAD_DIM

    def hidden(p_ref, w1_ref, g):
        r = r_s[g]
        r_next = pltpu.roll(r, nrow - 1, axis=0)
        a = (r + p_ref[:, :half]).astype(BF16)
        bb = (r_next + p_ref[:, half:]).astype(BF16)
        h = (jnp.dot(a, w1_ref[:half, :], preferred_element_type=F32)
             + jnp.dot(bb, w1_ref[half:, :], preferred_element_type=F32))
        return _gelu_tanh(h).astype(BF16)

    gpl = LANES // HEAD_DIM

    def regroup(src_refs):
        for t in range(CMP_STRIDE):
            for j, src_ref in enumerate(src_refs):
                rows = src_ref[pl.ds(t, nrow, stride=CMP_STRIDE), :]
                for gg in range(gpl):
                    r_s[j * gpl + gg, :, t * HEAD_DIM:(t + 1) * HEAD_DIM] = rows[:, gg * HEAD_DIM:(gg + 1) * HEAD_DIM]

    regroup(kc_refs)
    for g in range(N_KV_GROUPS):
        k = jnp.dot(hidden(pk_ref, kw1_ref, g), kw2_ref[...], preferred_element_type=F32)
        k = k * cos_ref[...] + _swap_halves(k) * sin_ref[...]
        ko_ref[0, g] = k.astype(BF16)
    regroup(vc_refs)
    for g in range(N_KV_GROUPS):
        hv = hidden(pv_ref, vw1_ref, g)
        vt = lax.dot_general(vw2t_ref[...], hv, (((1,), (1,)), ((), ())), preferred_element_type=F32)
        vo_ref[0, g] = vt.astype(BF16)


def _compress(proj, col_blk, b, s, pk, pv, kw1, kw2, vw1, vw2t, cos_c, sin_c):
    g = N_KV_GROUPS
    kvw = g * HEAD_DIM
    nrow = s // CMP_STRIDE

    def whole(a):
        return pl.BlockSpec(a.shape, lambda bi: (0,) * a.ndim)

    assert kvw == 2 * LANES

    def lane_tile(j):
        return pl.BlockSpec((s, LANES), lambda bi: (bi, col_blk * (kvw // LANES) + j))

    return pl.pallas_call(
        _cmp_kernel,
        out_shape=(jax.ShapeDtypeStruct((b, g, nrow, HEAD_DIM), BF16),
                   jax.ShapeDtypeStruct((b, g, HEAD_DIM, nrow), BF16)),
        grid=(b,),
        in_specs=[lane_tile(0), lane_tile(1), lane_tile(2), lane_tile(3),
                  whole(pk), whole(pv), whole(kw1), whole(kw2), whole(vw1), whole(vw2t),
                  whole(cos_c), whole(sin_c)],
        out_specs=(pl.BlockSpec((1, g, nrow, HEAD_DIM), lambda bi: (bi, 0, 0, 0)),
                   pl.BlockSpec((1, g, HEAD_DIM, nrow), lambda bi: (bi, 0, 0, 0))),
        scratch_shapes=[pltpu.VMEM((g, nrow, CMP_STRIDE * HEAD_DIM), F32)],
        compiler_params=_cparams(("arbitrary",)),
        name="compress_kv",
    )(proj, proj, proj, proj, pk, pv, kw1, kw2, vw1, vw2t, cos_c, sin_c)


def _kvprep_kernel(ks_ref, vs_ref, kw_ref, vw_ref, cos_ref, sin_ref, kso_ref, vso_ref, kwo_ref, vwo_ref):
    i = pl.program_id(1)
    tq = ks_ref.shape[0]
    cos = cos_ref[...]
    sin = sin_ref[...]
    blk = (i * tq + lax.broadcasted_iota(I32, (tq, HEAD_DIM), 0)) // SEL_BLOCK
    onehot = jnp.where(blk == lax.broadcasted_iota(I32, (tq, HEAD_DIM), 1), 1.0, 0.0).astype(BF16)
    ks = ks_ref[...]
    kw = kw_ref[...]
    vs_t = vs_ref[...].T
    vw_t = vw_ref[...].T
    for g in range(N_KV_GROUPS):
        cols = slice(g * HEAD_DIM, (g + 1) * HEAD_DIM)
        k = ks[:, cols]
        kso_ref[0, g, 0, :, :HEAD_DIM] = (k * cos + _swap_halves(k) * sin).astype(BF16)
        kso_ref[0, g, 0, :, HEAD_DIM:] = onehot
        k = kw[:, cols]
        kwo_ref[0, g, 0] = (k * cos + _swap_halves(k) * sin).astype(BF16)
        vso_ref[0, g, 0] = vs_t[cols].astype(BF16)
        vwo_ref[0, g, 0] = vw_t[cols].astype(BF16)


def _kv_prep(proj, col_blk, b, s, cos_f, sin_s):
    g = N_KV_GROUPS
    kvw = g * HEAD_DIM
    tq = ATTN_TILE
    nq = s // tq

    def col(j):
        return pl.BlockSpec((tq, kvw), lambda bi, i: (bi * nq + i, col_blk + j))

    tab = pl.BlockSpec((tq, HEAD_DIM), lambda bi, i: (i, 0))

    def out(r, c):
        return (jax.ShapeDtypeStruct((b, g, nq, r, c), BF16),
                pl.BlockSpec((1, g, 1, r, c), lambda bi, i: (bi, 0, i, 0, 0)))

    outs = [out(tq, 2 * HEAD_DIM), out(HEAD_DIM, tq), out(tq, HEAD_DIM), out(HEAD_DIM, tq)]
    return pl.pallas_call(
        _kvprep_kernel,
        out_shape=tuple(o[0] for o in outs),
        grid=(b, nq),
        in_specs=[col(2), col(3), col(4), col(5), tab, tab],
        out_specs=tuple(o[1] for o in outs),
        compiler_params=_cparams(("arbitrary", "arbitrary")),
        name="kv_prep",
    )(proj, proj, proj, proj, cos_f, sin_s)


def _attn_kernel(q_ref, cos_ref, sin_ref, gate_ref, kc_ref, vct_ref, ks_ref, vst_ref, kw_ref, vwt_ref, ovt_ref,
                 o_ref, qa_s, g_s, imp_s, m_s, l_s, c_s, acc_s, out_s, sc_s, pc_s, psum_s, s_s, p_s, *, hg, n_heads):
    g = pl.program_id(1)
    i = pl.program_id(2)
    tq = q_ref.shape[0]
    tk = tq
    ncp = kc_ref.shape[2]
    nsel = ovt_ref.shape[0]
    scale = HEAD_DIM ** -0.5

    qt = q_ref[...].T
    cos = cos_ref[...]
    sin = sin_ref[...]
    for h in range(hg):
        x1 = qt[h * HEAD_DIM:h * HEAD_DIM + HALF_DIM]
        x2 = qt[h * HEAD_DIM + HALF_DIM:(h + 1) * HEAD_DIM]
        qa_s[h, :HALF_DIM] = ((x1 * cos - x2 * sin) * scale).astype(BF16)
        qa_s[h, HALF_DIM:HEAD_DIM] = ((x2 * cos + x1 * sin) * scale).astype(BF16)
    g_s[...] = gate_ref[...].T

    def gate(branch, h):
        row = g_s[pl.ds(branch * n_heads + g * hg + h, 1), :]
        return 1.0 / (1.0 + jnp.exp(-row))

    chunks = [slice(c * LANES, (c + 1) * LANES) for c in range(tq // LANES)]

    def qk(dst, kx, qrows):
        for h in range(hg):
            dst[h] = jnp.dot(kx, qa_s[h, :qrows], preferred_element_type=F32)

    qk(sc_s, kc_ref[0, 0], HEAD_DIM)
    c_end = lax.broadcasted_iota(I32, (ncp, LANES), 0) * CMP_STRIDE + (CMP_BLOCK - 1)
    for c, cols in enumerate(chunks):
        cmask = c_end <= i * tq + c * LANES + lax.broadcasted_iota(I32, (ncp, LANES), 1)
        for h in range(hg):
            sm = jnp.where(cmask, sc_s[h, :, cols], NEG)
            m = jnp.max(sm, axis=0, keepdims=True)
            p = jnp.where(cmask, jnp.exp(sm - m), 0.0)
            den = jnp.maximum(jnp.sum(p, axis=0, keepdims=True), 1e-30)
            pn = p / den
            psum_s[:, cols] = pn if h == 0 else psum_s[:, cols] + pn
            pc_s[h, :, cols] = pn.astype(BF16)
    for h in range(hg):
        oc = jnp.dot(vct_ref[0, 0], pc_s[h], preferred_element_type=F32)
        out_s[h * HEAD_DIM:(h + 1) * HEAD_DIM, :] = oc * gate(0, h)

    psum = psum_s[...]
    p_hi = psum.astype(BF16)
    p_lo = (psum - p_hi.astype(F32)).astype(BF16)
    imp = (jnp.dot(ovt_ref[...], p_hi, preferred_element_type=F32)
           + jnp.dot(ovt_ref[...], p_lo, preferred_element_type=F32))
    jb = lax.broadcasted_iota(I32, (nsel, tq), 0)
    cur = (i * tq + lax.broadcasted_iota(I32, (nsel, tq), 1)) // SEL_BLOCK
    forced = (jb == 0) | (jb == cur) | (jb == cur - 1)
    imp = jnp.where(forced, FORCE_SCORE, jnp.where(jb > cur, -FORCE_SCORE, imp))
    imp_s[...] = imp
    nrb = nsel // SUBLANES
    blocks = [imp[r * SUBLANES:(r + 1) * SUBLANES] for r in range(nrb)]
    counts = [jnp.zeros((SUBLANES, tq), F32) for _ in range(nrb)]
    sub = lax.broadcasted_iota(I32, (SUBLANES, tq), 0)
    for jp in range(nsel):
        row = jnp.broadcast_to(imp_s[jp:jp + 1, :], (SUBLANES, tq))
        for r in range(nrb):
            if r * SUBLANES > jp:
                inc = jnp.where(row >= blocks[r], 1.0, 0.0)
            elif r * SUBLANES + SUBLANES - 1 <= jp:
                inc = jnp.where(row > blocks[r], 1.0, 0.0)
            else:
                tie = jnp.where(sub > jp - r * SUBLANES, 1.0, 0.0)
                inc = jnp.where(row > blocks[r], 1.0, jnp.where(row >= blocks[r], tie, 0.0))
            counts[r] = counts[r] + inc
    top_n = min(SEL_TOP_N, nsel)
    bias = jnp.concatenate([jnp.where(cnt < top_n, 0.0, NEG) for cnt in counts], axis=0).astype(BF16)
    for h in range(hg):
        qa_s[h, HEAD_DIM:HEAD_DIM + nsel] = bias
        if nsel < HEAD_DIM:
            qa_s[h, HEAD_DIM + nsel:] = jnp.zeros((HEAD_DIM - nsel, tq), BF16)

    def reset():
        m_s[...] = jnp.full(m_s.shape, NEG, F32)
        l_s[...] = jnp.zeros(l_s.shape, F32)
        acc_s[...] = jnp.zeros(acc_s.shape, F32)

    krow = lax.broadcasted_iota(I32, (tk, LANES), 0)
    lane = lax.broadcasted_iota(I32, (tk, LANES), 1)

    def softmax_pv(src, vt, kind, skip=None):
        off = None if skip is None else jnp.where(skip, tk, 0)
        for c, cols in enumerate(chunks):
            qcol = lane + c * LANES
            if kind == 'causal':
                mask = krow <= qcol
            elif kind == 'older':
                mask = krow > (qcol if off is None else qcol + off)
            else:
                mask = None if off is None else krow >= off
            for h in range(hg):
                s = src[h, :, cols]
                if mask is not None:
                    s = jnp.where(mask, s, NEG)
                m_old = m_s[h, :, cols]
                m_new = jnp.maximum(m_old, jnp.max(s, axis=0, keepdims=True))
                p = jnp.exp(s - m_new)
                corr = jnp.exp(m_old - m_new)
                l_s[h, :, cols] = l_s[h, :, cols] * corr + jnp.sum(p, axis=0, keepdims=True)
                m_s[h, :, cols] = m_new
                c_s[h, :, cols] = corr
                p_s[h, :, cols] = p.astype(BF16)
        for h in range(hg):
            acc_s[h] = acc_s[h] * c_s[h] + jnp.dot(vt, p_s[h], preferred_element_type=F32)

    def finish(branch):
        for h in range(hg):
            rows = slice(h * HEAD_DIM, (h + 1) * HEAD_DIM)
            out_s[rows, :] = out_s[rows, :] + acc_s[h] * (gate(branch, h) / l_s[h])

    s_a, s_b, s_c = s_s.at[0], s_s.at[1], s_s.at[2]
    sel_q = 2 * HEAD_DIM

    reset()
    qk(s_a, ks_ref[0, 0, 0], sel_q)

    def sel_pair(pi, carry):
        kt = 2 * pi
        qk(s_b, ks_ref[0, 0, kt + 1], sel_q)
        softmax_pv(s_a, vst_ref[0, 0, kt], None)
        qk(s_a, ks_ref[0, 0, kt + 2], sel_q)
        softmax_pv(s_b, vst_ref[0, 0, kt + 1], None)
        return carry

    lax.fori_loop(0, i // 2, sel_pair, 0)

    @pl.when(i % 2 == 0)
    def _():
        softmax_pv(s_a, vst_ref[0, 0, i], 'causal')

    @pl.when(i % 2 == 1)
    def _():
        qk(s_b, ks_ref[0, 0, i], sel_q)
        softmax_pv(s_a, vst_ref[0, 0, i - 1], None)
        softmax_pv(s_b, vst_ref[0, 0, i], 'causal')

    finish(1)

    reset()
    nwt = WINDOW // tk
    assert nwt == 2
    t_old = jnp.maximum(i - 2, 0)
    t_mid = jnp.maximum(i - 1, 0)
    qk(s_a, kw_ref[0, 0, t_old], HEAD_DIM)
    qk(s_b, kw_ref[0, 0, t_mid], HEAD_DIM)
    qk(s_c, kw_ref[0, 0, i], HEAD_DIM)
    softmax_pv(s_a, vwt_ref[0, 0, t_old], 'older', skip=i < 2)
    softmax_pv(s_b, vwt_ref[0, 0, t_mid], None, skip=i < 1)
    softmax_pv(s_c, vwt_ref[0, 0, i], 'causal')
    finish(2)

    o_ref[...] = out_s[...].T.astype(o_ref.dtype)


def _attention(proj, cos_t, sin_t, kcmp, vcmp_t, ksx, vs_t, kwx, vw_t, ov_t, *, b, s, pw, qw, gate_col):
    tq = ATTN_TILE
    nq = s // tq
    g = N_KV_GROUPS
    gw = qw // g
    hg = gw // HEAD_DIM
    n_heads = qw // HEAD_DIM
    ncp = kcmp.shape[2]
    nsel = ov_t.shape[0]
    q_blk0 = pw // gw
    kern = functools.partial(_attn_kernel, hg=hg, n_heads=n_heads)
    kv5 = lambda bi, gi, i: (bi, gi, 0, 0, 0)
    kv4 = lambda bi, gi, i: (bi, gi, 0, 0)
    return pl.pallas_call(
        kern,
        out_shape=jax.ShapeDtypeStruct((b * s, qw), BF16),
        grid=(b, g, nq),
        in_specs=[pl.BlockSpec((tq, gw), lambda bi, gi, i: (bi * nq + i, q_blk0 + gi)),
                  pl.BlockSpec((HALF_DIM, tq), lambda bi, gi, i: (0, i)),
                  pl.BlockSpec((HALF_DIM, tq), lambda bi, gi, i: (0, i)),
                  pl.BlockSpec((tq, LANES), lambda bi, gi, i: (bi * nq + i, gate_col // LANES)),
                  pl.BlockSpec((1, 1, ncp, HEAD_DIM), kv4),
                  pl.BlockSpec((1, 1, HEAD_DIM, ncp), kv4),
                  pl.BlockSpec((1, 1, nq, tq, 2 * HEAD_DIM), kv5),
                  pl.BlockSpec((1, 1, nq, HEAD_DIM, tq), kv5),
                  pl.BlockSpec((1, 1, nq, tq, HEAD_DIM), kv5),
                  pl.BlockSpec((1, 1, nq, HEAD_DIM, tq), kv5),
                  pl.BlockSpec(ov_t.shape, lambda bi, gi, i: (0, 0))],
        out_specs=pl.BlockSpec((tq, gw), lambda bi, gi, i: (bi * nq + i, gi)),
        scratch_shapes=[pltpu.VMEM((hg, 2 * HEAD_DIM, tq), BF16),
                        pltpu.VMEM((LANES, tq), F32),
                        pltpu.VMEM((nsel, tq), F32),
                        pltpu.VMEM((hg, 1, tq), F32),
                        pltpu.VMEM((hg, 1, tq), F32),
                        pltpu.VMEM((hg, 1, tq), F32),
                        pltpu.VMEM((hg, HEAD_DIM, tq), F32),
                        pltpu.VMEM((gw, tq), F32),
                        pltpu.VMEM((hg, ncp, tq), F32),
                        pltpu.VMEM((hg, ncp, tq), BF16),
                        pltpu.VMEM((ncp, tq), F32),
                        pltpu.VMEM((3, hg, tq, tq), F32),
                        pltpu.VMEM((hg, tq, tq), BF16)],
        compiler_params=_cparams(("arbitrary", "arbitrary", "arbitrary")),
        name="nsa_attention",
    )(proj, cos_t, sin_t, proj, kcmp, vcmp_t, ksx, vs_t, kwx, vw_t, ov_t)


def _layer_norm(r, g, b):
    mu = jnp.mean(r, axis=-1, keepdims=True)
    d = r - mu
    var = jnp.mean(d * d, axis=-1, keepdims=True)
    return d * lax.rsqrt(var + LN_EPS) * g + b


def _sigmoid(x):
    return 1.0 / (1.0 + jnp.exp(-x))


def _merge_kernel(x_ref, pm_ref, at_ref, wpp_ref, wnp_ref, wgp_ref, wga_ref, wo_ref, g_ref, b_ref, o_ref,
                  xb_ref, acc_ref, *, alpha):
    c = pl.program_id(1)

    @pl.when(c == 0)
    def _():
        xb_ref[...] = x_ref[...].astype(BF16)
        acc_ref[...] = jnp.zeros(acc_ref.shape, F32)

    xb = xb_ref[...]
    y_pool = jnp.dot(pm_ref[...], wpp_ref[...], preferred_element_type=F32)
    y_attn = jnp.dot(at_ref[...], wnp_ref[...], preferred_element_type=F32)
    g_pool = _sigmoid(jnp.dot(xb, wgp_ref[...], preferred_element_type=F32))
    g_attn = _sigmoid(jnp.dot(xb, wga_ref[...], preferred_element_type=F32))
    z = g_pool * y_pool + g_attn * y_attn
    acc_ref[...] += jnp.dot(z.astype(BF16), wo_ref[...], preferred_element_type=F32)

    @pl.when(c == pl.num_programs(1) - 1)
    def _():
        o_ref[...] = _layer_norm(alpha * x_ref[...] + acc_ref[...], g_ref[...], b_ref[...])


def _merge_out(x, pm, at, wpp, wnp, wgp, wga, wo, ln_g, ln_b, alpha, tm, ck):
    n, d = x.shape
    pw = pm.shape[1]
    qw = at.shape[1]
    row = lambda i, c: (i, 0)
    colc = lambda i, c: (0, c)
    return pl.pallas_call(
        functools.partial(_merge_kernel, alpha=alpha),
        out_shape=jax.ShapeDtypeStruct((n, d), F32),
        grid=(n // tm, d // ck),
        in_specs=[pl.BlockSpec((tm, d), row), pl.BlockSpec((tm, pw), row), pl.BlockSpec((tm, qw), row),
                  pl.BlockSpec((pw, ck), colc), pl.BlockSpec((qw, ck), colc),
                  pl.BlockSpec((d, ck), colc), pl.BlockSpec((d, ck), colc),
                  pl.BlockSpec((ck, d), lambda i, c: (c, 0)),
                  pl.BlockSpec((1, d), lambda i, c: (0, 0)), pl.BlockSpec((1, d), lambda i, c: (0, 0))],
        out_specs=pl.BlockSpec((tm, d), row),
        scratch_shapes=[pltpu.VMEM((tm, d), BF16), pltpu.VMEM((tm, d), F32)],
        compiler_params=_cparams(("arbitrary", "arbitrary")),
        name="merge_out_ln",
    )(x, pm, at, wpp, wnp, wgp, wga, wo, ln_g, ln_b)


def _router_kernel(h_ref, whi_ref, wlo_ref, b_ref, e_ref, w_ref, *, n_groups, per_group):
    h = h_ref[...]
    h_hi = h.astype(BF16)
    h_lo = (h - h_hi.astype(F32)).astype(BF16)
    logits = (jnp.dot(h_hi, whi_ref[...], preferred_element_type=F32)
              + jnp.dot(h_hi, wlo_ref[...], preferred_element_type=F32)
              + jnp.dot(h_lo, whi_ref[...], preferred_element_type=F32)) + b_ref[...]
    lane = lax.broadcasted_iota(I32, logits.shape, 1)
    far = LANES

    def first_argmax(v, vmax):
        return jnp.min(jnp.where(v == vmax, lane, far), axis=-1, keepdims=True)

    gl = jnp.where(lane < n_groups, logits, NEG)
    gmax = jnp.max(gl, axis=-1, keepdims=True)
    grp = first_argmax(gl, gmax)
    gsum = jnp.sum(jnp.where(lane < n_groups, jnp.exp(gl - gmax), 0.0), axis=-1, keepdims=True)
    g_gate = 1.0 / gsum
    lo = n_groups + grp * per_group
    el = jnp.where((lane >= lo) & (lane < lo + per_group), logits, NEG)
    v1 = jnp.max(el, axis=-1, keepdims=True)
    i1 = first_argmax(el, v1)
    el2 = jnp.where(lane == i1, NEG, el)
    v2 = jnp.max(el2, axis=-1, keepdims=True)
    i2 = first_argmax(el2, v2)
    e21 = jnp.exp(v2 - v1)
    w1 = g_gate / (1.0 + e21)
    w2 = g_gate * e21 / (1.0 + e21)
    e_ref[...] = jnp.where(lane == 0, i1 - n_groups, jnp.where(lane == 1, i2 - n_groups, 0))
    w_ref[...] = jnp.where(lane == 0, w1, jnp.where(lane == 1, w2, 0.0))


def _router(h, w_hi, w_lo, bias, n_groups, per_group, tm):
    n, d = h.shape
    return pl.pallas_call(
        functools.partial(_router_kernel, n_groups=n_groups, per_group=per_group),
        out_shape=(jax.ShapeDtypeStruct((n, LANES), I32), jax.ShapeDtypeStruct((n, LANES), F32)),
        grid=(n // tm,),
        in_specs=[pl.BlockSpec((tm, d), lambda i: (i, 0)),
                  pl.BlockSpec((d, LANES), lambda i: (0, 0)), pl.BlockSpec((d, LANES), lambda i: (0, 0)),
                  pl.BlockSpec((1, LANES), lambda i: (0, 0))],
        out_specs=(pl.BlockSpec((tm, LANES), lambda i: (i, 0)), pl.BlockSpec((tm, LANES), lambda i: (i, 0))),
        compiler_params=_cparams(("arbitrary",)),
        name="moe_router",
    )(h, w_hi, w_lo, bias)


def _expert_kernel(blk0_ref, row_tok_ref, h_hbm, wg_ref, wu_ref, wd_ref, y_hbm,
                   xbuf, ybuf, wgb, wub, wdb, gsem, ysem):
    e = pl.program_id(0)
    n_exp = pl.num_programs(0)
    rows = xbuf.shape[1]
    b0 = blk0_ref[e]
    nb = blk0_ref[e + 1] - b0
    n_used = blk0_ref[n_exp]

    def row_copy(blk, slot, r8, u):
        tok = row_tok_ref[blk * rows + r8 + u]
        return pltpu.make_async_copy(h_hbm.at[pl.ds(tok, 1)], xbuf.at[slot, pl.ds(r8 + u, 1)], gsem.at[slot])

    def start_gather(blk, slot):
        def body(t, c):
            r8 = pl.multiple_of(t * SUBLANES, SUBLANES)
            for u in range(SUBLANES):
                row_copy(blk, slot, r8, u).start()
            return c
        lax.fori_loop(0, rows // SUBLANES, body, 0)

    def wait_gather(blk, slot):
        def body(t, c):
            r8 = pl.multiple_of(t * SUBLANES, SUBLANES)
            for u in range(SUBLANES):
                row_copy(blk, slot, r8, u).wait()
            return c
        lax.fori_loop(0, rows // SUBLANES, body, 0)

    def y_copy(blk, slot):
        return pltpu.make_async_copy(ybuf.at[slot], y_hbm.at[pl.ds(blk * rows, rows)], ysem.at[slot])

    @pl.when((e == 0) & (n_used > 0))
    def _():
        start_gather(0, 0)

    @pl.when(nb > 0)
    def _():
        wgb[...] = wg_ref[0].astype(BF16)
        wub[...] = wu_ref[0].astype(BF16)
        wdb[...] = wd_ref[0].astype(BF16)

    def block(j, carry):
        b = b0 + j
        slot = b % 2

        @pl.when(b + 1 < n_used)
        def _():
            start_gather(b + 1, 1 - slot)

        wait_gather(b, slot)
        x = xbuf[slot].astype(BF16)
        hgate = jnp.dot(x, wgb[...], preferred_element_type=F32)
        hup = jnp.dot(x, wub[...], preferred_element_type=F32)
        a = (hgate * _sigmoid(hgate) * hup).astype(BF16)
        y = jnp.dot(a, wdb[...], preferred_element_type=F32)

        @pl.when(b >= 2)
        def _():
            y_copy(b - 2, slot).wait()

        ybuf[slot] = y
        y_copy(b, slot).start()
        return carry

    lax.fori_loop(0, nb, block, 0)

    @pl.when(e == n_exp - 1)
    def _():
        for back in (2, 1):
            @pl.when(n_used >= back)
            def _(back=back):
                y_copy(n_used - back, (n_used - back) % 2).wait()

        n_blk = y_hbm.shape[0] // rows
        ybuf[0] = jnp.zeros(ybuf.shape[1:], ybuf.dtype)

        def zero_start(blk, c):
            y_copy(blk, 0).start()
            return c

        def zero_wait(blk, c):
            y_copy(blk, 0).wait()
            return c

        lax.fori_loop(n_used, n_blk, zero_start, 0)
        lax.fori_loop(n_used, n_blk, zero_wait, 0)


def _experts(blk0, row_tok, h, w_gate, w_up, w_down, n_blk):
    n_exp = w_gate.shape[0]
    rows = EXPERT_ROWS
    d = h.shape[1]
    hid = w_gate.shape[2]
    grid_spec = pltpu.PrefetchScalarGridSpec(
        num_scalar_prefetch=2,
        grid=(n_exp,),
        in_specs=[pl.BlockSpec(memory_space=pl.ANY),
                  pl.BlockSpec((1, d, hid), lambda e, b0, rt: (e, 0, 0)),
                  pl.BlockSpec((1, d, hid), lambda e, b0, rt: (e, 0, 0)),
                  pl.BlockSpec((1, hid, d), lambda e, b0, rt: (e, 0, 0))],
        out_specs=pl.BlockSpec(memory_space=pl.ANY),
        scratch_shapes=[pltpu.VMEM((2, rows, d), F32), pltpu.VMEM((2, rows, d), F32),
                        pltpu.VMEM((d, hid), BF16), pltpu.VMEM((d, hid), BF16), pltpu.VMEM((hid, d), BF16),
                        pltpu.SemaphoreType.DMA((2,)), pltpu.SemaphoreType.DMA((2,))],
    )
    return pl.pallas_call(
        _expert_kernel,
        out_shape=jax.ShapeDtypeStruct((n_blk * rows, d), F32),
        grid_spec=grid_spec,
        compiler_params=_cparams(("arbitrary",), row_dma=True),
        name="moe_experts",
    )(blk0, row_tok, h, w_gate, w_up, w_down)


def _combine_kernel(dest_ref, h_ref, w_ref, y_hbm, g_ref, b_ref, o_ref, ybuf, sem, *, alpha):
    i = pl.program_id(0)
    n_steps = pl.num_programs(0)
    tm = h_ref.shape[0]
    slot = i % 2

    def row_copy(step_, slot_, r, k):
        src = dest_ref[(step_ * tm + r) * TOP_K + k]
        return pltpu.make_async_copy(y_hbm.at[pl.ds(src, 1)], ybuf.at[slot_, k, pl.ds(r, 1)], sem.at[slot_])

    def start_gather(step_, slot_):
        def body(r, c):
            for k in range(TOP_K):
                row_copy(step_, slot_, r, k).start()
            return c
        lax.fori_loop(0, tm, body, 0, unroll=8)

    def wait_gather(step_, slot_):
        def body(r, c):
            for k in range(TOP_K):
                row_copy(step_, slot_, r, k).wait()
            return c
        lax.fori_loop(0, tm, body, 0, unroll=8)

    @pl.when(i == 0)
    def _():
        start_gather(0, 0)

    @pl.when(i + 1 < n_steps)
    def _():
        start_gather(i + 1, 1 - slot)

    wait_gather(i, slot)
    w = w_ref[...]
    y = ybuf[slot, 0] * w[:, 0:1]
    for k in range(1, TOP_K):
        y = y + ybuf[slot, k] * w[:, k:k + 1]
    o_ref[...] = _layer_norm(alpha * h_ref[...] + y, g_ref[...], b_ref[...])


def _combine(dest, h, wts, ybuf, ln_g, ln_b, alpha, tm):
    n, d = h.shape
    grid_spec = pltpu.PrefetchScalarGridSpec(
        num_scalar_prefetch=1,
        grid=(n // tm,),
        in_specs=[pl.BlockSpec((tm, d), lambda i, ds: (i, 0)),
                  pl.BlockSpec((tm, LANES), lambda i, ds: (i, 0)),
                  pl.BlockSpec(memory_space=pl.ANY),
                  pl.BlockSpec((1, d), lambda i, ds: (0, 0)), pl.BlockSpec((1, d), lambda i, ds: (0, 0))],
        out_specs=pl.BlockSpec((tm, d), lambda i, ds: (i, 0)),
        scratch_shapes=[pltpu.VMEM((2, TOP_K, tm, d), F32), pltpu.SemaphoreType.DMA((2,))],
    )
    return pl.pallas_call(
        functools.partial(_combine_kernel, alpha=alpha),
        out_shape=jax.ShapeDtypeStruct((n, d), F32),
        grid_spec=grid_spec,
        compiler_params=_cparams(("arbitrary",), row_dma=True),
        name="moe_combine_ln",
    )(dest, h, wts, ybuf, ln_g, ln_b)


def _rope_tables(s):
    inv_freq = ROPE_THETA ** (-2.0 * jnp.arange(HALF_DIM, dtype=F32) / HEAD_DIM)

    def tables(pos):
        ang = pos.astype(F32)[:, None] * inv_freq[None, :]
        return jnp.cos(ang), jnp.sin(ang)

    cos, sin = tables(jnp.arange(s))
    n_rows = s // CMP_STRIDE
    c_end = jnp.arange(n_rows) * CMP_STRIDE + (CMP_BLOCK - 1)
    cos_c, sin_c = tables(c_end)
    full = lambda c: jnp.concatenate([c, c], axis=1)
    signed = lambda sn: jnp.concatenate([-sn, sn], axis=1)
    return cos.T, sin.T, full(cos), signed(sin), full(cos_c), signed(sin_c)


def _overlap_t(s):
    n_rows = s // CMP_STRIDE
    n_cmp = (s - CMP_BLOCK) // CMP_STRIDE + 1
    n_sel = s // SEL_BLOCK
    c_start = np.arange(n_rows) * CMP_STRIDE
    s_start = np.arange(n_sel) * SEL_BLOCK
    ov = ((c_start[None, :] + CMP_BLOCK - 1 >= s_start[:, None])
          & (c_start[None, :] <= s_start[:, None] + SEL_BLOCK - 1)
          & (np.arange(n_rows)[None, :] < n_cmp))
    return jnp.asarray(ov.astype(np.float32), dtype=BF16)


def _dispatch_plan(eid, n_experts):
    n = eid.shape[0]
    m = n * TOP_K
    rows_per = EXPERT_ROWS
    flat_e = eid.reshape(-1)
    onehot = (flat_e[:, None] == jnp.arange(n_experts)[None, :]).astype(I32)
    pos = jnp.take_along_axis(jnp.cumsum(onehot, axis=0), flat_e[:, None], axis=1)[:, 0] - 1
    sizes = jnp.sum(onehot, axis=0)
    padded = (sizes + rows_per - 1) // rows_per * rows_per
    ends = jnp.cumsum(padded)
    dest = (ends - padded)[flat_e] + pos
    n_blk = -(-(m + n_experts * (rows_per - 1)) // rows_per)
    row_tok = jnp.zeros((n_blk * rows_per,), I32).at[dest].set(jnp.arange(m, dtype=I32) // TOP_K)
    blk0 = jnp.concatenate([jnp.zeros((1,), I32), ends.astype(I32) // rows_per])
    return dest.astype(I32), row_tok, blk0, n_blk


def kernel(x, w_in, pool_mix, pool_scale, w_pool_proj, w_nsa_proj, cmp_pos_k, cmp_pos_v, cmp_k_w1, cmp_k_w2,
           cmp_v_w1, cmp_v_w2, w_out, ln1_g, ln1_b, router_group_w, router_group_b, router_expert_w,
           router_expert_b, w_gate, w_up, w_down, ln2_g, ln2_b):
    b, s, d = x.shape
    n = b * s
    depth = w_in.shape[0]
    alpha = (2.0 * depth) ** 0.25
    pw = pool_mix.shape[1] * pool_mix.shape[2]
    qw = w_nsa_proj.shape[1]
    kvw = N_KV_GROUPS * HEAD_DIM
    n_groups, _, per_group = router_expert_w.shape[1:]
    n_experts = n_groups * per_group
    gate_w = 3 * (qw // HEAD_DIM)
    assert s % ATTN_TILE == 0 and WINDOW % ATTN_TILE == 0 and ATTN_TILE % SEL_BLOCK == 0
    assert gate_w <= LANES and n_groups + n_experts <= LANES and s // SEL_BLOCK <= HEAD_DIM

    c_q = pw
    c_kv = pw + qw
    c_gate = c_kv + 6 * kvw
    c_merge = c_gate + gate_w
    tn = 768
    width_a = -(-(c_gate + LANES) // tn) * tn
    assert c_gate % LANES == 0

    cos_t, sin_t, cos_f, sin_s, cos_c, sin_c = _rope_tables(s)
    ov_t = _overlap_t(s)
    assert c_kv % kvw == 0

    h = x.reshape(n, d)
    for l in range(depth):
        wl = w_in[l]
        w_a = jnp.pad(wl[:, :c_merge], ((0, 0), (0, width_a - c_merge))).astype(BF16)
        w_gp = wl[:, c_merge:c_merge + d].astype(BF16)
        w_ga = wl[:, c_merge + d:].astype(BF16)

        proj = _project(h, w_a, _tile(n, 1024), tn)

        mixed = _pool_mixer(proj, pool_mix[l].astype(BF16), pool_scale[l].reshape(1, pw), b, s, pw, _tile(s, 512))

        kv_blk = c_kv // kvw
        kcmp, vcmp_t = _compress(
            proj, kv_blk, b, s,
            cmp_pos_k[l].reshape(1, CMP_BLOCK * HEAD_DIM), cmp_pos_v[l].reshape(1, CMP_BLOCK * HEAD_DIM),
            cmp_k_w1[l].astype(BF16), cmp_k_w2[l].astype(BF16), cmp_v_w1[l].astype(BF16),
            cmp_v_w2[l].T.astype(BF16), cos_c, sin_c)
        ksx, vs_t, kwx, vw_t = _kv_prep(proj, kv_blk, b, s, cos_f, sin_s)

        attn = _attention(proj, cos_t, sin_t, kcmp, vcmp_t, ksx, vs_t, kwx, vw_t, ov_t,
                          b=b, s=s, pw=pw, qw=qw, gate_col=c_gate)

        h = _merge_out(h, mixed, attn, w_pool_proj[l].astype(BF16), w_nsa_proj[l].astype(BF16), w_gp, w_ga,
                       w_out[l].astype(BF16), ln1_g[l].reshape(1, d), ln1_b[l].reshape(1, d), alpha,
                       _tile(n, 512), _tile(d, 512))

        w_r = jnp.concatenate([router_group_w[l], router_expert_w[l].transpose(1, 0, 2).reshape(d, n_experts)], axis=1)
        w_r = jnp.pad(w_r, ((0, 0), (0, LANES - w_r.shape[1])))
        w_r_hi = w_r.astype(BF16)
        w_r_lo = (w_r - w_r_hi.astype(F32)).astype(BF16)
        b_r = jnp.pad(jnp.concatenate([router_group_b[l], router_expert_b[l].reshape(-1)]),
                      (0, LANES - n_groups - n_experts)).reshape(1, LANES)
        eid_l, wts_l = _router(h, w_r_hi, w_r_lo, b_r, n_groups, per_group, _tile(n, 512))
        dest, row_tok, blk0, n_blk = _dispatch_plan(eid_l[:, :TOP_K], n_experts)

        ybuf = _experts(blk0, row_tok, h, w_gate[l], w_up[l], w_down[l], n_blk)
        h = _combine(dest, h, wts_l, ybuf, ln2_g[l].reshape(1, d), ln2_b[l].reshape(1, d), alpha, _tile(n, 256))
    return h.reshape(b, s, d)
```

```python
import functools
import math

import numpy as np
import jax
import jax.numpy as jnp
from jax import lax
from jax.experimental import pallas as pl
from jax.experimental.pallas import tpu as pltpu

F32 = jnp.float32
BF16 = jnp.bfloat16
I32 = jnp.int32

POOL_WINDOWS = (2, 4, 8, 16)
POOL_GROUPS = 4
POOL_HALO = 16
HEAD_DIM = 64
HALF_DIM = HEAD_DIM // 2
V_ROWS = HEAD_DIM + 16
N_KV_GROUPS = 4
CMP_BLOCK = 32
CMP_STRIDE = 16
SEL_BLOCK = 64
SEL_TOP_N = 16
WINDOW = 512
ROPE_THETA = 10000.0
FORCE_SCORE = 1e6
TOP_K = 2
LN_EPS = 1e-5
NEG = -1e30

LANES = 128
SUBLANES = 8
VMEM_BYTES_V7X = 64 * 1024 * 1024
VMEM_LIMIT = VMEM_BYTES_V7X - 8 * 1024 * 1024

ATTN_TILE = 256
EXPERT_ROWS = 256


def _cparams(sem, vmem=VMEM_LIMIT, row_dma=False):
    return pltpu.CompilerParams(dimension_semantics=sem, vmem_limit_bytes=vmem, disable_bounds_checks=row_dma)


def _tile(n, pref):
    t = min(n, pref)
    while n % t:
        t //= 2
    return t


def _proj_kernel(x_ref, w_ref, o_ref, xb_ref):
    @pl.when(pl.program_id(1) == 0)
    def _():
        xb_ref[...] = x_ref[...].astype(BF16)

    o_ref[...] = jnp.dot(xb_ref[...], w_ref[...], preferred_element_type=F32)


def _project(x, w, tm, tn):
    n, d = x.shape
    cols = w.shape[1]
    return pl.pallas_call(
        _proj_kernel,
        out_shape=jax.ShapeDtypeStruct((n, cols), F32),
        grid=(n // tm, cols // tn),
        in_specs=[pl.BlockSpec((tm, d), lambda i, j: (i, 0)),
                  pl.BlockSpec((d, tn), lambda i, j: (0, j))],
        out_specs=pl.BlockSpec((tm, tn), lambda i, j: (i, j)),
        scratch_shapes=[pltpu.VMEM((tm, d), BF16)],
        compiler_params=_cparams(("arbitrary", "arbitrary")),
        name="in_proj",
    )(x, w)


def _pool_kernel(u_ref, halo_ref, mix_ref, scale_ref, o_ref, ext_ref):
    i = pl.program_id(1)
    ts = u_ref.shape[0]
    gd = mix_ref.shape[1]
    ext_ref[POOL_HALO:, :] = u_ref[...]
    ext_ref[:POOL_HALO, :] = jnp.where(i == 0, 0.0, halo_ref[...])
    t = i * ts + lax.broadcasted_iota(I32, (ts, gd), 0)
    for g, w in enumerate(POOL_WINDOWS):
        cols = slice(g * gd, (g + 1) * gd)
        s = ext_ref[:, cols]
        k = 1
        while k < w:
            s = s + pltpu.roll(s, k, axis=0)
            k *= 2
        cnt = jnp.minimum(t + 1, w).astype(F32)
        pooled = s[POOL_HALO:, :] / cnt - u_ref[:, cols]
        mixed = jnp.dot(pooled.astype(BF16), mix_ref[g], preferred_element_type=F32)
        o_ref[:, cols] = (mixed * scale_ref[:, cols]).astype(BF16)


def _pool_mixer(proj, pool_mix_b, pool_scale, b, s, pw, ts):
    ns = s // ts
    hb = ts // POOL_HALO
    return pl.pallas_call(
        _pool_kernel,
        out_shape=jax.ShapeDtypeStruct((b * s, pw), BF16),
        grid=(b, ns),
        in_specs=[pl.BlockSpec((ts, pw), lambda bi, i: (bi * ns + i, 0)),
                  pl.BlockSpec((POOL_HALO, pw), lambda bi, i: (jnp.maximum((bi * ns + i) * hb - 1, 0), 0)),
                  pl.BlockSpec(pool_mix_b.shape, lambda bi, i: (0, 0, 0)),
                  pl.BlockSpec((1, pw), lambda bi, i: (0, 0))],
        out_specs=pl.BlockSpec((ts, pw), lambda bi, i: (bi * ns + i, 0)),
        scratch_shapes=[pltpu.VMEM((POOL_HALO + ts, pw), F32)],
        compiler_params=_cparams(("arbitrary", "arbitrary")),
        name="pool_mixer",
    )(proj, proj, pool_mix_b, pool_scale)


def _gelu_tanh(x):
    return 0.5 * x * (1.0 + jnp.tanh(math.sqrt(2.0 / math.pi) * (x + 0.044715 * (x * x * x))))


def _swap_halves(x):
    return jnp.concatenate([x[:, HALF_DIM:], x[:, :HALF_DIM]], axis=1)


def _cmp_kernel(kc0_ref, kc1_ref, vc0_ref, vc1_ref, pk_ref, pv_ref, kw1_ref, kw2_ref, vw1_ref, vw2t_ref,
                cos_ref, sin_ref, ko_ref, vo_ref, r_s):
    kc_refs = (kc0_ref, kc1_ref)
    vc_refs = (vc0_ref, vc1_ref)
    s = kc0_ref.shape[0]
    nrow = s // CMP_STRIDE
    half = CMP_STRIDE * HEAD_DIM

    def hidden(p_ref, w1_ref, g):
        r = r_s[g]
        r_next = pltpu.roll(r, nrow - 1, axis=0)
        a = (r + p_ref[:, :half]).astype(BF16)
        bb = (r_next + p_ref[:, half:]).astype(BF16)
        h = (jnp.dot(a, w1_ref[:half, :], preferred_element_type=F32)
             + jnp.dot(bb, w1_ref[half:, :], preferred_element_type=F32))
        return _gelu_tanh(h).astype(BF16)

    gpl = LANES // HEAD_DIM

    def regroup(src_refs):
        for t in range(CMP_STRIDE):
            for j, src_ref in enumerate(src_refs):
                rows = src_ref[pl.ds(t, nrow, stride=CMP_STRIDE), :]
                for gg in range(gpl):
                    r_s[j * gpl + gg, :, t * HEAD_DIM:(t + 1) * HEAD_DIM] = rows[:, gg * HEAD_DIM:(gg + 1) * HEAD_DIM]

    regroup(kc_refs)
    for g in range(N_KV_GROUPS):
        k = jnp.dot(hidden(pk_ref, kw1_ref, g), kw2_ref[...], preferred_element_type=F32)
        k = k * cos_ref[...] + _swap_halves(k) * sin_ref[...]
        ko_ref[0, g] = k.astype(BF16)
    regroup(vc_refs)
    for g in range(N_KV_GROUPS):
        hv = hidden(pv_ref, vw1_ref, g)
        vt = lax.dot_general(vw2t_ref[...], hv, (((1,), (1,)), ((), ())), preferred_element_type=F32)
        vo_ref[0, g] = vt.astype(BF16)


def _compress(proj, col_blk, b, s, pk, pv, kw1, kw2, vw1, vw2t, cos_c, sin_c):
    g = N_KV_GROUPS
    kvw = g * HEAD_DIM
    nrow = s // CMP_STRIDE

    def whole(a):
        return pl.BlockSpec(a.shape, lambda bi: (0,) * a.ndim)

    assert kvw == 2 * LANES

    def lane_tile(j):
        return pl.BlockSpec((s, LANES), lambda bi: (bi, col_blk * (kvw // LANES) + j))

    return pl.pallas_call(
        _cmp_kernel,
        out_shape=(jax.ShapeDtypeStruct((b, g, nrow, HEAD_DIM), BF16),
                   jax.ShapeDtypeStruct((b, g, HEAD_DIM, nrow), BF16)),
        grid=(b,),
        in_specs=[lane_tile(0), lane_tile(1), lane_tile(2), lane_tile(3),
                  whole(pk), whole(pv), whole(kw1), whole(kw2), whole(vw1), whole(vw2t),
                  whole(cos_c), whole(sin_c)],
        out_specs=(pl.BlockSpec((1, g, nrow, HEAD_DIM), lambda bi: (bi, 0, 0, 0)),
                   pl.BlockSpec((1, g, HEAD_DIM, nrow), lambda bi: (bi, 0, 0, 0))),
        scratch_shapes=[pltpu.VMEM((g, nrow, CMP_STRIDE * HEAD_DIM), F32)],
        compiler_params=_cparams(("arbitrary",)),
        name="compress_kv",
    )(proj, proj, proj, proj, pk, pv, kw1, kw2, vw1, vw2t, cos_c, sin_c)


def _kvprep_kernel(ks_ref, vs_ref, kw_ref, vw_ref, cos_ref, sin_ref, kso_ref, vso_ref, kwo_ref, vwo_ref):
    i = pl.program_id(1)
    tq = ks_ref.shape[0]
    cos = cos_ref[...]
    sin = sin_ref[...]
    blk = (i * tq + lax.broadcasted_iota(I32, (tq, HEAD_DIM), 0)) // SEL_BLOCK
    onehot = jnp.where(blk == lax.broadcasted_iota(I32, (tq, HEAD_DIM), 1), 1.0, 0.0).astype(BF16)
    ks = ks_ref[...]
    kw = kw_ref[...]
    vs_t = vs_ref[...].T
    vw_t = vw_ref[...].T
    for g in range(N_KV_GROUPS):
        cols = slice(g * HEAD_DIM, (g + 1) * HEAD_DIM)
        k = ks[:, cols]
        kso_ref[0, g, 0, :, :HEAD_DIM] = (k * cos + _swap_halves(k) * sin).astype(BF16)
        kso_ref[0, g, 0, :, HEAD_DIM:] = onehot
        k = kw[:, cols]
        kwo_ref[0, g, 0] = (k * cos + _swap_halves(k) * sin).astype(BF16)
        ones = jnp.ones((V_ROWS - HEAD_DIM, tq), BF16)
        vso_ref[0, g, 0, :HEAD_DIM] = vs_t[cols].astype(BF16)
        vso_ref[0, g, 0, HEAD_DIM:] = ones
        vwo_ref[0, g, 0, :HEAD_DIM] = vw_t[cols].astype(BF16)
        vwo_ref[0, g, 0, HEAD_DIM:] = ones


def _kv_prep(proj, col_blk, b, s, cos_f, sin_s):
    g = N_KV_GROUPS
    kvw = g * HEAD_DIM
    tq = ATTN_TILE
    nq = s // tq

    def col(j):
        return pl.BlockSpec((tq, kvw), lambda bi, i: (bi * nq + i, col_blk + j))

    tab = pl.BlockSpec((tq, HEAD_DIM), lambda bi, i: (i, 0))

    def out(r, c):
        return (jax.ShapeDtypeStruct((b, g, nq, r, c), BF16),
                pl.BlockSpec((1, g, 1, r, c), lambda bi, i: (bi, 0, i, 0, 0)))

    outs = [out(tq, 2 * HEAD_DIM), out(V_ROWS, tq), out(tq, HEAD_DIM), out(V_ROWS, tq)]
    return pl.pallas_call(
        _kvprep_kernel,
        out_shape=tuple(o[0] for o in outs),
        grid=(b, nq),
        in_specs=[col(2), col(3), col(4), col(5), tab, tab],
        out_specs=tuple(o[1] for o in outs),
        compiler_params=_cparams(("arbitrary", "arbitrary")),
        name="kv_prep",
    )(proj, proj, proj, proj, cos_f, sin_s)


def _attn_kernel(q_ref, cos_ref, sin_ref, gate_ref, kc_ref, vct_ref, ks_ref, vst_ref, kw_ref, vwt_ref, ovt_ref,
                 o_ref, qa_s, g_s, imp_s, m_s, c_s, acc_s, out_s, sc_s, pc_s, psum_s, s_s, p_s, *, hg, n_heads):
    g = pl.program_id(1)
    i = pl.program_id(2)
    tq = q_ref.shape[0]
    tk = tq
    ncp = kc_ref.shape[2]
    nsel = ovt_ref.shape[0]
    scale = HEAD_DIM ** -0.5 * math.log2(math.e)

    qt = q_ref[...].T
    cos = cos_ref[...]
    sin = sin_ref[...]
    for h in range(hg):
        x1 = qt[h * HEAD_DIM:h * HEAD_DIM + HALF_DIM]
        x2 = qt[h * HEAD_DIM + HALF_DIM:(h + 1) * HEAD_DIM]
        qa_s[h, :HALF_DIM] = ((x1 * cos - x2 * sin) * scale).astype(BF16)
        qa_s[h, HALF_DIM:HEAD_DIM] = ((x2 * cos + x1 * sin) * scale).astype(BF16)
    g_s[...] = gate_ref[...].T

    def gate(branch, h):
        row = g_s[pl.ds(branch * n_heads + g * hg + h, 1), :]
        return 1.0 / (1.0 + jnp.exp(-row))

    chunks = [slice(c * LANES, (c + 1) * LANES) for c in range(tq // LANES)]

    def qk(dst, kx, qrows):
        for h in range(hg):
            dst[h] = jnp.dot(kx, qa_s[h, :qrows], preferred_element_type=F32)

    qk(sc_s, kc_ref[0, 0], HEAD_DIM)
    c_end = lax.broadcasted_iota(I32, (ncp, LANES), 0) * CMP_STRIDE + (CMP_BLOCK - 1)
    for c, cols in enumerate(chunks):
        cmask = c_end <= i * tq + c * LANES + lax.broadcasted_iota(I32, (ncp, LANES), 1)
        for h in range(hg):
            sm = jnp.where(cmask, sc_s[h, :, cols], NEG)
            m = jnp.max(sm, axis=0, keepdims=True)
            p = jnp.where(cmask, jnp.exp2(sm - m), 0.0)
            den = jnp.maximum(jnp.sum(p, axis=0, keepdims=True), 1e-30)
            pn = p / den
            psum_s[:, cols] = pn if h == 0 else psum_s[:, cols] + pn
            pc_s[h, :, cols] = pn.astype(BF16)
    for h in range(hg):
        oc = jnp.dot(vct_ref[0, 0], pc_s[h], preferred_element_type=F32)
        out_s[h * HEAD_DIM:(h + 1) * HEAD_DIM, :] = oc * gate(0, h)

    psum = psum_s[...]
    p_hi = psum.astype(BF16)
    p_lo = (psum - p_hi.astype(F32)).astype(BF16)
    imp = (jnp.dot(ovt_ref[...], p_hi, preferred_element_type=F32)
           + jnp.dot(ovt_ref[...], p_lo, preferred_element_type=F32))
    jb = lax.broadcasted_iota(I32, (nsel, tq), 0)
    cur = (i * tq + lax.broadcasted_iota(I32, (nsel, tq), 1)) // SEL_BLOCK
    forced = (jb == 0) | (jb == cur) | (jb == cur - 1)
    imp = jnp.where(forced, FORCE_SCORE, jnp.where(jb > cur, -FORCE_SCORE, imp))
    imp_s[...] = imp
    nrb = nsel // SUBLANES
    blocks = [imp[r * SUBLANES:(r + 1) * SUBLANES] for r in range(nrb)]
    counts = [jnp.zeros((SUBLANES, tq), F32) for _ in range(nrb)]
    sub = lax.broadcasted_iota(I32, (SUBLANES, tq), 0)
    for jp in range(nsel):
        row = jnp.broadcast_to(imp_s[jp:jp + 1, :], (SUBLANES, tq))
        for r in range(nrb):
            if r * SUBLANES > jp:
                inc = jnp.where(row >= blocks[r], 1.0, 0.0)
            elif r * SUBLANES + SUBLANES - 1 <= jp:
                inc = jnp.where(row > blocks[r], 1.0, 0.0)
            else:
                tie = jnp.where(sub > jp - r * SUBLANES, 1.0, 0.0)
                inc = jnp.where(row > blocks[r], 1.0, jnp.where(row >= blocks[r], tie, 0.0))
            counts[r] = counts[r] + inc
    top_n = min(SEL_TOP_N, nsel)
    bias = jnp.concatenate([jnp.where(cnt < top_n, 0.0, NEG) for cnt in counts], axis=0).astype(BF16)
    for h in range(hg):
        qa_s[h, HEAD_DIM:HEAD_DIM + nsel] = bias
        if nsel < HEAD_DIM:
            qa_s[h, HEAD_DIM + nsel:] = jnp.zeros((HEAD_DIM - nsel, tq), BF16)

    def reset():
        m_s[...] = jnp.full(m_s.shape, NEG, F32)
        acc_s[...] = jnp.zeros(acc_s.shape, F32)

    krow = lax.broadcasted_iota(I32, (tk, LANES), 0)
    lane = lax.broadcasted_iota(I32, (tk, LANES), 1)

    def softmax_pv(src, vt, kind, skip=None):
        off = None if skip is None else jnp.where(skip, tk, 0)
        for c, cols in enumerate(chunks):
            qcol = lane + c * LANES

            if kind == 'causal':
                mask = krow <= qcol
            elif kind == 'older':
                mask = krow > (qcol if off is None else qcol + off)
            else:
                mask = None if off is None else krow >= off
            for h in range(hg):
                s = src[h, :, cols]
                if mask is not None:
                    s = jnp.where(mask, s, NEG)
                m_old = m_s[h, :, cols]
                m_new = jnp.maximum(m_old, jnp.max(s, axis=0, keepdims=True))
                m_s[h, :, cols] = m_new
                c_s[h, :, cols] = jnp.exp2(m_old - m_new)
                p_s[h, :, cols] = jnp.exp2(s - m_new).astype(BF16)
        for h in range(hg):
            acc_s[h] = acc_s[h] * c_s[h] + jnp.dot(vt, p_s[h], preferred_element_type=F32)

    def finish(branch):
        for h in range(hg):
            rows = slice(h * HEAD_DIM, (h + 1) * HEAD_DIM)
            den = acc_s[h, HEAD_DIM:HEAD_DIM + 1, :]
            out_s[rows, :] = out_s[rows, :] + acc_s[h, :HEAD_DIM, :] * (gate(branch, h) / den)

    s_a, s_b, s_c = s_s.at[0], s_s.at[1], s_s.at[2]
    sel_q = 2 * HEAD_DIM

    reset()
    qk(s_a, ks_ref[0, 0, 0], sel_q)

    def sel_pair(pi, carry):
        kt = 2 * pi
        qk(s_b, ks_ref[0, 0, kt + 1], sel_q)
        softmax_pv(s_a, vst_ref[0, 0, kt], None)
        qk(s_a, ks_ref[0, 0, kt + 2], sel_q)
        softmax_pv(s_b, vst_ref[0, 0, kt + 1], None)
        return carry

    lax.fori_loop(0, i // 2, sel_pair, 0)

    @pl.when(i % 2 == 0)
    def _():
        softmax_pv(s_a, vst_ref[0, 0, i], 'causal')

    @pl.when(i % 2 == 1)
    def _():
        qk(s_b, ks_ref[0, 0, i], sel_q)
        softmax_pv(s_a, vst_ref[0, 0, i - 1], None)
        softmax_pv(s_b, vst_ref[0, 0, i], 'causal')

    finish(1)

    reset()
    nwt = WINDOW // tk
    assert nwt == 2
    t_old = jnp.maximum(i - 2, 0)
    t_mid = jnp.maximum(i - 1, 0)
    qk(s_a, kw_ref[0, 0, t_old], HEAD_DIM)
    qk(s_b, kw_ref[0, 0, t_mid], HEAD_DIM)
    qk(s_c, kw_ref[0, 0, i], HEAD_DIM)
    softmax_pv(s_a, vwt_ref[0, 0, t_old], 'older', skip=i < 2)
    softmax_pv(s_b, vwt_ref[0, 0, t_mid], None, skip=i < 1)
    softmax_pv(s_c, vwt_ref[0, 0, i], 'causal')
    finish(2)

    o_ref[...] = out_s[...].T.astype(o_ref.dtype)


def _attention(proj, cos_t, sin_t, kcmp, vcmp_t, ksx, vs_t, kwx, vw_t, ov_t, *, b, s, pw, qw, gate_col):
    tq = ATTN_TILE
    nq = s // tq
    g = N_KV_GROUPS
    gw = qw // g
    hg = gw // HEAD_DIM
    n_heads = qw // HEAD_DIM
    ncp = kcmp.shape[2]
    nsel = ov_t.shape[0]
    q_blk0 = pw // gw
    kern = functools.partial(_attn_kernel, hg=hg, n_heads=n_heads)
    kv5 = lambda bi, gi, i: (bi, gi, 0, 0, 0)
    kv4 = lambda bi, gi, i: (bi, gi, 0, 0)
    return pl.pallas_call(
        kern,
        out_shape=jax.ShapeDtypeStruct((b * s, qw), BF16),
        grid=(b, g, nq),
        in_specs=[pl.BlockSpec((tq, gw), lambda bi, gi, i: (bi * nq + i, q_blk0 + gi)),
                  pl.BlockSpec((HALF_DIM, tq), lambda bi, gi, i: (0, i)),
                  pl.BlockSpec((HALF_DIM, tq), lambda bi, gi, i: (0, i)),
                  pl.BlockSpec((tq, LANES), lambda bi, gi, i: (bi * nq + i, gate_col // LANES)),
                  pl.BlockSpec((1, 1, ncp, HEAD_DIM), kv4),
                  pl.BlockSpec((1, 1, HEAD_DIM, ncp), kv4),
                  pl.BlockSpec((1, 1, nq, tq, 2 * HEAD_DIM), kv5),
                  pl.BlockSpec((1, 1, nq, V_ROWS, tq), kv5),
                  pl.BlockSpec((1, 1, nq, tq, HEAD_DIM), kv5),
                  pl.BlockSpec((1, 1, nq, V_ROWS, tq), kv5),
                  pl.BlockSpec(ov_t.shape, lambda bi, gi, i: (0, 0))],
        out_specs=pl.BlockSpec((tq, gw), lambda bi, gi, i: (bi * nq + i, gi)),
        scratch_shapes=[pltpu.VMEM((hg, 2 * HEAD_DIM, tq), BF16),
                        pltpu.VMEM((LANES, tq), F32),
                        pltpu.VMEM((nsel, tq), F32),
                        pltpu.VMEM((hg, 1, tq), F32),
                        pltpu.VMEM((hg, 1, tq), F32),
                        pltpu.VMEM((hg, V_ROWS, tq), F32),
                        pltpu.VMEM((gw, tq), F32),
                        pltpu.VMEM((hg, ncp, tq), F32),
                        pltpu.VMEM((hg, ncp, tq), BF16),
                        pltpu.VMEM((ncp, tq), F32),
                        pltpu.VMEM((3, hg, tq, tq), F32),
                        pltpu.VMEM((hg, tq, tq), BF16)],
        compiler_params=_cparams(("arbitrary", "arbitrary", "arbitrary")),
        name="nsa_attention",
    )(proj, cos_t, sin_t, proj, kcmp, vcmp_t, ksx, vs_t, kwx, vw_t, ov_t)


def _layer_norm(r, g, b):
    mu = jnp.mean(r, axis=-1, keepdims=True)
    d = r - mu
    var = jnp.mean(d * d, axis=-1, keepdims=True)
    return d * lax.rsqrt(var + LN_EPS) * g + b


def _sigmoid(x):
    return 1.0 / (1.0 + jnp.exp(-x))


def _merge_kernel(x_ref, pm_ref, at_ref, wpp_ref, wnp_ref, wgp_ref, wga_ref, wo_ref, g_ref, b_ref, o_ref,
                  xb_ref, acc_ref, *, alpha):
    c = pl.program_id(1)

    @pl.when(c == 0)
    def _():
        xb_ref[...] = x_ref[...].astype(BF16)
        acc_ref[...] = jnp.zeros(acc_ref.shape, F32)

    xb = xb_ref[...]
    y_pool = jnp.dot(pm_ref[...], wpp_ref[...], preferred_element_type=F32)
    y_attn = jnp.dot(at_ref[...], wnp_ref[...], preferred_element_type=F32)
    g_pool = _sigmoid(jnp.dot(xb, wgp_ref[...], preferred_element_type=F32))
    g_attn = _sigmoid(jnp.dot(xb, wga_ref[...], preferred_element_type=F32))
    z = g_pool * y_pool + g_attn * y_attn
    acc_ref[...] += jnp.dot(z.astype(BF16), wo_ref[...], preferred_element_type=F32)

    @pl.when(c == pl.num_programs(1) - 1)
    def _():
        o_ref[...] = _layer_norm(alpha * x_ref[...] + acc_ref[...], g_ref[...], b_ref[...])


def _merge_out(x, pm, at, wpp, wnp, wgp, wga, wo, ln_g, ln_b, alpha, tm, ck):
    n, d = x.shape
    pw = pm.shape[1]
    qw = at.shape[1]
    row = lambda i, c: (i, 0)
    colc = lambda i, c: (0, c)
    return pl.pallas_call(
        functools.partial(_merge_kernel, alpha=alpha),
        out_shape=jax.ShapeDtypeStruct((n, d), F32),
        grid=(n // tm, d // ck),
        in_specs=[pl.BlockSpec((tm, d), row), pl.BlockSpec((tm, pw), row), pl.BlockSpec((tm, qw), row),
                  pl.BlockSpec((pw, ck), colc), pl.BlockSpec((qw, ck), colc),
                  pl.BlockSpec((d, ck), colc), pl.BlockSpec((d, ck), colc),
                  pl.BlockSpec((ck, d), lambda i, c: (c, 0)),
                  pl.BlockSpec((1, d), lambda i, c: (0, 0)), pl.BlockSpec((1, d), lambda i, c: (0, 0))],
        out_specs=pl.BlockSpec((tm, d), row),
        scratch_shapes=[pltpu.VMEM((tm, d), BF16), pltpu.VMEM((tm, d), F32)],
        compiler_params=_cparams(("arbitrary", "arbitrary")),
        name="merge_out_ln",
    )(x, pm, at, wpp, wnp, wgp, wga, wo, ln_g, ln_b)


def _router_kernel(h_ref, whi_ref, wlo_ref, b_ref, e_ref, w_ref, *, n_groups, per_group):
    h = h_ref[...]
    h_hi = h.astype(BF16)
    h_lo = (h - h_hi.astype(F32)).astype(BF16)
    logits = (jnp.dot(h_hi, whi_ref[...], preferred_element_type=F32)
              + jnp.dot(h_hi, wlo_ref[...], preferred_element_type=F32)
              + jnp.dot(h_lo, whi_ref[...], preferred_element_type=F32)) + b_ref[...]
    lane = lax.broadcasted_iota(I32, logits.shape, 1)
    far = LANES

    def first_argmax(v, vmax):
        return jnp.min(jnp.where(v == vmax, lane, far), axis=-1, keepdims=True)

    gl = jnp.where(lane < n_groups, logits, NEG)
    gmax = jnp.max(gl, axis=-1, keepdims=True)
    grp = first_argmax(gl, gmax)
    gsum = jnp.sum(jnp.where(lane < n_groups, jnp.exp(gl - gmax), 0.0), axis=-1, keepdims=True)
    g_gate = 1.0 / gsum
    lo = n_groups + grp * per_group
    el = jnp.where((lane >= lo) & (lane < lo + per_group), logits, NEG)
    v1 = jnp.max(el, axis=-1, keepdims=True)
    i1 = first_argmax(el, v1)
    el2 = jnp.where(lane == i1, NEG, el)
    v2 = jnp.max(el2, axis=-1, keepdims=True)
    i2 = first_argmax(el2, v2)
    e21 = jnp.exp(v2 - v1)
    w1 = g_gate / (1.0 + e21)
    w2 = g_gate * e21 / (1.0 + e21)
    e_ref[...] = jnp.where(lane == 0, i1 - n_groups, jnp.where(lane == 1, i2 - n_groups, 0))
    w_ref[...] = jnp.where(lane == 0, w1, jnp.where(lane == 1, w2, 0.0))


def _router(h, w_hi, w_lo, bias, n_groups, per_group, tm):
    n, d = h.shape
    return pl.pallas_call(
        functools.partial(_router_kernel, n_groups=n_groups, per_group=per_group),
        out_shape=(jax.ShapeDtypeStruct((n, LANES), I32), jax.ShapeDtypeStruct((n, LANES), F32)),
        grid=(n // tm,),
        in_specs=[pl.BlockSpec((tm, d), lambda i: (i, 0)),
                  pl.BlockSpec((d, LANES), lambda i: (0, 0)), pl.BlockSpec((d, LANES), lambda i: (0, 0)),
                  pl.BlockSpec((1, LANES), lambda i: (0, 0))],
        out_specs=(pl.BlockSpec((tm, LANES), lambda i: (i, 0)), pl.BlockSpec((tm, LANES), lambda i: (i, 0))),
        compiler_params=_cparams(("arbitrary",)),
        name="moe_router",
    )(h, w_hi, w_lo, bias)


def _expert_kernel(blk0_ref, row_tok_ref, h_hbm, wg_ref, wu_ref, wd_ref, y_hbm,
                   xbuf, ybuf, wgb, wub, wdb, gsem, ysem):
    e = pl.program_id(0)
    n_exp = pl.num_programs(0)
    rows = xbuf.shape[1]
    b0 = blk0_ref[e]
    nb = blk0_ref[e + 1] - b0
    n_used = blk0_ref[n_exp]

    def row_copy(blk, slot, r8, u):
        tok = row_tok_ref[blk * rows + r8 + u]
        return pltpu.make_async_copy(h_hbm.at[pl.ds(tok, 1)], xbuf.at[slot, pl.ds(r8 + u, 1)], gsem.at[slot])

    def start_gather(blk, slot):
        def body(t, c):
            r8 = pl.multiple_of(t * SUBLANES, SUBLANES)
            for u in range(SUBLANES):
                row_copy(blk, slot, r8, u).start()
            return c
        lax.fori_loop(0, rows // SUBLANES, body, 0)

    def wait_gather(blk, slot):
        def body(t, c):
            r8 = pl.multiple_of(t * SUBLANES, SUBLANES)
            for u in range(SUBLANES):
                row_copy(blk, slot, r8, u).wait()
            return c
        lax.fori_loop(0, rows // SUBLANES, body, 0)

    def y_copy(blk, slot):
        return pltpu.make_async_copy(ybuf.at[slot], y_hbm.at[pl.ds(blk * rows, rows)], ysem.at[slot])

    @pl.when((e == 0) & (n_used > 0))
    def _():
        start_gather(0, 0)

    @pl.when(nb > 0)
    def _():
        wgb[...] = wg_ref[0].astype(BF16)
        wub[...] = wu_ref[0].astype(BF16)
        wdb[...] = wd_ref[0].astype(BF16)

    def block(j, carry):
        b = b0 + j
        slot = b % 2

        @pl.when(b + 1 < n_used)
        def _():
            start_gather(b + 1, 1 - slot)

        wait_gather(b, slot)
        x = xbuf[slot].astype(BF16)
        hgate = jnp.dot(x, wgb[...], preferred_element_type=F32)
        hup = jnp.dot(x, wub[...], preferred_element_type=F32)
        a = (hgate * _sigmoid(hgate) * hup).astype(BF16)
        y = jnp.dot(a, wdb[...], preferred_element_type=F32)

        @pl.when(b >= 2)
        def _():
            y_copy(b - 2, slot).wait()

        ybuf[slot] = y
        y_copy(b, slot).start()
        return carry

    lax.fori_loop(0, nb, block, 0)

    @pl.when(e == n_exp - 1)
    def _():
        for back in (2, 1):
            @pl.when(n_used >= back)
            def _(back=back):
                y_copy(n_used - back, (n_used - back) % 2).wait()

        n_blk = y_hbm.shape[0] // rows
        ybuf[0] = jnp.zeros(ybuf.shape[1:], ybuf.dtype)

        def zero_start(blk, c):
            y_copy(blk, 0).start()
            return c

        def zero_wait(blk, c):
            y_copy(blk, 0).wait()
            return c

        lax.fori_loop(n_used, n_blk, zero_start, 0)
        lax.fori_loop(n_used, n_blk, zero_wait, 0)


def _experts(blk0, row_tok, h, w_gate, w_up, w_down, n_blk):
    n_exp = w_gate.shape[0]
    rows = EXPERT_ROWS
    d = h.shape[1]
    hid = w_gate.shape[2]
    grid_spec = pltpu.PrefetchScalarGridSpec(
        num_scalar_prefetch=2,
        grid=(n_exp,),
        in_specs=[pl.BlockSpec(memory_space=pl.ANY),
                  pl.BlockSpec((1, d, hid), lambda e, b0, rt: (e, 0, 0)),
                  pl.BlockSpec((1, d, hid), lambda e, b0, rt: (e, 0, 0)),
                  pl.BlockSpec((1, hid, d), lambda e, b0, rt: (e, 0, 0))],
        out_specs=pl.BlockSpec(memory_space=pl.ANY),
        scratch_shapes=[pltpu.VMEM((2, rows, d), F32), pltpu.VMEM((2, rows, d), F32),
                        pltpu.VMEM((d, hid), BF16), pltpu.VMEM((d, hid), BF16), pltpu.VMEM((hid, d), BF16),
                        pltpu.SemaphoreType.DMA((2,)), pltpu.SemaphoreType.DMA((2,))],
    )
    return pl.pallas_call(
        _expert_kernel,
        out_shape=jax.ShapeDtypeStruct((n_blk * rows, d), F32),
        grid_spec=grid_spec,
        compiler_params=_cparams(("arbitrary",), row_dma=True),
        name="moe_experts",
    )(blk0, row_tok, h, w_gate, w_up, w_down)


def _combine_kernel(dest_ref, h_ref, w_ref, y_hbm, g_ref, b_ref, o_ref, ybuf, sem, *, alpha):
    i = pl.program_id(0)
    n_steps = pl.num_programs(0)
    tm = h_ref.shape[0]
    slot = i % 2

    def row_copy(step_, slot_, r, k):
        src = dest_ref[(step_ * tm + r) * TOP_K + k]
        return pltpu.make_async_copy(y_hbm.at[pl.ds(src, 1)], ybuf.at[slot_, k, pl.ds(r, 1)], sem.at[slot_])

    def start_gather(step_, slot_):
        def body(r, c):
            for k in range(TOP_K):
                row_copy(step_, slot_, r, k).start()
            return c
        lax.fori_loop(0, tm, body, 0, unroll=8)

    def wait_gather(step_, slot_):
        def body(r, c):
            for k in range(TOP_K):
                row_copy(step_, slot_, r, k).wait()
            return c
        lax.fori_loop(0, tm, body, 0, unroll=8)

    @pl.when(i == 0)
    def _():
        start_gather(0, 0)

    @pl.when(i + 1 < n_steps)
    def _():
        start_gather(i + 1, 1 - slot)

    wait_gather(i, slot)
    w = w_ref[...]
    y = ybuf[slot, 0] * w[:, 0:1]
    for k in range(1, TOP_K):
        y = y + ybuf[slot, k] * w[:, k:k + 1]
    o_ref[...] = _layer_norm(alpha * h_ref[...] + y, g_ref[...], b_ref[...])


def _combine(dest, h, wts, ybuf, ln_g, ln_b, alpha, tm):
    n, d = h.shape
    grid_spec = pltpu.PrefetchScalarGridSpec(
        num_scalar_prefetch=1,
        grid=(n // tm,),
        in_specs=[pl.BlockSpec((tm, d), lambda i, ds: (i, 0)),
                  pl.BlockSpec((tm, LANES), lambda i, ds: (i, 0)),
                  pl.BlockSpec(memory_space=pl.ANY),
                  pl.BlockSpec((1, d), lambda i, ds: (0, 0)), pl.BlockSpec((1, d), lambda i, ds: (0, 0))],
        out_specs=pl.BlockSpec((tm, d), lambda i, ds: (i, 0)),
        scratch_shapes=[pltpu.VMEM((2, TOP_K, tm, d), F32), pltpu.SemaphoreType.DMA((2,))],
    )
    return pl.pallas_call(
        functools.partial(_combine_kernel, alpha=alpha),
        out_shape=jax.ShapeDtypeStruct((n, d), F32),
        grid_spec=grid_spec,
        compiler_params=_cparams(("arbitrary",), row_dma=True),
        name="moe_combine_ln",
    )(dest, h, wts, ybuf, ln_g, ln_b)


def _rope_tables(s):
    inv_freq = ROPE_THETA ** (-2.0 * jnp.arange(HALF_DIM, dtype=F32) / HEAD_DIM)

    def tables(pos):
        ang = pos.astype(F32)[:, None] * inv_freq[None, :]
        return jnp.cos(ang), jnp.sin(ang)

    cos, sin = tables(jnp.arange(s))
    n_rows = s // CMP_STRIDE
    c_end = jnp.arange(n_rows) * CMP_STRIDE + (CMP_BLOCK - 1)
    cos_c, sin_c = tables(c_end)
    full = lambda c: jnp.concatenate([c, c], axis=1)
    signed = lambda sn: jnp.concatenate([-sn, sn], axis=1)
    return cos.T, sin.T, full(cos), signed(sin), full(cos_c), signed(sin_c)


def _overlap_t(s):
    n_rows = s // CMP_STRIDE
    n_cmp = (s - CMP_BLOCK) // CMP_STRIDE + 1
    n_sel = s // SEL_BLOCK
    c_start = np.arange(n_rows) * CMP_STRIDE
    s_start = np.arange(n_sel) * SEL_BLOCK
    ov = ((c_start[None, :] + CMP_BLOCK - 1 >= s_start[:, None])
          & (c_start[None, :] <= s_start[:, None] + SEL_BLOCK - 1)
          & (np.arange(n_rows)[None, :] < n_cmp))
    return jnp.asarray(ov.astype(np.float32), dtype=BF16)


def _dispatch_plan(eid, n_experts):
    n = eid.shape[0]
    m = n * TOP_K
    rows_per = EXPERT_ROWS
    flat_e = eid.reshape(-1)
    onehot = (flat_e[:, None] == jnp.arange(n_experts)[None, :]).astype(I32)
    pos = jnp.take_along_axis(jnp.cumsum(onehot, axis=0), flat_e[:, None], axis=1)[:, 0] - 1
    sizes = jnp.sum(onehot, axis=0)
    padded = (sizes + rows_per - 1) // rows_per * rows_per
    ends = jnp.cumsum(padded)
    dest = (ends - padded)[flat_e] + pos
    n_blk = -(-(m + n_experts * (rows_per - 1)) // rows_per)
    row_tok = jnp.zeros((n_blk * rows_per,), I32).at[dest].set(jnp.arange(m, dtype=I32) // TOP_K)
    blk0 = jnp.concatenate([jnp.zeros((1,), I32), ends.astype(I32) // rows_per])
    return dest.astype(I32), row_tok, blk0, n_blk


def kernel(x, w_in, pool_mix, pool_scale, w_pool_proj, w_nsa_proj, cmp_pos_k, cmp_pos_v, cmp_k_w1, cmp_k_w2,
           cmp_v_w1, cmp_v_w2, w_out, ln1_g, ln1_b, router_group_w, router_group_b, router_expert_w,
           router_expert_b, w_gate, w_up, w_down, ln2_g, ln2_b):
    b, s, d = x.shape
    n = b * s
    depth = w_in.shape[0]
    alpha = (2.0 * depth) ** 0.25
    pw = pool_mix.shape[1] * pool_mix.shape[2]
    qw = w_nsa_proj.shape[1]
    kvw = N_KV_GROUPS * HEAD_DIM
    n_groups, _, per_group = router_expert_w.shape[1:]
    n_experts = n_groups * per_group
    gate_w = 3 * (qw // HEAD_DIM)
    assert s % ATTN_TILE == 0 and WINDOW % ATTN_TILE == 0 and ATTN_TILE % SEL_BLOCK == 0
    assert gate_w <= LANES and n_groups + n_experts <= LANES and s // SEL_BLOCK <= HEAD_DIM

    c_q = pw
    c_kv = pw + qw
    c_gate = c_kv + 6 * kvw
    c_merge = c_gate + gate_w
    tn = 768
    width_a = -(-(c_gate + LANES) // tn) * tn
    assert c_gate % LANES == 0

    cos_t, sin_t, cos_f, sin_s, cos_c, sin_c = _rope_tables(s)
    ov_t = _overlap_t(s)
    assert c_kv % kvw == 0

    h = x.reshape(n, d)
    for l in range(depth):
        wl = w_in[l]
        w_a = jnp.pad(wl[:, :c_merge], ((0, 0), (0, width_a - c_merge))).astype(BF16)
        w_gp = wl[:, c_merge:c_merge + d].astype(BF16)
        w_ga = wl[:, c_merge + d:].astype(BF16)

        proj = _project(h, w_a, _tile(n, 1024), tn)

        mixed = _pool_mixer(proj, pool_mix[l].astype(BF16), pool_scale[l].reshape(1, pw), b, s, pw, _tile(s, 512))

        kv_blk = c_kv // kvw
        kcmp, vcmp_t = _compress(
            proj, kv_blk, b, s,
            cmp_pos_k[l].reshape(1, CMP_BLOCK * HEAD_DIM), cmp_pos_v[l].reshape(1, CMP_BLOCK * HEAD_DIM),
            cmp_k_w1[l].astype(BF16), cmp_k_w2[l].astype(BF16), cmp_v_w1[l].astype(BF16),
            cmp_v_w2[l].T.astype(BF16), cos_c, sin_c)
        ksx, vs_t, kwx, vw_t = _kv_prep(proj, kv_blk, b, s, cos_f, sin_s)

        attn = _attention(proj, cos_t, sin_t, kcmp, vcmp_t, ksx, vs_t, kwx, vw_t, ov_t,
                          b=b, s=s, pw=pw, qw=qw, gate_col=c_gate)

        h = _merge_out(h, mixed, attn, w_pool_proj[l].astype(BF16), w_nsa_proj[l].astype(BF16), w_gp, w_ga,
                       w_out[l].astype(BF16), ln1_g[l].reshape(1, d), ln1_b[l].reshape(1, d), alpha,
                       _tile(n, 512), _tile(d, 512))

        w_r = jnp.concatenate([router_group_w[l], router_expert_w[l].transpose(1, 0, 2).reshape(d, n_experts)], axis=1)
        w_r = jnp.pad(w_r, ((0, 0), (0, LANES - w_r.shape[1])))
        w_r_hi = w_r.astype(BF16)
        w_r_lo = (w_r - w_r_hi.astype(F32)).astype(BF16)
        b_r = jnp.pad(jnp.concatenate([router_group_b[l], router_expert_b[l].reshape(-1)]),
                      (0, LANES - n_groups - n_experts)).reshape(1, LANES)
        eid_l, wts_l = _router(h, w_r_hi, w_r_lo, b_r, n_groups, per_group, _tile(n, 512))
        dest, row_tok, blk0, n_blk = _dispatch_plan(eid_l[:, :TOP_K], n_experts)

        ybuf = _experts(blk0, row_tok, h, w_gate[l], w_up[l], w_down[l], n_blk)
        h = _combine(dest, h, wts_l, ybuf, ln2_g[l].reshape(1, d), ln2_b[l].reshape(1, d), alpha, _tile(n, 256))
    return h.reshape(b, s, d)
```

```python
import functools
import math

import numpy as np
import jax
import jax.numpy as jnp
from jax import lax
from jax.experimental import pallas as pl
from jax.experimental.pallas import tpu as pltpu

F32 = jnp.float32
BF16 = jnp.bfloat16
I32 = jnp.int32

POOL_WINDOWS = (2, 4, 8, 16)
POOL_GROUPS = 4
POOL_HALO = 16
HEAD_DIM = 64
HALF_DIM = HEAD_DIM // 2
V_ROWS = HEAD_DIM + 16
N_KV_GROUPS = 4
CMP_BLOCK = 32
CMP_STRIDE = 16
SEL_BLOCK = 64
SEL_TOP_N = 16
WINDOW = 512
ROPE_THETA = 10000.0
FORCE_SCORE = 1e6
TOP_K = 2
LN_EPS = 1e-5
NEG = -1e30

LANES = 128
SUBLANES = 8
VMEM_BYTES_V7X = 64 * 1024 * 1024
VMEM_LIMIT = VMEM_BYTES_V7X - 8 * 1024 * 1024

ATTN_TILE = 256
EXPERT_ROWS = 256


def _cparams(sem, vmem=VMEM_LIMIT, row_dma=False):
    return pltpu.CompilerParams(dimension_semantics=sem, vmem_limit_bytes=vmem, disable_bounds_checks=row_dma)


def _tile(n, pref):
    t = min(n, pref)
    while n % t:
        t //= 2
    return t


def _proj_kernel(x_ref, w_ref, o_ref, xb_ref):
    @pl.when(pl.program_id(1) == 0)
    def _():
        xb_ref[...] = x_ref[...].astype(BF16)

    o_ref[...] = jnp.dot(xb_ref[...], w_ref[...], preferred_element_type=F32)


def _project(x, w, tm, tn):
    n, d = x.shape
    cols = w.shape[1]
    return pl.pallas_call(
        _proj_kernel,
        out_shape=jax.ShapeDtypeStruct((n, cols), F32),
        grid=(n // tm, cols // tn),
        in_specs=[pl.BlockSpec((tm, d), lambda i, j: (i, 0)),
                  pl.BlockSpec((d, tn), lambda i, j: (0, j))],
        out_specs=pl.BlockSpec((tm, tn), lambda i, j: (i, j)),
        scratch_shapes=[pltpu.VMEM((tm, d), BF16)],
        compiler_params=_cparams(("arbitrary", "arbitrary")),
        name="in_proj",
    )(x, w)


def _pool_kernel(u_ref, halo_ref, mix_ref, scale_ref, o_ref, ext_ref):
    i = pl.program_id(1)
    ts = u_ref.shape[0]
    gd = mix_ref.shape[1]
    ext_ref[POOL_HALO:, :] = u_ref[...]
    ext_ref[:POOL_HALO, :] = jnp.where(i == 0, 0.0, halo_ref[...])
    t = i * ts + lax.broadcasted_iota(I32, (ts, gd), 0)
    for g, w in enumerate(POOL_WINDOWS):
        cols = slice(g * gd, (g + 1) * gd)
        s = ext_ref[:, cols]
        k = 1
        while k < w:
            s = s + pltpu.roll(s, k, axis=0)
            k *= 2
        cnt = jnp.minimum(t + 1, w).astype(F32)
        pooled = s[POOL_HALO:, :] / cnt - u_ref[:, cols]
        mixed = jnp.dot(pooled.astype(BF16), mix_ref[g], preferred_element_type=F32)
        o_ref[:, cols] = (mixed * scale_ref[:, cols]).astype(BF16)


def _pool_mixer(proj, pool_mix_b, pool_scale, b, s, pw, ts):
    ns = s // ts
    hb = ts // POOL_HALO
    return pl.pallas_call(
        _pool_kernel,
        out_shape=jax.ShapeDtypeStruct((b * s, pw), BF16),
        grid=(b, ns),
        in_specs=[pl.BlockSpec((ts, pw), lambda bi, i: (bi * ns + i, 0)),
                  pl.BlockSpec((POOL_HALO, pw), lambda bi, i: (jnp.maximum((bi * ns + i) * hb - 1, 0), 0)),
                  pl.BlockSpec(pool_mix_b.shape, lambda bi, i: (0, 0, 0)),
                  pl.BlockSpec((1, pw), lambda bi, i: (0, 0))],
        out_specs=pl.BlockSpec((ts, pw), lambda bi, i: (bi * ns + i, 0)),
        scratch_shapes=[pltpu.VMEM((POOL_HALO + ts, pw), F32)],
        compiler_params=_cparams(("arbitrary", "arbitrary")),
        name="pool_mixer",
    )(proj, proj, pool_mix_b, pool_scale)


def _gelu_tanh(x):
    return 0.5 * x * (1.0 + jnp.tanh(math.sqrt(2.0 / math.pi) * (x + 0.044715 * (x * x * x))))


def _swap_halves(x):
    return jnp.concatenate([x[:, HALF_DIM:], x[:, :HALF_DIM]], axis=1)


def _cmp_kernel(kc0_ref, kc1_ref, vc0_ref, vc1_ref, pk_ref, pv_ref, kw1_ref, kw2_ref, vw1_ref, vw2t_ref,
                cos_ref, sin_ref, ko_ref, vo_ref, r_s):
    kc_refs = (kc0_ref, kc1_ref)
    vc_refs = (vc0_ref, vc1_ref)
    s = kc0_ref.shape[0]
    nrow = s // CMP_STRIDE
    half = CMP_STRIDE * HEAD_DIM

    def hidden(p_ref, w1_ref, g):
        r = r_s[g]
        r_next = pltpu.roll(r, nrow - 1, axis=0)
        a = (r + p_ref[:, :half]).astype(BF16)
        bb = (r_next + p_ref[:, half:]).astype(BF16)
        h = (jnp.dot(a, w1_ref[:half, :], preferred_element_type=F32)
             + jnp.dot(bb, w1_ref[half:, :], preferred_element_type=F32))
        return _gelu_tanh(h).astype(BF16)

    gpl = LANES // HEAD_DIM

    def regroup(src_refs):
        for t in range(CMP_STRIDE):
            for j, src_ref in enumerate(src_refs):
                rows = src_ref[pl.ds(t, nrow, stride=CMP_STRIDE), :]
                for gg in range(gpl):
                    r_s[j * gpl + gg, :, t * HEAD_DIM:(t + 1) * HEAD_DIM] = rows[:, gg * HEAD_DIM:(gg + 1) * HEAD_DIM]

    regroup(kc_refs)
    for g in range(N_KV_GROUPS):
        k = jnp.dot(hidden(pk_ref, kw1_ref, g), kw2_ref[...], preferred_element_type=F32)
        k = k * cos_ref[...] + _swap_halves(k) * sin_ref[...]
        ko_ref[0, g] = k.astype(BF16)
    regroup(vc_refs)
    for g in range(N_KV_GROUPS):
        hv = hidden(pv_ref, vw1_ref, g)
        vt = lax.dot_general(vw2t_ref[...], hv, (((1,), (1,)), ((), ())), preferred_element_type=F32)
        vo_ref[0, g] = vt.astype(BF16)


def _compress(proj, col_blk, b, s, pk, pv, kw1, kw2, vw1, vw2t, cos_c, sin_c):
    g = N_KV_GROUPS
    kvw = g * HEAD_DIM
    nrow = s // CMP_STRIDE

    def whole(a):
        return pl.BlockSpec(a.shape, lambda bi: (0,) * a.ndim)

    assert kvw == 2 * LANES

    def lane_tile(j):
        return pl.BlockSpec((s, LANES), lambda bi: (bi, col_blk * (kvw // LANES) + j))

    return pl.pallas_call(
        _cmp_kernel,
        out_shape=(jax.ShapeDtypeStruct((b, g, nrow, HEAD_DIM), BF16),
                   jax.ShapeDtypeStruct((b, g, HEAD_DIM, nrow), BF16)),
        grid=(b,),
        in_specs=[lane_tile(0), lane_tile(1), lane_tile(2), lane_tile(3),
                  whole(pk), whole(pv), whole(kw1), whole(kw2), whole(vw1), whole(vw2t),
                  whole(cos_c), whole(sin_c)],
        out_specs=(pl.BlockSpec((1, g, nrow, HEAD_DIM), lambda bi: (bi, 0, 0, 0)),
                   pl.BlockSpec((1, g, HEAD_DIM, nrow), lambda bi: (bi, 0, 0, 0))),
        scratch_shapes=[pltpu.VMEM((g, nrow, CMP_STRIDE * HEAD_DIM), F32)],
        compiler_params=_cparams(("arbitrary",)),
        name="compress_kv",
    )(proj, proj, proj, proj, pk, pv, kw1, kw2, vw1, vw2t, cos_c, sin_c)


def _kvprep_kernel(ks_ref, vs_ref, kw_ref, vw_ref, cos_ref, sin_ref, kso_ref, vso_ref, kwo_ref, vwo_ref):
    i = pl.program_id(1)
    tq = ks_ref.shape[0]
    cos = cos_ref[...]
    sin = sin_ref[...]
    blk = (i * tq + lax.broadcasted_iota(I32, (tq, HEAD_DIM), 0)) // SEL_BLOCK
    onehot = jnp.where(blk == lax.broadcasted_iota(I32, (tq, HEAD_DIM), 1), 1.0, 0.0).astype(BF16)
    ks = ks_ref[...]
    kw = kw_ref[...]
    vs_t = vs_ref[...].T
    vw_t = vw_ref[...].T
    for g in range(N_KV_GROUPS):
        cols = slice(g * HEAD_DIM, (g + 1) * HEAD_DIM)
        k = ks[:, cols]
        kso_ref[0, g, 0, :, :HEAD_DIM] = (k * cos + _swap_halves(k) * sin).astype(BF16)
        kso_ref[0, g, 0, :, HEAD_DIM:] = onehot
        k = kw[:, cols]
        kwo_ref[0, g, 0] = (k * cos + _swap_halves(k) * sin).astype(BF16)
        ones = jnp.ones((V_ROWS - HEAD_DIM, tq), BF16)
        vso_ref[0, g, 0, :HEAD_DIM] = vs_t[cols].astype(BF16)
        vso_ref[0, g, 0, HEAD_DIM:] = ones
        vwo_ref[0, g, 0, :HEAD_DIM] = vw_t[cols].astype(BF16)
        vwo_ref[0, g, 0, HEAD_DIM:] = ones


def _kv_prep(proj, col_blk, b, s, cos_f, sin_s):
    g = N_KV_GROUPS
    kvw = g * HEAD_DIM
    tq = ATTN_TILE
    nq = s // tq

    def col(j):
        return pl.BlockSpec((tq, kvw), lambda bi, i: (bi * nq + i, col_blk + j))

    tab = pl.BlockSpec((tq, HEAD_DIM), lambda bi, i: (i, 0))

    def out(r, c):
        return (jax.ShapeDtypeStruct((b, g, nq, r, c), BF16),
                pl.BlockSpec((1, g, 1, r, c), lambda bi, i: (bi, 0, i, 0, 0)))

    outs = [out(tq, 2 * HEAD_DIM), out(V_ROWS, tq), out(tq, HEAD_DIM), out(V_ROWS, tq)]
    return pl.pallas_call(
        _kvprep_kernel,
        out_shape=tuple(o[0] for o in outs),
        grid=(b, nq),
        in_specs=[col(2), col(3), col(4), col(5), tab, tab],
        out_specs=tuple(o[1] for o in outs),
        compiler_params=_cparams(("arbitrary", "arbitrary")),
        name="kv_prep",
    )(proj, proj, proj, proj, cos_f, sin_s)


def _attn_kernel(q_ref, cos_ref, sin_ref, gate_ref, kc_ref, vct_ref, ks_ref, vst_ref, kw_ref, vwt_ref, ovt_ref,
                 o_ref, qa_s, g_s, imp_s, m_s, c_s, acc_s, out_s, sc_s, pc_s, psum_s, s_s, p_s, *, hg, n_heads):
    g = pl.program_id(1)
    i = pl.program_id(2)
    tq = q_ref.shape[0]
    tk = tq
    ncp = kc_ref.shape[2]
    nsel = ovt_ref.shape[0]
    scale = HEAD_DIM ** -0.5 * math.log2(math.e)

    qt = q_ref[...].T
    cos = cos_ref[...]
    sin = sin_ref[...]
    for h in range(hg):
        x1 = qt[h * HEAD_DIM:h * HEAD_DIM + HALF_DIM]
        x2 = qt[h * HEAD_DIM + HALF_DIM:(h + 1) * HEAD_DIM]
        qa_s[h, :HALF_DIM] = ((x1 * cos - x2 * sin) * scale).astype(BF16)
        qa_s[h, HALF_DIM:HEAD_DIM] = ((x2 * cos + x1 * sin) * scale).astype(BF16)
    g_s[...] = gate_ref[...].T

    def gate(branch, h):
        row = g_s[pl.ds(branch * n_heads + g * hg + h, 1), :]
        return 1.0 / (1.0 + jnp.exp(-row))

    chunks = [slice(c * LANES, (c + 1) * LANES) for c in range(tq // LANES)]

    def qk(dst, kx, qrows):
        for h in range(hg):
            dst[h] = jnp.dot(kx, qa_s[h, :qrows], preferred_element_type=F32)

    qk(sc_s, kc_ref[0, 0], HEAD_DIM)
    c_end = lax.broadcasted_iota(I32, (ncp, LANES), 0) * CMP_STRIDE + (CMP_BLOCK - 1)
    for c, cols in enumerate(chunks):
        cmask = c_end <= i * tq + c * LANES + lax.broadcasted_iota(I32, (ncp, LANES), 1)
        for h in range(hg):
            sm = jnp.where(cmask, sc_s[h, :, cols], NEG)
            m = jnp.max(sm, axis=0, keepdims=True)
            p = jnp.where(cmask, jnp.exp2(sm - m), 0.0)
            den = jnp.maximum(jnp.sum(p, axis=0, keepdims=True), 1e-30)
            pn = p / den
            psum_s[:, cols] = pn if h == 0 else psum_s[:, cols] + pn
            pc_s[h, :, cols] = pn.astype(BF16)
    for h in range(hg):
        oc = jnp.dot(vct_ref[0, 0], pc_s[h], preferred_element_type=F32)
        out_s[h * HEAD_DIM:(h + 1) * HEAD_DIM, :] = oc * gate(0, h)

    psum = psum_s[...]
    p_hi = psum.astype(BF16)
    p_lo = (psum - p_hi.astype(F32)).astype(BF16)
    imp = (jnp.dot(ovt_ref[...], p_hi, preferred_element_type=F32)
           + jnp.dot(ovt_ref[...], p_lo, preferred_element_type=F32))
    jb = lax.broadcasted_iota(I32, (nsel, tq), 0)
    cur = (i * tq + lax.broadcasted_iota(I32, (nsel, tq), 1)) // SEL_BLOCK
    forced = (jb == 0) | (jb == cur) | (jb == cur - 1)
    imp = jnp.where(forced, FORCE_SCORE, jnp.where(jb > cur, -FORCE_SCORE, imp))
    imp_s[...] = imp
    nrb = nsel // SUBLANES
    blocks = [imp[r * SUBLANES:(r + 1) * SUBLANES] for r in range(nrb)]
    counts = [jnp.zeros((SUBLANES, tq), F32) for _ in range(nrb)]
    sub = lax.broadcasted_iota(I32, (SUBLANES, tq), 0)
    for jp in range(nsel):
        row = jnp.broadcast_to(imp_s[jp:jp + 1, :], (SUBLANES, tq))
        for r in range(nrb):
            if r * SUBLANES > jp:
                inc = jnp.where(row >= blocks[r], 1.0, 0.0)
            elif r * SUBLANES + SUBLANES - 1 <= jp:
                inc = jnp.where(row > blocks[r], 1.0, 0.0)
            else:
                tie = jnp.where(sub > jp - r * SUBLANES, 1.0, 0.0)
                inc = jnp.where(row > blocks[r], 1.0, jnp.where(row >= blocks[r], tie, 0.0))
            counts[r] = counts[r] + inc
    top_n = min(SEL_TOP_N, nsel)
    bias = jnp.concatenate([jnp.where(cnt < top_n, 0.0, NEG) for cnt in counts], axis=0).astype(BF16)
    for h in range(hg):
        qa_s[h, HEAD_DIM:HEAD_DIM + nsel] = bias
        if nsel < HEAD_DIM:
            qa_s[h, HEAD_DIM + nsel:] = jnp.zeros((HEAD_DIM - nsel, tq), BF16)

    def reset():
        m_s[...] = jnp.full(m_s.shape, NEG, F32)
        acc_s[...] = jnp.zeros(acc_s.shape, F32)

    krow = lax.broadcasted_iota(I32, (tk, LANES), 0)
    lane = lax.broadcasted_iota(I32, (tk, LANES), 1)

    def softmax_pv(src, vt, kind, skip=None):
        off = None if skip is None else jnp.where(skip, tk, 0)
        for c, cols in enumerate(chunks):
            qcol = lane + c * LANES

            if kind == 'causal':
                mask = krow <= qcol
            elif kind == 'older':
                mask = krow > (qcol if off is None else qcol + off)
            else:
                mask = None if off is None else krow >= off
            for h in range(hg):
                s = src[h, :, cols]
                if mask is not None:
                    s = jnp.where(mask, s, NEG)
                m_old = m_s[h, :, cols]
                m_new = jnp.maximum(m_old, jnp.max(s, axis=0, keepdims=True))
                m_s[h, :, cols] = m_new
                c_s[h, :, cols] = jnp.exp2(m_old - m_new)
                p_s[h, :, cols] = jnp.exp2(s - m_new).astype(BF16)
        for h in range(hg):
            acc_s[h] = acc_s[h] * c_s[h] + jnp.dot(vt, p_s[h], preferred_element_type=F32)

    def finish(branch):
        for h in range(hg):
            rows = slice(h * HEAD_DIM, (h + 1) * HEAD_DIM)
            den = acc_s[h, HEAD_DIM:HEAD_DIM + 1, :]
            out_s[rows, :] = out_s[rows, :] + acc_s[h, :HEAD_DIM, :] * (gate(branch, h) / den)

    s_a, s_b, s_c = s_s.at[0], s_s.at[1], s_s.at[2]
    sel_q = 2 * HEAD_DIM

    reset()
    qk(s_a, ks_ref[0, 0, 0], sel_q)

    def sel_pair(pi, carry):
        kt = 2 * pi
        qk(s_b, ks_ref[0, 0, kt + 1], sel_q)
        softmax_pv(s_a, vst_ref[0, 0, kt], None)
        qk(s_a, ks_ref[0, 0, kt + 2], sel_q)
        softmax_pv(s_b, vst_ref[0, 0, kt + 1], None)
        return carry

    lax.fori_loop(0, i // 2, sel_pair, 0)

    @pl.when(i % 2 == 0)
    def _():
        softmax_pv(s_a, vst_ref[0, 0, i], 'causal')

    @pl.when(i % 2 == 1)
    def _():
        qk(s_b, ks_ref[0, 0, i], sel_q)
        softmax_pv(s_a, vst_ref[0, 0, i - 1], None)
        softmax_pv(s_b, vst_ref[0, 0, i], 'causal')

    finish(1)

    reset()
    nwt = WINDOW // tk
    assert nwt == 2
    t_old = jnp.maximum(i - 2, 0)
    t_mid = jnp.maximum(i - 1, 0)
    qk(s_a, kw_ref[0, 0, t_old], HEAD_DIM)
    qk(s_b, kw_ref[0, 0, t_mid], HEAD_DIM)
    qk(s_c, kw_ref[0, 0, i], HEAD_DIM)
    softmax_pv(s_a, vwt_ref[0, 0, t_old], 'older', skip=i < 2)
    softmax_pv(s_b, vwt_ref[0, 0, t_mid], None, skip=i < 1)
    softmax_pv(s_c, vwt_ref[0, 0, i], 'causal')
    finish(2)

    o_ref[...] = out_s[...].T.astype(o_ref.dtype)


def _attention(proj, cos_t, sin_t, kcmp, vcmp_t, ksx, vs_t, kwx, vw_t, ov_t, *, b, s, pw, qw, gate_col):
    tq = ATTN_TILE
    nq = s // tq
    g = N_KV_GROUPS
    gw = qw // g
    hg = gw // HEAD_DIM
    n_heads = qw // HEAD_DIM
    ncp = kcmp.shape[2]
    nsel = ov_t.shape[0]
    q_blk0 = pw // gw
    kern = functools.partial(_attn_kernel, hg=hg, n_heads=n_heads)
    kv5 = lambda bi, gi, i: (bi, gi, 0, 0, 0)
    kv4 = lambda bi, gi, i: (bi, gi, 0, 0)
    return pl.pallas_call(
        kern,
        out_shape=jax.ShapeDtypeStruct((b * s, qw), BF16),
        grid=(b, g, nq),
        in_specs=[pl.BlockSpec((tq, gw), lambda bi, gi, i: (bi * nq + i, q_blk0 + gi)),
                  pl.BlockSpec((HALF_DIM, tq), lambda bi, gi, i: (0, i)),
                  pl.BlockSpec((HALF_DIM, tq), lambda bi, gi, i: (0, i)),
                  pl.BlockSpec((tq, LANES), lambda bi, gi, i: (bi * nq + i, gate_col // LANES)),
                  pl.BlockSpec((1, 1, ncp, HEAD_DIM), kv4),
                  pl.BlockSpec((1, 1, HEAD_DIM, ncp), kv4),
                  pl.BlockSpec((1, 1, nq, tq, 2 * HEAD_DIM), kv5),
                  pl.BlockSpec((1, 1, nq, V_ROWS, tq), kv5),
                  pl.BlockSpec((1, 1, nq, tq, HEAD_DIM), kv5),
                  pl.BlockSpec((1, 1, nq, V_ROWS, tq), kv5),
                  pl.BlockSpec(ov_t.shape, lambda bi, gi, i: (0, 0))],
        out_specs=pl.BlockSpec((tq, gw), lambda bi, gi, i: (bi * nq + i, gi)),
        scratch_shapes=[pltpu.VMEM((hg, 2 * HEAD_DIM, tq), BF16),
                        pltpu.VMEM((LANES, tq), F32),
                        pltpu.VMEM((nsel, tq), F32),
                        pltpu.VMEM((hg, 1, tq), F32),
                        pltpu.VMEM((hg, 1, tq), F32),
                        pltpu.VMEM((hg, V_ROWS, tq), F32),
                        pltpu.VMEM((gw, tq), F32),
                        pltpu.VMEM((hg, ncp, tq), F32),
                        pltpu.VMEM((hg, ncp, tq), BF16),
                        pltpu.VMEM((ncp, tq), F32),
                        pltpu.VMEM((3, hg, tq, tq), F32),
                        pltpu.VMEM((hg, tq, tq), BF16)],
        compiler_params=_cparams(("arbitrary", "arbitrary", "arbitrary")),
        name="nsa_attention",
    )(proj, cos_t, sin_t, proj, kcmp, vcmp_t, ksx, vs_t, kwx, vw_t, ov_t)


def _layer_norm(r, g, b):
    mu = jnp.mean(r, axis=-1, keepdims=True)
    d = r - mu
    var = jnp.mean(d * d, axis=-1, keepdims=True)
    return d * lax.rsqrt(var + LN_EPS) * g + b


def _sigmoid(x):
    return 1.0 / (1.0 + jnp.exp(-x))


def _pack_rows(x, dst_ref):
    n = x.shape[0]
    tr = x.shape[1] // LANES
    for j in range(tr):
        dst_ref[pl.ds(j, n, stride=tr), :] = x[:, j * LANES:(j + 1) * LANES]


def _unpack_rows(src_ref, n, dtype):
    tr = src_ref.shape[0] // n
    return jnp.concatenate([src_ref[pl.ds(j, n, stride=tr), :].astype(dtype) for j in range(tr)], axis=1)


def _merge_kernel(x_ref, pm_ref, at_ref, wpp_ref, wnp_ref, wgp_ref, wga_ref, wo_ref, g_ref, b_ref, hp_ref,
                  xb_ref, acc_ref, *, alpha):
    c = pl.program_id(1)

    @pl.when(c == 0)
    def _():
        xb_ref[...] = x_ref[...].astype(BF16)
        acc_ref[...] = jnp.zeros(acc_ref.shape, F32)

    xb = xb_ref[...]
    y_pool = jnp.dot(pm_ref[...], wpp_ref[...], preferred_element_type=F32)
    y_attn = jnp.dot(at_ref[...], wnp_ref[...], preferred_element_type=F32)
    g_pool = _sigmoid(jnp.dot(xb, wgp_ref[...], preferred_element_type=F32))
    g_attn = _sigmoid(jnp.dot(xb, wga_ref[...], preferred_element_type=F32))
    z = g_pool * y_pool + g_attn * y_attn
    acc_ref[...] += jnp.dot(z.astype(BF16), wo_ref[...], preferred_element_type=F32)

    @pl.when(c == pl.num_programs(1) - 1)
    def _():
        h = _layer_norm(alpha * x_ref[...] + acc_ref[...], g_ref[...], b_ref[...])
        _pack_rows(h, hp_ref)


def _merge_out(x, pm, at, wpp, wnp, wgp, wga, wo, ln_g, ln_b, alpha, tm, ck):
    n, d = x.shape
    pw = pm.shape[1]
    qw = at.shape[1]
    row = lambda i, c: (i, 0)
    colc = lambda i, c: (0, c)
    return pl.pallas_call(
        functools.partial(_merge_kernel, alpha=alpha),
        out_shape=jax.ShapeDtypeStruct((n * d // LANES, LANES), F32),
        grid=(n // tm, d // ck),
        in_specs=[pl.BlockSpec((tm, d), row), pl.BlockSpec((tm, pw), row), pl.BlockSpec((tm, qw), row),
                  pl.BlockSpec((pw, ck), colc), pl.BlockSpec((qw, ck), colc),
                  pl.BlockSpec((d, ck), colc), pl.BlockSpec((d, ck), colc),
                  pl.BlockSpec((ck, d), lambda i, c: (c, 0)),
                  pl.BlockSpec((1, d), lambda i, c: (0, 0)), pl.BlockSpec((1, d), lambda i, c: (0, 0))],
        out_specs=pl.BlockSpec((tm * d // LANES, LANES), row),
        scratch_shapes=[pltpu.VMEM((tm, d), BF16), pltpu.VMEM((tm, d), F32)],
        compiler_params=_cparams(("arbitrary", "arbitrary")),
        name="merge_out_ln",
    )(x, pm, at, wpp, wnp, wgp, wga, wo, ln_g, ln_b)


def _router_kernel(h_ref, whi_ref, wlo_ref, b_ref, e_ref, w_ref, *, n_groups, per_group):
    h = _unpack_rows(h_ref, e_ref.shape[0], F32)
    h_hi = h.astype(BF16)
    h_lo = (h - h_hi.astype(F32)).astype(BF16)
    logits = (jnp.dot(h_hi, whi_ref[...], preferred_element_type=F32)
              + jnp.dot(h_hi, wlo_ref[...], preferred_element_type=F32)
              + jnp.dot(h_lo, whi_ref[...], preferred_element_type=F32)) + b_ref[...]
    lane = lax.broadcasted_iota(I32, logits.shape, 1)
    far = LANES

    def first_argmax(v, vmax):
        return jnp.min(jnp.where(v == vmax, lane, far), axis=-1, keepdims=True)

    gl = jnp.where(lane < n_groups, logits, NEG)
    gmax = jnp.max(gl, axis=-1, keepdims=True)
    grp = first_argmax(gl, gmax)
    gsum = jnp.sum(jnp.where(lane < n_groups, jnp.exp(gl - gmax), 0.0), axis=-1, keepdims=True)
    g_gate = 1.0 / gsum
    lo = n_groups + grp * per_group
    el = jnp.where((lane >= lo) & (lane < lo + per_group), logits, NEG)
    v1 = jnp.max(el, axis=-1, keepdims=True)
    i1 = first_argmax(el, v1)
    el2 = jnp.where(lane == i1, NEG, el)
    v2 = jnp.max(el2, axis=-1, keepdims=True)
    i2 = first_argmax(el2, v2)
    e21 = jnp.exp(v2 - v1)
    w1 = g_gate / (1.0 + e21)
    w2 = g_gate * e21 / (1.0 + e21)
    e_ref[...] = jnp.where(lane == 0, i1 - n_groups, jnp.where(lane == 1, i2 - n_groups, 0))
    w_ref[...] = jnp.where(lane == 0, w1, jnp.where(lane == 1, w2, 0.0))


def _router(hp, w_hi, w_lo, bias, n_groups, per_group, tm):
    d = w_hi.shape[0]
    tr = d // LANES
    n = hp.shape[0] // tr
    return pl.pallas_call(
        functools.partial(_router_kernel, n_groups=n_groups, per_group=per_group),
        out_shape=(jax.ShapeDtypeStruct((n, LANES), I32), jax.ShapeDtypeStruct((n, LANES), F32)),
        grid=(n // tm,),
        in_specs=[pl.BlockSpec((tm * tr, LANES), lambda i: (i, 0)),
                  pl.BlockSpec((d, LANES), lambda i: (0, 0)), pl.BlockSpec((d, LANES), lambda i: (0, 0)),
                  pl.BlockSpec((1, LANES), lambda i: (0, 0))],
        out_specs=(pl.BlockSpec((tm, LANES), lambda i: (i, 0)), pl.BlockSpec((tm, LANES), lambda i: (i, 0))),
        compiler_params=_cparams(("arbitrary",)),
        name="moe_router",
    )(hp, w_hi, w_lo, bias)


def _expert_kernel(blk0_ref, row_tok_ref, hp_hbm, wg_ref, wu_ref, wd_ref, y_hbm,
                   xbuf, ybuf, wgb, wub, wdb, gsem, ysem):
    e = pl.program_id(0)
    n_exp = pl.num_programs(0)
    tr = wg_ref.shape[1] // LANES
    rows = xbuf.shape[1] // tr
    b0 = blk0_ref[e]
    nb = blk0_ref[e + 1] - b0
    n_used = blk0_ref[n_exp]

    def row_copy(blk, slot, r):
        src = pl.multiple_of(row_tok_ref[blk * rows + r], tr)
        dst = pl.multiple_of(r * tr, tr)
        return pltpu.make_async_copy(hp_hbm.at[pl.ds(src, tr)], xbuf.at[slot, pl.ds(dst, tr)], gsem.at[slot])

    def start_gather(blk, slot):
        def body(r, c):
            row_copy(blk, slot, r).start()
            return c
        lax.fori_loop(0, rows, body, 0, unroll=8)

    def wait_gather(blk, slot):
        def body(r, c):
            row_copy(blk, slot, r).wait()
            return c
        lax.fori_loop(0, rows, body, 0, unroll=8)

    def y_copy(blk, slot):
        dst = pl.multiple_of(blk * rows * tr, tr)
        return pltpu.make_async_copy(ybuf.at[slot], y_hbm.at[pl.ds(dst, rows * tr)], ysem.at[slot])

    @pl.when((e == 0) & (n_used > 0))
    def _():
        start_gather(0, 0)

    @pl.when(nb > 0)
    def _():
        wgb[...] = wg_ref[0].astype(BF16)
        wub[...] = wu_ref[0].astype(BF16)
        wdb[...] = wd_ref[0].astype(BF16)

    def block(j, carry):
        b = b0 + j
        slot = b % 2

        @pl.when(b + 1 < n_used)
        def _():
            start_gather(b + 1, 1 - slot)

        wait_gather(b, slot)
        x = _unpack_rows(xbuf.at[slot], rows, BF16)
        hgate = jnp.dot(x, wgb[...], preferred_element_type=F32)
        hup = jnp.dot(x, wub[...], preferred_element_type=F32)
        a = (hgate * _sigmoid(hgate) * hup).astype(BF16)
        y = jnp.dot(a, wdb[...], preferred_element_type=F32)

        @pl.when(b >= 2)
        def _():
            y_copy(b - 2, slot).wait()

        _pack_rows(y, ybuf.at[slot])
        y_copy(b, slot).start()
        return carry

    lax.fori_loop(0, nb, block, 0)

    @pl.when(e == n_exp - 1)
    def _():
        for back in (2, 1):
            @pl.when(n_used >= back)
            def _(back=back):
                y_copy(n_used - back, (n_used - back) % 2).wait()

        n_blk = y_hbm.shape[0] // (rows * tr)
        ybuf[0] = jnp.zeros(ybuf.shape[1:], ybuf.dtype)

        def zero_start(blk, c):
            y_copy(blk, 0).start()
            return c

        def zero_wait(blk, c):
            y_copy(blk, 0).wait()
            return c

        lax.fori_loop(n_used, n_blk, zero_start, 0)
        lax.fori_loop(n_used, n_blk, zero_wait, 0)


def _experts(blk0, row_tok, hp, w_gate, w_up, w_down, n_blk):
    n_exp = w_gate.shape[0]
    rows = EXPERT_ROWS
    d = w_gate.shape[1]
    hid = w_gate.shape[2]
    grid_spec = pltpu.PrefetchScalarGridSpec(
        num_scalar_prefetch=2,
        grid=(n_exp,),
        in_specs=[pl.BlockSpec(memory_space=pl.ANY),
                  pl.BlockSpec((1, d, hid), lambda e, b0, rt: (e, 0, 0)),
                  pl.BlockSpec((1, d, hid), lambda e, b0, rt: (e, 0, 0)),
                  pl.BlockSpec((1, hid, d), lambda e, b0, rt: (e, 0, 0))],
        out_specs=pl.BlockSpec(memory_space=pl.ANY),
        scratch_shapes=[pltpu.VMEM((2, rows * d // LANES, LANES), F32), pltpu.VMEM((2, rows * d // LANES, LANES), F32),
                        pltpu.VMEM((d, hid), BF16), pltpu.VMEM((d, hid), BF16), pltpu.VMEM((hid, d), BF16),
                        pltpu.SemaphoreType.DMA((2,)), pltpu.SemaphoreType.DMA((2,))],
    )
    return pl.pallas_call(
        _expert_kernel,
        out_shape=jax.ShapeDtypeStruct((n_blk * rows * d // LANES, LANES), F32),
        grid_spec=grid_spec,
        compiler_params=_cparams(("arbitrary",), row_dma=True),
        name="moe_experts",
    )(blk0, row_tok, hp, w_gate, w_up, w_down)


def _combine_kernel(dest_ref, h_ref, w_ref, y_hbm, g_ref, b_ref, o_ref, ybuf, sem, *, alpha):
    i = pl.program_id(0)
    n_steps = pl.num_programs(0)
    tm = o_ref.shape[0]
    tr = o_ref.shape[1] // LANES
    slot = i % 2

    def row_copy(step_, slot_, r, k):
        src = pl.multiple_of(dest_ref[(step_ * tm + r) * TOP_K + k], tr)
        dst = pl.multiple_of(r * tr, tr)
        return pltpu.make_async_copy(y_hbm.at[pl.ds(src, tr)], ybuf.at[slot_, k, pl.ds(dst, tr)], sem.at[slot_])

    def start_gather(step_, slot_):
        def body(r, c):
            for k in range(TOP_K):
                row_copy(step_, slot_, r, k).start()
            return c
        lax.fori_loop(0, tm, body, 0, unroll=8)

    def wait_gather(step_, slot_):
        def body(r, c):
            for k in range(TOP_K):
                row_copy(step_, slot_, r, k).wait()
            return c
        lax.fori_loop(0, tm, body, 0, unroll=8)

    @pl.when(i == 0)
    def _():
        start_gather(0, 0)

    @pl.when(i + 1 < n_steps)
    def _():
        start_gather(i + 1, 1 - slot)

    wait_gather(i, slot)
    w = w_ref[...]
    y = _unpack_rows(ybuf.at[slot, 0], tm, F32) * w[:, 0:1]
    for k in range(1, TOP_K):
        y = y + _unpack_rows(ybuf.at[slot, k], tm, F32) * w[:, k:k + 1]
    h = _unpack_rows(h_ref, tm, F32)
    o_ref[...] = _layer_norm(alpha * h + y, g_ref[...], b_ref[...])


def _combine(dest, hp, wts, ybuf, ln_g, ln_b, alpha, tm):
    d = ln_g.shape[1]
    n = hp.shape[0] * LANES // d
    grid_spec = pltpu.PrefetchScalarGridSpec(
        num_scalar_prefetch=1,
        grid=(n // tm,),
        in_specs=[pl.BlockSpec((tm * d // LANES, LANES), lambda i, ds: (i, 0)),
                  pl.BlockSpec((tm, LANES), lambda i, ds: (i, 0)),
                  pl.BlockSpec(memory_space=pl.ANY),
                  pl.BlockSpec((1, d), lambda i, ds: (0, 0)), pl.BlockSpec((1, d), lambda i, ds: (0, 0))],
        out_specs=pl.BlockSpec((tm, d), lambda i, ds: (i, 0)),
        scratch_shapes=[pltpu.VMEM((2, TOP_K, tm * d // LANES, LANES), F32), pltpu.SemaphoreType.DMA((2,))],
    )
    return pl.pallas_call(
        functools.partial(_combine_kernel, alpha=alpha),
        out_shape=jax.ShapeDtypeStruct((n, d), F32),
        grid_spec=grid_spec,
        compiler_params=_cparams(("arbitrary",), row_dma=True),
        name="moe_combine_ln",
    )(dest, hp, wts, ybuf, ln_g, ln_b)


def _rope_tables(s):
    inv_freq = ROPE_THETA ** (-2.0 * jnp.arange(HALF_DIM, dtype=F32) / HEAD_DIM)

    def tables(pos):
        ang = pos.astype(F32)[:, None] * inv_freq[None, :]
        return jnp.cos(ang), jnp.sin(ang)

    cos, sin = tables(jnp.arange(s))
    n_rows = s // CMP_STRIDE
    c_end = jnp.arange(n_rows) * CMP_STRIDE + (CMP_BLOCK - 1)
    cos_c, sin_c = tables(c_end)
    full = lambda c: jnp.concatenate([c, c], axis=1)
    signed = lambda sn: jnp.concatenate([-sn, sn], axis=1)
    return cos.T, sin.T, full(cos), signed(sin), full(cos_c), signed(sin_c)


def _overlap_t(s):
    n_rows = s // CMP_STRIDE
    n_cmp = (s - CMP_BLOCK) // CMP_STRIDE + 1
    n_sel = s // SEL_BLOCK
    c_start = np.arange(n_rows) * CMP_STRIDE
    s_start = np.arange(n_sel) * SEL_BLOCK
    ov = ((c_start[None, :] + CMP_BLOCK - 1 >= s_start[:, None])
          & (c_start[None, :] <= s_start[:, None] + SEL_BLOCK - 1)
          & (np.arange(n_rows)[None, :] < n_cmp))
    return jnp.asarray(ov.astype(np.float32), dtype=BF16)


def _dispatch_plan(eid, n_experts, tr):
    n = eid.shape[0]
    m = n * TOP_K
    rows_per = EXPERT_ROWS
    flat_e = eid.reshape(-1)
    onehot = (flat_e[:, None] == jnp.arange(n_experts)[None, :]).astype(I32)
    pos = jnp.take_along_axis(jnp.cumsum(onehot, axis=0), flat_e[:, None], axis=1)[:, 0] - 1
    sizes = jnp.sum(onehot, axis=0)
    padded = (sizes + rows_per - 1) // rows_per * rows_per
    ends = jnp.cumsum(padded)
    dest = (ends - padded)[flat_e] + pos
    n_blk = -(-(m + n_experts * (rows_per - 1)) // rows_per)
    row_src = jnp.zeros((n_blk * rows_per,), I32).at[dest].set(jnp.arange(m, dtype=I32) // TOP_K * tr)
    blk0 = jnp.concatenate([jnp.zeros((1,), I32), ends.astype(I32) // rows_per])
    return dest.astype(I32) * tr, row_src, blk0, n_blk


def kernel(x, w_in, pool_mix, pool_scale, w_pool_proj, w_nsa_proj, cmp_pos_k, cmp_pos_v, cmp_k_w1, cmp_k_w2,
           cmp_v_w1, cmp_v_w2, w_out, ln1_g, ln1_b, router_group_w, router_group_b, router_expert_w,
           router_expert_b, w_gate, w_up, w_down, ln2_g, ln2_b):
    b, s, d = x.shape
    n = b * s
    depth = w_in.shape[0]
    alpha = (2.0 * depth) ** 0.25
    pw = pool_mix.shape[1] * pool_mix.shape[2]
    qw = w_nsa_proj.shape[1]
    kvw = N_KV_GROUPS * HEAD_DIM
    n_groups, _, per_group = router_expert_w.shape[1:]
    n_experts = n_groups * per_group
    gate_w = 3 * (qw // HEAD_DIM)
    assert s % ATTN_TILE == 0 and WINDOW % ATTN_TILE == 0 and ATTN_TILE % SEL_BLOCK == 0
    assert gate_w <= LANES and n_groups + n_experts <= LANES and s // SEL_BLOCK <= HEAD_DIM
    assert d % (LANES * SUBLANES) == 0

    c_q = pw
    c_kv = pw + qw
    c_gate = c_kv + 6 * kvw
    c_merge = c_gate + gate_w
    tn = 768
    width_a = -(-(c_gate + LANES) // tn) * tn
    assert c_gate % LANES == 0

    cos_t, sin_t, cos_f, sin_s, cos_c, sin_c = _rope_tables(s)
    ov_t = _overlap_t(s)
    assert c_kv % kvw == 0

    h = x.reshape(n, d)
    for l in range(depth):
        wl = w_in[l]
        w_a = jnp.pad(wl[:, :c_merge], ((0, 0), (0, width_a - c_merge))).astype(BF16)
        w_gp = wl[:, c_merge:c_merge + d].astype(BF16)
        w_ga = wl[:, c_merge + d:].astype(BF16)

        proj = _project(h, w_a, _tile(n, 1024), tn)

        mixed = _pool_mixer(proj, pool_mix[l].astype(BF16), pool_scale[l].reshape(1, pw), b, s, pw, _tile(s, 512))

        kv_blk = c_kv // kvw
        kcmp, vcmp_t = _compress(
            proj, kv_blk, b, s,
            cmp_pos_k[l].reshape(1, CMP_BLOCK * HEAD_DIM), cmp_pos_v[l].reshape(1, CMP_BLOCK * HEAD_DIM),
            cmp_k_w1[l].astype(BF16), cmp_k_w2[l].astype(BF16), cmp_v_w1[l].astype(BF16),
            cmp_v_w2[l].T.astype(BF16), cos_c, sin_c)
        ksx, vs_t, kwx, vw_t = _kv_prep(proj, kv_blk, b, s, cos_f, sin_s)

        attn = _attention(proj, cos_t, sin_t, kcmp, vcmp_t, ksx, vs_t, kwx, vw_t, ov_t,
                          b=b, s=s, pw=pw, qw=qw, gate_col=c_gate)

        h_packed = _merge_out(h, mixed, attn, w_pool_proj[l].astype(BF16), w_nsa_proj[l].astype(BF16), w_gp, w_ga,
                              w_out[l].astype(BF16), ln1_g[l].reshape(1, d), ln1_b[l].reshape(1, d), alpha,
                              _tile(n, 512), _tile(d, 512))

        w_r = jnp.concatenate([router_group_w[l], router_expert_w[l].transpose(1, 0, 2).reshape(d, n_experts)], axis=1)
        w_r = jnp.pad(w_r, ((0, 0), (0, LANES - w_r.shape[1])))
        w_r_hi = w_r.astype(BF16)
        w_r_lo = (w_r - w_r_hi.astype(F32)).astype(BF16)
        b_r = jnp.pad(jnp.concatenate([router_group_b[l], router_expert_b[l].reshape(-1)]),
                      (0, LANES - n_groups - n_experts)).reshape(1, LANES)
        eid_l, wts_l = _router(h_packed, w_r_hi, w_r_lo, b_r, n_groups, per_group, _tile(n, 512))
        dest, row_tok, blk0, n_blk = _dispatch_plan(eid_l[:, :TOP_K], n_experts, d // LANES)

        ybuf = _experts(blk0, row_tok, h_packed, w_gate[l], w_up[l], w_down[l], n_blk)
        h = _combine(dest, h_packed, wts_l, ybuf, ln2_g[l].reshape(1, d), ln2_b[l].reshape(1, d), alpha, _tile(n, 256))
    return h.reshape(b, s, d)
```

```python
import functools
import math

import numpy as np
import jax
import jax.numpy as jnp
from jax import lax
from jax.experimental import pallas as pl
from jax.experimental.pallas import tpu as pltpu

F32 = jnp.float32
BF16 = jnp.bfloat16
I32 = jnp.int32

POOL_WINDOWS = (2, 4, 8, 16)
POOL_GROUPS = 4
POOL_HALO = 16
HEAD_DIM = 64
HALF_DIM = HEAD_DIM // 2
V_ROWS = HEAD_DIM + 16
N_KV_GROUPS = 4
CMP_BLOCK = 32
CMP_STRIDE = 16
SEL_BLOCK = 64
SEL_TOP_N = 16
WINDOW = 512
ROPE_THETA = 10000.0
FORCE_SCORE = 1e6
TOP_K = 2
LN_EPS = 1e-5
NEG = -1e30

LANES = 128
SUBLANES = 8
VMEM_BYTES_V7X = 64 * 1024 * 1024
VMEM_LIMIT = VMEM_BYTES_V7X - 8 * 1024 * 1024

ATTN_TILE = 256
EXPERT_ROWS = 256


def _cparams(sem, vmem=VMEM_LIMIT, row_dma=False):
    return pltpu.CompilerParams(dimension_semantics=sem, vmem_limit_bytes=vmem, disable_bounds_checks=row_dma)


def _tile(n, pref):
    t = min(n, pref)
    while n % t:
        t //= 2
    return t


def _proj_kernel(x_ref, w_ref, o_ref, xb_ref):
    @pl.when(pl.program_id(1) == 0)
    def _():
        xb_ref[...] = x_ref[...].astype(BF16)

    o_ref[...] = jnp.dot(xb_ref[...], w_ref[...], preferred_element_type=F32)


def _project(x, w, tm, tn):
    n, d = x.shape
    cols = w.shape[1]
    return pl.pallas_call(
        _proj_kernel,
        out_shape=jax.ShapeDtypeStruct((n, cols), F32),
        grid=(n // tm, cols // tn),
        in_specs=[pl.BlockSpec((tm, d), lambda i, j: (i, 0)),
                  pl.BlockSpec((d, tn), lambda i, j: (0, j))],
        out_specs=pl.BlockSpec((tm, tn), lambda i, j: (i, j)),
        scratch_shapes=[pltpu.VMEM((tm, d), BF16)],
        compiler_params=_cparams(("arbitrary", "arbitrary")),
        name="in_proj",
    )(x, w)


def _pool_kernel(u_ref, halo_ref, mix_ref, scale_ref, o_ref, ext_ref):
    i = pl.program_id(1)
    ts = u_ref.shape[0]
    gd = mix_ref.shape[1]
    ext_ref[POOL_HALO:, :] = u_ref[...]
    ext_ref[:POOL_HALO, :] = jnp.where(i == 0, 0.0, halo_ref[...])
    t = i * ts + lax.broadcasted_iota(I32, (ts, gd), 0)
    for g, w in enumerate(POOL_WINDOWS):
        cols = slice(g * gd, (g + 1) * gd)
        s = ext_ref[:, cols]
        k = 1
        while k < w:
            s = s + pltpu.roll(s, k, axis=0)
            k *= 2
        cnt = jnp.minimum(t + 1, w).astype(F32)
        pooled = s[POOL_HALO:, :] / cnt - u_ref[:, cols]
        mixed = jnp.dot(pooled.astype(BF16), mix_ref[g], preferred_element_type=F32)
        o_ref[:, cols] = (mixed * scale_ref[:, cols]).astype(BF16)


def _pool_mixer(proj, pool_mix_b, pool_scale, b, s, pw, ts):
    ns = s // ts
    hb = ts // POOL_HALO
    return pl.pallas_call(
        _pool_kernel,
        out_shape=jax.ShapeDtypeStruct((b * s, pw), BF16),
        grid=(b, ns),
        in_specs=[pl.BlockSpec((ts, pw), lambda bi, i: (bi * ns + i, 0)),
                  pl.BlockSpec((POOL_HALO, pw), lambda bi, i: (jnp.maximum((bi * ns + i) * hb - 1, 0), 0)),
                  pl.BlockSpec(pool_mix_b.shape, lambda bi, i: (0, 0, 0)),
                  pl.BlockSpec((1, pw), lambda bi, i: (0, 0))],
        out_specs=pl.BlockSpec((ts, pw), lambda bi, i: (bi * ns + i, 0)),
        scratch_shapes=[pltpu.VMEM((POOL_HALO + ts, pw), F32)],
        compiler_params=_cparams(("arbitrary", "arbitrary")),
        name="pool_mixer",
    )(proj, proj, pool_mix_b, pool_scale)


def _gelu_tanh(x):
    return 0.5 * x * (1.0 + jnp.tanh(math.sqrt(2.0 / math.pi) * (x + 0.044715 * (x * x * x))))


def _swap_halves(x):
    return jnp.concatenate([x[:, HALF_DIM:], x[:, :HALF_DIM]], axis=1)


def _cmp_kernel(kc0_ref, kc1_ref, vc0_ref, vc1_ref, pk_ref, pv_ref, kw1_ref, kw2_ref, vw1_ref, vw2t_ref,
                cos_ref, sin_ref, ko_ref, vo_ref, r_s):
    kc_refs = (kc0_ref, kc1_ref)
    vc_refs = (vc0_ref, vc1_ref)
    s = kc0_ref.shape[0]
    nrow = s // CMP_STRIDE
    half = CMP_STRIDE * HEAD_DIM

    def hidden(p_ref, w1_ref, g):
        r = r_s[g]
        r_next = pltpu.roll(r, nrow - 1, axis=0)
        a = (r + p_ref[:, :half]).astype(BF16)
        bb = (r_next + p_ref[:, half:]).astype(BF16)
        h = (jnp.dot(a, w1_ref[:half, :], preferred_element_type=F32)
             + jnp.dot(bb, w1_ref[half:, :], preferred_element_type=F32))
        return _gelu_tanh(h).astype(BF16)

    gpl = LANES // HEAD_DIM

    def regroup(src_refs):
        for t in range(CMP_STRIDE):
            for j, src_ref in enumerate(src_refs):
                rows = src_ref[pl.ds(t, nrow, stride=CMP_STRIDE), :]
                for gg in range(gpl):
                    r_s[j * gpl + gg, :, t * HEAD_DIM:(t + 1) * HEAD_DIM] = rows[:, gg * HEAD_DIM:(gg + 1) * HEAD_DIM]

    regroup(kc_refs)
    for g in range(N_KV_GROUPS):
        k = jnp.dot(hidden(pk_ref, kw1_ref, g), kw2_ref[...], preferred_element_type=F32)
        k = k * cos_ref[...] + _swap_halves(k) * sin_ref[...]
        ko_ref[0, g] = k.astype(BF16)
    regroup(vc_refs)
    for g in range(N_KV_GROUPS):
        hv = hidden(pv_ref, vw1_ref, g)
        vt = lax.dot_general(vw2t_ref[...], hv, (((1,), (1,)), ((), ())), preferred_element_type=F32)
        vo_ref[0, g] = vt.astype(BF16)


def _compress(proj, col_blk, b, s, pk, pv, kw1, kw2, vw1, vw2t, cos_c, sin_c):
    g = N_KV_GROUPS
    kvw = g * HEAD_DIM
    nrow = s // CMP_STRIDE

    def whole(a):
        return pl.BlockSpec(a.shape, lambda bi: (0,) * a.ndim)

    assert kvw == 2 * LANES

    def lane_tile(j):
        return pl.BlockSpec((s, LANES), lambda bi: (bi, col_blk * (kvw // LANES) + j))

    return pl.pallas_call(
        _cmp_kernel,
        out_shape=(jax.ShapeDtypeStruct((b, g, nrow, HEAD_DIM), BF16),
                   jax.ShapeDtypeStruct((b, g, HEAD_DIM, nrow), BF16)),
        grid=(b,),
        in_specs=[lane_tile(0), lane_tile(1), lane_tile(2), lane_tile(3),
                  whole(pk), whole(pv), whole(kw1), whole(kw2), whole(vw1), whole(vw2t),
                  whole(cos_c), whole(sin_c)],
        out_specs=(pl.BlockSpec((1, g, nrow, HEAD_DIM), lambda bi: (bi, 0, 0, 0)),
                   pl.BlockSpec((1, g, HEAD_DIM, nrow), lambda bi: (bi, 0, 0, 0))),
        scratch_shapes=[pltpu.VMEM((g, nrow, CMP_STRIDE * HEAD_DIM), F32)],
        compiler_params=_cparams(("arbitrary",)),
        name="compress_kv",
    )(proj, proj, proj, proj, pk, pv, kw1, kw2, vw1, vw2t, cos_c, sin_c)


def _kvprep_kernel(ks_ref, vs_ref, kw_ref, vw_ref, cos_ref, sin_ref, kso_ref, vso_ref, kwo_ref, vwo_ref):
    i = pl.program_id(1)
    tq = ks_ref.shape[0]
    cos = cos_ref[...]
    sin = sin_ref[...]
    blk = (i * tq + lax.broadcasted_iota(I32, (tq, HEAD_DIM), 0)) // SEL_BLOCK
    onehot = jnp.where(blk == lax.broadcasted_iota(I32, (tq, HEAD_DIM), 1), 1.0, 0.0).astype(BF16)
    ks = ks_ref[...]
    kw = kw_ref[...]
    vs_t = vs_ref[...].T
    vw_t = vw_ref[...].T
    for g in range(N_KV_GROUPS):
        cols = slice(g * HEAD_DIM, (g + 1) * HEAD_DIM)
        k = ks[:, cols]
        kso_ref[0, g, 0, :, :HEAD_DIM] = (k * cos + _swap_halves(k) * sin).astype(BF16)
        kso_ref[0, g, 0, :, HEAD_DIM:] = onehot
        k = kw[:, cols]
        kwo_ref[0, g, 0] = (k * cos + _swap_halves(k) * sin).astype(BF16)
        ones = jnp.ones((V_ROWS - HEAD_DIM, tq), BF16)
        vso_ref[0, g, 0, :HEAD_DIM] = vs_t[cols].astype(BF16)
        vso_ref[0, g, 0, HEAD_DIM:] = ones
        vwo_ref[0, g, 0, :HEAD_DIM] = vw_t[cols].astype(BF16)
        vwo_ref[0, g, 0, HEAD_DIM:] = ones


def _kv_prep(proj, col_blk, b, s, cos_f, sin_s):
    g = N_KV_GROUPS
    kvw = g * HEAD_DIM
    tq = ATTN_TILE
    nq = s // tq

    def col(j):
        return pl.BlockSpec((tq, kvw), lambda bi, i: (bi * nq + i, col_blk + j))

    tab = pl.BlockSpec((tq, HEAD_DIM), lambda bi, i: (i, 0))

    def out(r, c):
        return (jax.ShapeDtypeStruct((b, g, nq, r, c), BF16),
                pl.BlockSpec((1, g, 1, r, c), lambda bi, i: (bi, 0, i, 0, 0)))

    outs = [out(tq, 2 * HEAD_DIM), out(V_ROWS, tq), out(tq, HEAD_DIM), out(V_ROWS, tq)]
    return pl.pallas_call(
        _kvprep_kernel,
        out_shape=tuple(o[0] for o in outs),
        grid=(b, nq),
        in_specs=[col(2), col(3), col(4), col(5), tab, tab],
        out_specs=tuple(o[1] for o in outs),
        compiler_params=_cparams(("arbitrary", "arbitrary")),
        name="kv_prep",
    )(proj, proj, proj, proj, cos_f, sin_s)


def _attn_kernel(q_ref, cos_ref, sin_ref, gate_ref, kc_ref, vct_ref, ks_ref, vst_ref, kw_ref, vwt_ref, ovt_ref,
                 o_ref, qa_s, g_s, imp_s, m_s, c_s, acc_s, out_s, sc_s, pc_s, psum_s, s_s, p_s, *, hg, n_heads):
    g = pl.program_id(1)
    i = pl.program_id(2)
    tq = q_ref.shape[0]
    tk = tq
    ncp = kc_ref.shape[2]
    nsel = ovt_ref.shape[0]
    scale = HEAD_DIM ** -0.5 * math.log2(math.e)

    qt = q_ref[...].T
    cos = cos_ref[...]
    sin = sin_ref[...]
    for h in range(hg):
        x1 = qt[h * HEAD_DIM:h * HEAD_DIM + HALF_DIM]
        x2 = qt[h * HEAD_DIM + HALF_DIM:(h + 1) * HEAD_DIM]
        qa_s[h, :HALF_DIM] = ((x1 * cos - x2 * sin) * scale).astype(BF16)
        qa_s[h, HALF_DIM:HEAD_DIM] = ((x2 * cos + x1 * sin) * scale).astype(BF16)
    g_s[...] = gate_ref[...].T

    def gate(branch, h):
        row = g_s[pl.ds(branch * n_heads + g * hg + h, 1), :]
        return 1.0 / (1.0 + jnp.exp(-row))

    chunks = [slice(c * LANES, (c + 1) * LANES) for c in range(tq // LANES)]

    def qk(dst, kx, qrows):
        for h in range(hg):
            dst[h] = jnp.dot(kx, qa_s[h, :qrows], preferred_element_type=F32)

    qk(sc_s, kc_ref[0, 0], HEAD_DIM)
    c_end = lax.broadcasted_iota(I32, (ncp, LANES), 0) * CMP_STRIDE + (CMP_BLOCK - 1)
    for c, cols in enumerate(chunks):
        cmask = c_end <= i * tq + c * LANES + lax.broadcasted_iota(I32, (ncp, LANES), 1)
        for h in range(hg):
            sm = jnp.where(cmask, sc_s[h, :, cols], NEG)
            m = jnp.max(sm, axis=0, keepdims=True)
            p = jnp.where(cmask, jnp.exp2(sm - m), 0.0)
            den = jnp.maximum(jnp.sum(p, axis=0, keepdims=True), 1e-30)
            pn = p / den
            psum_s[:, cols] = pn if h == 0 else psum_s[:, cols] + pn
            pc_s[h, :, cols] = pn.astype(BF16)
    for h in range(hg):
        oc = jnp.dot(vct_ref[0, 0], pc_s[h], preferred_element_type=F32)
        out_s[h * HEAD_DIM:(h + 1) * HEAD_DIM, :] = oc * gate(0, h)

    psum = psum_s[...]
    p_hi = psum.astype(BF16)
    p_lo = (psum - p_hi.astype(F32)).astype(BF16)
    imp = (jnp.dot(ovt_ref[...], p_hi, preferred_element_type=F32)
           + jnp.dot(ovt_ref[...], p_lo, preferred_element_type=F32))
    jb = lax.broadcasted_iota(I32, (nsel, tq), 0)
    cur = (i * tq + lax.broadcasted_iota(I32, (nsel, tq), 1)) // SEL_BLOCK
    forced = (jb == 0) | (jb == cur) | (jb == cur - 1)
    imp = jnp.where(forced, FORCE_SCORE, jnp.where(jb > cur, -FORCE_SCORE, imp))
    imp_s[...] = imp
    nrb = nsel // SUBLANES
    blocks = [imp[r * SUBLANES:(r + 1) * SUBLANES] for r in range(nrb)]
    counts = [jnp.zeros((SUBLANES, tq), F32) for _ in range(nrb)]
    sub = lax.broadcasted_iota(I32, (SUBLANES, tq), 0)
    for jp in range(nsel):
        row = jnp.broadcast_to(imp_s[jp:jp + 1, :], (SUBLANES, tq))
        for r in range(nrb):
            if r * SUBLANES > jp:
                inc = jnp.where(row >= blocks[r], 1.0, 0.0)
            elif r * SUBLANES + SUBLANES - 1 <= jp:
                inc = jnp.where(row > blocks[r], 1.0, 0.0)
            else:
                tie = jnp.where(sub > jp - r * SUBLANES, 1.0, 0.0)
                inc = jnp.where(row > blocks[r], 1.0, jnp.where(row >= blocks[r], tie, 0.0))
            counts[r] = counts[r] + inc
    top_n = min(SEL_TOP_N, nsel)
    bias = jnp.concatenate([jnp.where(cnt < top_n, 0.0, NEG) for cnt in counts], axis=0).astype(BF16)
    for h in range(hg):
        qa_s[h, HEAD_DIM:HEAD_DIM + nsel] = bias
        if nsel < HEAD_DIM:
            qa_s[h, HEAD_DIM + nsel:] = jnp.zeros((HEAD_DIM - nsel, tq), BF16)

    def reset():
        m_s[...] = jnp.full(m_s.shape, NEG, F32)
        acc_s[...] = jnp.zeros(acc_s.shape, F32)

    krow = lax.broadcasted_iota(I32, (tk, LANES), 0)
    lane = lax.broadcasted_iota(I32, (tk, LANES), 1)

    def softmax_pv(src, vt, kind, skip=None):
        off = None if skip is None else jnp.where(skip, tk, 0)
        for c, cols in enumerate(chunks):
            qcol = lane + c * LANES

            if kind == 'causal':
                mask = krow <= qcol
            elif kind == 'older':
                mask = krow > (qcol if off is None else qcol + off)
            else:
                mask = None if off is None else krow >= off
            for h in range(hg):
                s = src[h, :, cols]
                if mask is not None:
                    s = jnp.where(mask, s, NEG)
                m_old = m_s[h, :, cols]
                m_new = jnp.maximum(m_old, jnp.max(s, axis=0, keepdims=True))
                m_s[h, :, cols] = m_new
                c_s[h, :, cols] = jnp.exp2(m_old - m_new)
                p_s[h, :, cols] = jnp.exp2(s - m_new).astype(BF16)
        for h in range(hg):
            acc_s[h] = acc_s[h] * c_s[h] + jnp.dot(vt, p_s[h], preferred_element_type=F32)

    def finish(branch):
        for h in range(hg):
            rows = slice(h * HEAD_DIM, (h + 1) * HEAD_DIM)
            den = acc_s[h, HEAD_DIM:HEAD_DIM + 1, :]
            out_s[rows, :] = out_s[rows, :] + acc_s[h, :HEAD_DIM, :] * (gate(branch, h) / den)

    s_a, s_b, s_c = s_s.at[0], s_s.at[1], s_s.at[2]
    sel_q = 2 * HEAD_DIM

    reset()
    qk(s_a, ks_ref[0, 0, 0], sel_q)

    def sel_pair(pi, carry):
        kt = 2 * pi
        qk(s_b, ks_ref[0, 0, kt + 1], sel_q)
        softmax_pv(s_a, vst_ref[0, 0, kt], None)
        qk(s_a, ks_ref[0, 0, kt + 2], sel_q)
        softmax_pv(s_b, vst_ref[0, 0, kt + 1], None)
        return carry

    lax.fori_loop(0, i // 2, sel_pair, 0)

    @pl.when(i % 2 == 0)
    def _():
        softmax_pv(s_a, vst_ref[0, 0, i], 'causal')

    @pl.when(i % 2 == 1)
    def _():
        qk(s_b, ks_ref[0, 0, i], sel_q)
        softmax_pv(s_a, vst_ref[0, 0, i - 1], None)
        softmax_pv(s_b, vst_ref[0, 0, i], 'causal')

    finish(1)

    reset()
    nwt = WINDOW // tk
    assert nwt == 2
    t_old = jnp.maximum(i - 2, 0)
    t_mid = jnp.maximum(i - 1, 0)
    qk(s_a, kw_ref[0, 0, t_old], HEAD_DIM)
    qk(s_b, kw_ref[0, 0, t_mid], HEAD_DIM)
    qk(s_c, kw_ref[0, 0, i], HEAD_DIM)
    softmax_pv(s_a, vwt_ref[0, 0, t_old], 'older', skip=i < 2)
    softmax_pv(s_b, vwt_ref[0, 0, t_mid], None, skip=i < 1)
    softmax_pv(s_c, vwt_ref[0, 0, i], 'causal')
    finish(2)

    o_ref[...] = out_s[...].T.astype(o_ref.dtype)


def _attention(proj, cos_t, sin_t, kcmp, vcmp_t, ksx, vs_t, kwx, vw_t, ov_t, *, b, s, pw, qw, gate_col):
    tq = ATTN_TILE
    nq = s // tq
    g = N_KV_GROUPS
    gw = qw // g
    hg = gw // HEAD_DIM
    n_heads = qw // HEAD_DIM
    ncp = kcmp.shape[2]
    nsel = ov_t.shape[0]
    q_blk0 = pw // gw
    kern = functools.partial(_attn_kernel, hg=hg, n_heads=n_heads)
    kv5 = lambda bi, gi, i: (bi, gi, 0, 0, 0)
    kv4 = lambda bi, gi, i: (bi, gi, 0, 0)
    return pl.pallas_call(
        kern,
        out_shape=jax.ShapeDtypeStruct((b * s, qw), BF16),
        grid=(b, g, nq),
        in_specs=[pl.BlockSpec((tq, gw), lambda bi, gi, i: (bi * nq + i, q_blk0 + gi)),
                  pl.BlockSpec((HALF_DIM, tq), lambda bi, gi, i: (0, i)),
                  pl.BlockSpec((HALF_DIM, tq), lambda bi, gi, i: (0, i)),
                  pl.BlockSpec((tq, LANES), lambda bi, gi, i: (bi * nq + i, gate_col // LANES)),
                  pl.BlockSpec((1, 1, ncp, HEAD_DIM), kv4),
                  pl.BlockSpec((1, 1, HEAD_DIM, ncp), kv4),
                  pl.BlockSpec((1, 1, nq, tq, 2 * HEAD_DIM), kv5),
                  pl.BlockSpec((1, 1, nq, V_ROWS, tq), kv5),
                  pl.BlockSpec((1, 1, nq, tq, HEAD_DIM), kv5),
                  pl.BlockSpec((1, 1, nq, V_ROWS, tq), kv5),
                  pl.BlockSpec(ov_t.shape, lambda bi, gi, i: (0, 0))],
        out_specs=pl.BlockSpec((tq, gw), lambda bi, gi, i: (bi * nq + i, gi)),
        scratch_shapes=[pltpu.VMEM((hg, 2 * HEAD_DIM, tq), BF16),
                        pltpu.VMEM((LANES, tq), F32),
                        pltpu.VMEM((nsel, tq), F32),
                        pltpu.VMEM((hg, 1, tq), F32),
                        pltpu.VMEM((hg, 1, tq), F32),
                        pltpu.VMEM((hg, V_ROWS, tq), F32),
                        pltpu.VMEM((gw, tq), F32),
                        pltpu.VMEM((hg, ncp, tq), F32),
                        pltpu.VMEM((hg, ncp, tq), BF16),
                        pltpu.VMEM((ncp, tq), F32),
                        pltpu.VMEM((3, hg, tq, tq), F32),
                        pltpu.VMEM((hg, tq, tq), BF16)],
        compiler_params=_cparams(("arbitrary", "arbitrary", "arbitrary")),
        name="nsa_attention",
    )(proj, cos_t, sin_t, proj, kcmp, vcmp_t, ksx, vs_t, kwx, vw_t, ov_t)


def _layer_norm(r, g, b):
    mu = jnp.mean(r, axis=-1, keepdims=True)
    d = r - mu
    var = jnp.mean(d * d, axis=-1, keepdims=True)
    return d * lax.rsqrt(var + LN_EPS) * g + b


def _sigmoid(x):
    return 1.0 / (1.0 + jnp.exp(-x))


def _merge_kernel(x_ref, pm_ref, at_ref, wpp_ref, wnp_ref, wgp_ref, wga_ref, wo_ref, g_ref, b_ref, o_ref,
                  xb_ref, acc_ref, *, alpha):
    c = pl.program_id(1)

    @pl.when(c == 0)
    def _():
        xb_ref[...] = x_ref[...].astype(BF16)
        acc_ref[...] = jnp.zeros(acc_ref.shape, F32)

    xb = xb_ref[...]
    y_pool = jnp.dot(pm_ref[...], wpp_ref[...], preferred_element_type=F32)
    y_attn = jnp.dot(at_ref[...], wnp_ref[...], preferred_element_type=F32)
    g_pool = _sigmoid(jnp.dot(xb, wgp_ref[...], preferred_element_type=F32))
    g_attn = _sigmoid(jnp.dot(xb, wga_ref[...], preferred_element_type=F32))
    z = g_pool * y_pool + g_attn * y_attn
    acc_ref[...] += jnp.dot(z.astype(BF16), wo_ref[...], preferred_element_type=F32)

    @pl.when(c == pl.num_programs(1) - 1)
    def _():
        o_ref[...] = _layer_norm(alpha * x_ref[...] + acc_ref[...], g_ref[...], b_ref[...])


def _merge_out(x, pm, at, wpp, wnp, wgp, wga, wo, ln_g, ln_b, alpha, tm, ck):
    n, d = x.shape
    pw = pm.shape[1]
    qw = at.shape[1]
    row = lambda i, c: (i, 0)
    colc = lambda i, c: (0, c)
    return pl.pallas_call(
        functools.partial(_merge_kernel, alpha=alpha),
        out_shape=jax.ShapeDtypeStruct((n, d), F32),
        grid=(n // tm, d // ck),
        in_specs=[pl.BlockSpec((tm, d), row), pl.BlockSpec((tm, pw), row), pl.BlockSpec((tm, qw), row),
                  pl.BlockSpec((pw, ck), colc), pl.BlockSpec((qw, ck), colc),
                  pl.BlockSpec((d, ck), colc), pl.BlockSpec((d, ck), colc),
                  pl.BlockSpec((ck, d), lambda i, c: (c, 0)),
                  pl.BlockSpec((1, d), lambda i, c: (0, 0)), pl.BlockSpec((1, d), lambda i, c: (0, 0))],
        out_specs=pl.BlockSpec((tm, d), row),
        scratch_shapes=[pltpu.VMEM((tm, d), BF16), pltpu.VMEM((tm, d), F32)],
        compiler_params=_cparams(("arbitrary", "arbitrary")),
        name="merge_out_ln",
    )(x, pm, at, wpp, wnp, wgp, wga, wo, ln_g, ln_b)


def _router_kernel(h_ref, whi_ref, wlo_ref, b_ref, e_ref, w_ref, *, n_groups, per_group):
    h = h_ref[...]
    h_hi = h.astype(BF16)
    h_lo = (h - h_hi.astype(F32)).astype(BF16)
    logits = (jnp.dot(h_hi, whi_ref[...], preferred_element_type=F32)
              + jnp.dot(h_hi, wlo_ref[...], preferred_element_type=F32)
              + jnp.dot(h_lo, whi_ref[...], preferred_element_type=F32)) + b_ref[...]
    lane = lax.broadcasted_iota(I32, logits.shape, 1)
    far = LANES

    def first_argmax(v, vmax):
        return jnp.min(jnp.where(v == vmax, lane, far), axis=-1, keepdims=True)

    gl = jnp.where(lane < n_groups, logits, NEG)
    gmax = jnp.max(gl, axis=-1, keepdims=True)
    grp = first_argmax(gl, gmax)
    gsum = jnp.sum(jnp.where(lane < n_groups, jnp.exp(gl - gmax), 0.0), axis=-1, keepdims=True)
    g_gate = 1.0 / gsum
    lo = n_groups + grp * per_group
    el = jnp.where((lane >= lo) & (lane < lo + per_group), logits, NEG)
    v1 = jnp.max(el, axis=-1, keepdims=True)
    i1 = first_argmax(el, v1)
    el2 = jnp.where(lane == i1, NEG, el)
    v2 = jnp.max(el2, axis=-1, keepdims=True)
    i2 = first_argmax(el2, v2)
    e21 = jnp.exp(v2 - v1)
    w1 = g_gate / (1.0 + e21)
    w2 = g_gate * e21 / (1.0 + e21)
    e_ref[...] = jnp.where(lane == 0, i1 - n_groups, jnp.where(lane == 1, i2 - n_groups, 0))
    w_ref[...] = jnp.where(lane == 0, w1, jnp.where(lane == 1, w2, 0.0))


def _router(h, w_hi, w_lo, bias, n_groups, per_group, tm):
    n, d = h.shape
    return pl.pallas_call(
        functools.partial(_router_kernel, n_groups=n_groups, per_group=per_group),
        out_shape=(jax.ShapeDtypeStruct((n, LANES), I32), jax.ShapeDtypeStruct((n, LANES), F32)),
        grid=(n // tm,),
        in_specs=[pl.BlockSpec((tm, d), lambda i: (i, 0)),
                  pl.BlockSpec((d, LANES), lambda i: (0, 0)), pl.BlockSpec((d, LANES), lambda i: (0, 0)),
                  pl.BlockSpec((1, LANES), lambda i: (0, 0))],
        out_specs=(pl.BlockSpec((tm, LANES), lambda i: (i, 0)), pl.BlockSpec((tm, LANES), lambda i: (i, 0))),
        compiler_params=_cparams(("arbitrary",)),
        name="moe_router",
    )(h, w_hi, w_lo, bias)


def _expert_kernel(blk0_ref, row_tok_ref, h_hbm, wg_ref, wu_ref, wd_ref, y_hbm,
                   xbuf, ybuf, wgb, wub, wdb, gsem, ysem):
    e = pl.program_id(0)
    n_exp = pl.num_programs(0)
    rows = ybuf.shape[1]
    b0 = blk0_ref[e]
    nb = blk0_ref[e + 1] - b0
    n_used = blk0_ref[n_exp]

    def row_copy(blk, slot, t, u):
        tok = row_tok_ref[blk * rows + t * SUBLANES + u]
        src = h_hbm.at[lax.shift_right_logical(tok, 3), pl.ds(tok & (SUBLANES - 1), 1)]
        return pltpu.make_async_copy(src, xbuf.at[slot, t, pl.ds(u, 1)], gsem.at[slot])

    def start_gather(blk, slot):
        def body(t, c):
            for u in range(SUBLANES):
                row_copy(blk, slot, t, u).start(priority=u % 2)
            return c
        lax.fori_loop(0, rows // SUBLANES, body, 0)

    def wait_gather(blk, slot):
        def body(t, c):
            for u in range(SUBLANES):
                row_copy(blk, slot, t, u).wait()
            return c
        lax.fori_loop(0, rows // SUBLANES, body, 0)

    def y_copy(blk, slot):
        return pltpu.make_async_copy(ybuf.at[slot], y_hbm.at[pl.ds(blk * rows, rows)], ysem.at[slot])

    @pl.when((e == 0) & (n_used > 0))
    def _():
        start_gather(0, 0)

    @pl.when(nb > 0)
    def _():
        wgb[...] = wg_ref[0].astype(BF16)
        wub[...] = wu_ref[0].astype(BF16)
        wdb[...] = wd_ref[0].astype(BF16)

    def block(j, carry):
        b = b0 + j
        slot = b % 2

        wait_gather(b, slot)
        x = xbuf[slot].reshape(rows, xbuf.shape[3]).astype(BF16)
        hgate = jnp.dot(x, wgb[...], preferred_element_type=F32)
        hup = jnp.dot(x, wub[...], preferred_element_type=F32)
        nxt = jnp.minimum(b + 1, n_used - 1)
        for t in range(rows // SUBLANES):
            for u in range(SUBLANES):
                row_copy(nxt, 1 - slot, t, u).start()
        a = (hgate * _sigmoid(hgate) * hup).astype(BF16)
        y = jnp.dot(a, wdb[...], preferred_element_type=F32)

        @pl.when(b >= 2)
        def _():
            y_copy(b - 2, slot).wait()

        ybuf[slot] = y
        y_copy(b, slot).start()
        return carry

    lax.fori_loop(0, nb, block, 0)

    @pl.when((e == n_exp - 1) & (n_used > 0))
    def _():
        wait_gather(n_used - 1, n_used % 2)

    @pl.when(e == n_exp - 1)
    def _():
        for back in (2, 1):
            @pl.when(n_used >= back)
            def _(back=back):
                y_copy(n_used - back, (n_used - back) % 2).wait()

        n_blk = y_hbm.shape[0] // rows
        ybuf[0] = jnp.zeros(ybuf.shape[1:], ybuf.dtype)

        def zero_start(blk, c):
            y_copy(blk, 0).start()
            return c

        def zero_wait(blk, c):
            y_copy(blk, 0).wait()
            return c

        lax.fori_loop(n_used, n_blk, zero_start, 0)
        lax.fori_loop(n_used, n_blk, zero_wait, 0)


def _experts(blk0, row_tok, h, w_gate, w_up, w_down, n_blk):
    n_exp = w_gate.shape[0]
    rows = EXPERT_ROWS
    d = h.shape[1]
    hid = w_gate.shape[2]
    grid_spec = pltpu.PrefetchScalarGridSpec(
        num_scalar_prefetch=2,
        grid=(n_exp,),
        in_specs=[pl.BlockSpec(memory_space=pl.ANY),
                  pl.BlockSpec((1, d, hid), lambda e, b0, rt: (e, 0, 0)),
                  pl.BlockSpec((1, d, hid), lambda e, b0, rt: (e, 0, 0)),
                  pl.BlockSpec((1, hid, d), lambda e, b0, rt: (e, 0, 0))],
        out_specs=pl.BlockSpec(memory_space=pl.ANY),
        scratch_shapes=[pltpu.VMEM((2, rows // SUBLANES, SUBLANES, d), F32), pltpu.VMEM((2, rows, d), F32),
                        pltpu.VMEM((d, hid), BF16), pltpu.VMEM((d, hid), BF16), pltpu.VMEM((hid, d), BF16),
                        pltpu.SemaphoreType.DMA((2,)), pltpu.SemaphoreType.DMA((2,))],
    )
    return pl.pallas_call(
        _expert_kernel,
        out_shape=jax.ShapeDtypeStruct((n_blk * rows, d), F32),
        grid_spec=grid_spec,
        compiler_params=_cparams(("arbitrary",), row_dma=True),
        name="moe_experts",
    )(blk0, row_tok, h.reshape(h.shape[0] // SUBLANES, SUBLANES, d), w_gate, w_up, w_down)


def _combine_kernel(dest_ref, h_ref, w_ref, y_hbm, g_ref, b_ref, o_ref, ybuf, sem, *, alpha):
    i = pl.program_id(0)
    n_steps = pl.num_programs(0)
    tm = h_ref.shape[0]
    slot = i % 2

    def row_copy(step_, slot_, t, u, k):
        row = dest_ref[(step_ * tm + t * SUBLANES + u) * TOP_K + k]
        src = y_hbm.at[lax.shift_right_logical(row, 3), pl.ds(row & (SUBLANES - 1), 1)]
        return pltpu.make_async_copy(src, ybuf.at[slot_, k, t, pl.ds(u, 1)], sem.at[slot_])

    def start_gather(step_, slot_):
        def body(t, c):
            for u in range(SUBLANES):
                for k in range(TOP_K):
                    row_copy(step_, slot_, t, u, k).start(priority=k % 2)
            return c
        lax.fori_loop(0, tm // SUBLANES, body, 0)

    def wait_gather(step_, slot_):
        def body(t, c):
            for u in range(SUBLANES):
                for k in range(TOP_K):
                    row_copy(step_, slot_, t, u, k).wait()
            return c
        lax.fori_loop(0, tm // SUBLANES, body, 0)

    @pl.when(i == 0)
    def _():
        start_gather(0, 0)

    wait_gather(i, slot)
    w = w_ref[...]
    y = ybuf[slot, 0].reshape(h_ref.shape) * w[:, 0:1]
    for k in range(1, TOP_K):
        y = y + ybuf[slot, k].reshape(h_ref.shape) * w[:, k:k + 1]
    nxt = jnp.minimum(i + 1, n_steps - 1)
    for t in range(tm // SUBLANES):
        for u in range(SUBLANES):
            for k in range(TOP_K):
                row_copy(nxt, 1 - slot, t, u, k).start()
    o_ref[...] = _layer_norm(alpha * h_ref[...] + y, g_ref[...], b_ref[...])

    @pl.when(i == n_steps - 1)
    def _():
        wait_gather(i, 1 - slot)


def _combine(dest, h, wts, ybuf, ln_g, ln_b, alpha, tm):
    n, d = h.shape
    grid_spec = pltpu.PrefetchScalarGridSpec(
        num_scalar_prefetch=1,
        grid=(n // tm,),
        in_specs=[pl.BlockSpec((tm, d), lambda i, ds: (i, 0)),
                  pl.BlockSpec((tm, LANES), lambda i, ds: (i, 0)),
                  pl.BlockSpec(memory_space=pl.ANY),
                  pl.BlockSpec((1, d), lambda i, ds: (0, 0)), pl.BlockSpec((1, d), lambda i, ds: (0, 0))],
        out_specs=pl.BlockSpec((tm, d), lambda i, ds: (i, 0)),
        scratch_shapes=[pltpu.VMEM((2, TOP_K, tm // SUBLANES, SUBLANES, d), F32), pltpu.SemaphoreType.DMA((2,))],
    )
    return pl.pallas_call(
        functools.partial(_combine_kernel, alpha=alpha),
        out_shape=jax.ShapeDtypeStruct((n, d), F32),
        grid_spec=grid_spec,
        compiler_params=_cparams(("arbitrary",), row_dma=True),
        name="moe_combine_ln",
    )(dest, h, wts, ybuf.reshape(ybuf.shape[0] // SUBLANES, SUBLANES, d), ln_g, ln_b)


def _rope_tables(s):
    inv_freq = ROPE_THETA ** (-2.0 * jnp.arange(HALF_DIM, dtype=F32) / HEAD_DIM)

    def tables(pos):
        ang = pos.astype(F32)[:, None] * inv_freq[None, :]
        return jnp.cos(ang), jnp.sin(ang)

    cos, sin = tables(jnp.arange(s))
    n_rows = s // CMP_STRIDE
    c_end = jnp.arange(n_rows) * CMP_STRIDE + (CMP_BLOCK - 1)
    cos_c, sin_c = tables(c_end)
    full = lambda c: jnp.concatenate([c, c], axis=1)
    signed = lambda sn: jnp.concatenate([-sn, sn], axis=1)
    return cos.T, sin.T, full(cos), signed(sin), full(cos_c), signed(sin_c)


def _overlap_t(s):
    n_rows = s // CMP_STRIDE
    n_cmp = (s - CMP_BLOCK) // CMP_STRIDE + 1
    n_sel = s // SEL_BLOCK
    c_start = np.arange(n_rows) * CMP_STRIDE
    s_start = np.arange(n_sel) * SEL_BLOCK
    ov = ((c_start[None, :] + CMP_BLOCK - 1 >= s_start[:, None])
          & (c_start[None, :] <= s_start[:, None] + SEL_BLOCK - 1)
          & (np.arange(n_rows)[None, :] < n_cmp))
    return jnp.asarray(ov.astype(np.float32), dtype=BF16)


def _dispatch_plan(eid, n_experts):
    n = eid.shape[0]
    m = n * TOP_K
    rows_per = EXPERT_ROWS
    flat_e = eid.reshape(-1)
    onehot = (flat_e[:, None] == jnp.arange(n_experts)[None, :]).astype(I32)
    pos = jnp.take_along_axis(jnp.cumsum(onehot, axis=0), flat_e[:, None], axis=1)[:, 0] - 1
    sizes = jnp.sum(onehot, axis=0)
    padded = (sizes + rows_per - 1) // rows_per * rows_per
    ends = jnp.cumsum(padded)
    dest = (ends - padded)[flat_e] + pos
    n_blk = -(-(m + n_experts * (rows_per - 1)) // rows_per)
    row_tok = jnp.zeros((n_blk * rows_per,), I32).at[dest].set(jnp.arange(m, dtype=I32) // TOP_K)
    blk0 = jnp.concatenate([jnp.zeros((1,), I32), ends.astype(I32) // rows_per])
    return dest.astype(I32), row_tok, blk0, n_blk


def kernel(x, w_in, pool_mix, pool_scale, w_pool_proj, w_nsa_proj, cmp_pos_k, cmp_pos_v, cmp_k_w1, cmp_k_w2,
           cmp_v_w1, cmp_v_w2, w_out, ln1_g, ln1_b, router_group_w, router_group_b, router_expert_w,
           router_expert_b, w_gate, w_up, w_down, ln2_g, ln2_b):
    b, s, d = x.shape
    n = b * s
    depth = w_in.shape[0]
    alpha = (2.0 * depth) ** 0.25
    pw = pool_mix.shape[1] * pool_mix.shape[2]
    qw = w_nsa_proj.shape[1]
    kvw = N_KV_GROUPS * HEAD_DIM
    n_groups, _, per_group = router_expert_w.shape[1:]
    n_experts = n_groups * per_group
    gate_w = 3 * (qw // HEAD_DIM)
    assert s % ATTN_TILE == 0 and WINDOW % ATTN_TILE == 0 and ATTN_TILE % SEL_BLOCK == 0
    assert gate_w <= LANES and n_groups + n_experts <= LANES and s // SEL_BLOCK <= HEAD_DIM
    c_q = pw
    c_kv = pw + qw
    c_gate = c_kv + 6 * kvw
    c_merge = c_gate + gate_w
    tn = 768
    width_a = -(-(c_gate + LANES) // tn) * tn
    assert c_gate % LANES == 0

    cos_t, sin_t, cos_f, sin_s, cos_c, sin_c = _rope_tables(s)
    ov_t = _overlap_t(s)
    assert c_kv % kvw == 0

    h = x.reshape(n, d)
    for l in range(depth):
        wl = w_in[l]
        w_a = jnp.pad(wl[:, :c_merge], ((0, 0), (0, width_a - c_merge))).astype(BF16)
        w_gp = wl[:, c_merge:c_merge + d].astype(BF16)
        w_ga = wl[:, c_merge + d:].astype(BF16)

        proj = _project(h, w_a, _tile(n, 1024), tn)

        mixed = _pool_mixer(proj, pool_mix[l].astype(BF16), pool_scale[l].reshape(1, pw), b, s, pw, _tile(s, 512))

        kv_blk = c_kv // kvw
        kcmp, vcmp_t = _compress(
            proj, kv_blk, b, s,
            cmp_pos_k[l].reshape(1, CMP_BLOCK * HEAD_DIM), cmp_pos_v[l].reshape(1, CMP_BLOCK * HEAD_DIM),
            cmp_k_w1[l].astype(BF16), cmp_k_w2[l].astype(BF16), cmp_v_w1[l].astype(BF16),
            cmp_v_w2[l].T.astype(BF16), cos_c, sin_c)
        ksx, vs_t, kwx, vw_t = _kv_prep(proj, kv_blk, b, s, cos_f, sin_s)

        attn = _attention(proj, cos_t, sin_t, kcmp, vcmp_t, ksx, vs_t, kwx, vw_t, ov_t,
                          b=b, s=s, pw=pw, qw=qw, gate_col=c_gate)

        h = _merge_out(h, mixed, attn, w_pool_proj[l].astype(BF16), w_nsa_proj[l].astype(BF16), w_gp, w_ga,
                       w_out[l].astype(BF16), ln1_g[l].reshape(1, d), ln1_b[l].reshape(1, d), alpha,
                       _tile(n, 512), _tile(d, 512))

        w_r = jnp.concatenate([router_group_w[l], router_expert_w[l].transpose(1, 0, 2).reshape(d, n_experts)], axis=1)
        w_r = jnp.pad(w_r, ((0, 0), (0, LANES - w_r.shape[1])))
        w_r_hi = w_r.astype(BF16)
        w_r_lo = (w_r - w_r_hi.astype(F32)).astype(BF16)
        b_r = jnp.pad(jnp.concatenate([router_group_b[l], router_expert_b[l].reshape(-1)]),
                      (0, LANES - n_groups - n_experts)).reshape(1, LANES)
        eid_l, wts_l = _router(h, w_r_hi, w_r_lo, b_r, n_groups, per_group, _tile(n, 512))
        dest, row_tok, blk0, n_blk = _dispatch_plan(eid_l[:, :TOP_K], n_experts)

        ybuf = _experts(blk0, row_tok, h, w_gate[l], w_up[l], w_down[l], n_blk)
        h = _combine(dest, h, wts_l, ybuf, ln2_g[l].reshape(1, d), ln2_b[l].reshape(1, d), alpha, _tile(n, 256))
    return h.reshape(b, s, d)
```

```python
import functools
import math

import numpy as np
import jax
import jax.numpy as jnp
from jax import lax
from jax.experimental import pallas as pl
from jax.experimental.pallas import tpu as pltpu

F32 = jnp.float32
BF16 = jnp.bfloat16
I32 = jnp.int32

POOL_WINDOWS = (2, 4, 8, 16)
POOL_GROUPS = 4
POOL_HALO = 16
HEAD_DIM = 64
HALF_DIM = HEAD_DIM // 2
V_ROWS = HEAD_DIM + 16
N_KV_GROUPS = 4
CMP_BLOCK = 32
CMP_STRIDE = 16
SEL_BLOCK = 64
SEL_TOP_N = 16
WINDOW = 512
ROPE_THETA = 10000.0
FORCE_SCORE = 1e6
TOP_K = 2
LN_EPS = 1e-5
NEG = -1e30

LANES = 128
SUBLANES = 8
VMEM_BYTES_V7X = 64 * 1024 * 1024
VMEM_LIMIT = VMEM_BYTES_V7X - 8 * 1024 * 1024

ATTN_TILE = 256
SEL_UNROLL = 4
EXPERT_ROWS = 256


def _cparams(sem, vmem=VMEM_LIMIT, row_dma=False):
    return pltpu.CompilerParams(dimension_semantics=sem, vmem_limit_bytes=vmem, disable_bounds_checks=row_dma)


def _tile(n, pref):
    t = min(n, pref)
    while n % t:
        t //= 2
    return t


def _proj_kernel(x_ref, w_ref, o_ref, xb_ref):
    @pl.when(pl.program_id(1) == 0)
    def _():
        xb_ref[...] = x_ref[...].astype(BF16)

    o_ref[...] = jnp.dot(xb_ref[...], w_ref[...], preferred_element_type=F32)


def _project(x, w, tm, tn):
    n, d = x.shape
    cols = w.shape[1]
    return pl.pallas_call(
        _proj_kernel,
        out_shape=jax.ShapeDtypeStruct((n, cols), F32),
        grid=(n // tm, cols // tn),
        in_specs=[pl.BlockSpec((tm, d), lambda i, j: (i, 0)),
                  pl.BlockSpec((d, tn), lambda i, j: (0, j))],
        out_specs=pl.BlockSpec((tm, tn), lambda i, j: (i, j)),
        scratch_shapes=[pltpu.VMEM((tm, d), BF16)],
        compiler_params=_cparams(("arbitrary", "arbitrary")),
        name="in_proj",
    )(x, w)


def _pool_kernel(u_ref, halo_ref, mix_ref, scale_ref, o_ref, ext_ref):
    i = pl.program_id(1)
    ts = u_ref.shape[0]
    gd = mix_ref.shape[1]
    ext_ref[POOL_HALO:, :] = u_ref[...]
    ext_ref[:POOL_HALO, :] = jnp.where(i == 0, 0.0, halo_ref[...])
    t = i * ts + lax.broadcasted_iota(I32, (ts, gd), 0)
    for g, w in enumerate(POOL_WINDOWS):
        cols = slice(g * gd, (g + 1) * gd)
        s = ext_ref[:, cols]
        k = 1
        while k < w:
            s = s + pltpu.roll(s, k, axis=0)
            k *= 2
        cnt = jnp.minimum(t + 1, w).astype(F32)
        pooled = s[POOL_HALO:, :] / cnt - u_ref[:, cols]
        mixed = jnp.dot(pooled.astype(BF16), mix_ref[g], preferred_element_type=F32)
        o_ref[:, cols] = (mixed * scale_ref[:, cols]).astype(BF16)


def _pool_mixer(proj, pool_mix_b, pool_scale, b, s, pw, ts):
    ns = s // ts
    hb = ts // POOL_HALO
    return pl.pallas_call(
        _pool_kernel,
        out_shape=jax.ShapeDtypeStruct((b * s, pw), BF16),
        grid=(b, ns),
        in_specs=[pl.BlockSpec((ts, pw), lambda bi, i: (bi * ns + i, 0)),
                  pl.BlockSpec((POOL_HALO, pw), lambda bi, i: (jnp.maximum((bi * ns + i) * hb - 1, 0), 0)),
                  pl.BlockSpec(pool_mix_b.shape, lambda bi, i: (0, 0, 0)),
                  pl.BlockSpec((1, pw), lambda bi, i: (0, 0))],
        out_specs=pl.BlockSpec((ts, pw), lambda bi, i: (bi * ns + i, 0)),
        scratch_shapes=[pltpu.VMEM((POOL_HALO + ts, pw), F32)],
        compiler_params=_cparams(("arbitrary", "arbitrary")),
        name="pool_mixer",
    )(proj, proj, pool_mix_b, pool_scale)


def _gelu_tanh(x):
    return 0.5 * x * (1.0 + jnp.tanh(math.sqrt(2.0 / math.pi) * (x + 0.044715 * (x * x * x))))


def _swap_halves(x):
    return jnp.concatenate([x[:, HALF_DIM:], x[:, :HALF_DIM]], axis=1)


def _cmp_kernel(kc0_ref, kc1_ref, vc0_ref, vc1_ref, pk_ref, pv_ref, kw1_ref, kw2_ref, vw1_ref, vw2t_ref,
                cos_ref, sin_ref, ko_ref, vo_ref, r_s):
    kc_refs = (kc0_ref, kc1_ref)
    vc_refs = (vc0_ref, vc1_ref)
    s = kc0_ref.shape[0]
    nrow = s // CMP_STRIDE
    half = CMP_STRIDE * HEAD_DIM

    def hidden(p_ref, w1_ref, g):
        r = r_s[g]
        r_next = pltpu.roll(r, nrow - 1, axis=0)
        a = (r + p_ref[:, :half]).astype(BF16)
        bb = (r_next + p_ref[:, half:]).astype(BF16)
        h = (jnp.dot(a, w1_ref[:half, :], preferred_element_type=F32)
             + jnp.dot(bb, w1_ref[half:, :], preferred_element_type=F32))
        return _gelu_tanh(h).astype(BF16)

    gpl = LANES // HEAD_DIM

    def regroup(src_refs):
        for t in range(CMP_STRIDE):
            for j, src_ref in enumerate(src_refs):
                rows = src_ref[pl.ds(t, nrow, stride=CMP_STRIDE), :]
                for gg in range(gpl):
                    r_s[j * gpl + gg, :, t * HEAD_DIM:(t + 1) * HEAD_DIM] = rows[:, gg * HEAD_DIM:(gg + 1) * HEAD_DIM]

    regroup(kc_refs)
    for g in range(N_KV_GROUPS):
        k = jnp.dot(hidden(pk_ref, kw1_ref, g), kw2_ref[...], preferred_element_type=F32)
        k = k * cos_ref[...] + _swap_halves(k) * sin_ref[...]
        ko_ref[0, g] = k.astype(BF16)
    regroup(vc_refs)
    for g in range(N_KV_GROUPS):
        hv = hidden(pv_ref, vw1_ref, g)
        vt = lax.dot_general(vw2t_ref[...], hv, (((1,), (1,)), ((), ())), preferred_element_type=F32)
        vo_ref[0, g] = vt.astype(BF16)


def _compress(proj, col_blk, b, s, pk, pv, kw1, kw2, vw1, vw2t, cos_c, sin_c):
    g = N_KV_GROUPS
    kvw = g * HEAD_DIM
    nrow = s // CMP_STRIDE

    def whole(a):
        return pl.BlockSpec(a.shape, lambda bi: (0,) * a.ndim)

    assert kvw == 2 * LANES

    def lane_tile(j):
        return pl.BlockSpec((s, LANES), lambda bi: (bi, col_blk * (kvw // LANES) + j))

    return pl.pallas_call(
        _cmp_kernel,
        out_shape=(jax.ShapeDtypeStruct((b, g, nrow, HEAD_DIM), BF16),
                   jax.ShapeDtypeStruct((b, g, HEAD_DIM, nrow), BF16)),
        grid=(b,),
        in_specs=[lane_tile(0), lane_tile(1), lane_tile(2), lane_tile(3),
                  whole(pk), whole(pv), whole(kw1), whole(kw2), whole(vw1), whole(vw2t),
                  whole(cos_c), whole(sin_c)],
        out_specs=(pl.BlockSpec((1, g, nrow, HEAD_DIM), lambda bi: (bi, 0, 0, 0)),
                   pl.BlockSpec((1, g, HEAD_DIM, nrow), lambda bi: (bi, 0, 0, 0))),
        scratch_shapes=[pltpu.VMEM((g, nrow, CMP_STRIDE * HEAD_DIM), F32)],
        compiler_params=_cparams(("arbitrary",)),
        name="compress_kv",
    )(proj, proj, proj, proj, pk, pv, kw1, kw2, vw1, vw2t, cos_c, sin_c)


def _kvprep_kernel(ks_ref, vs_ref, kw_ref, vw_ref, cos_ref, sin_ref, kso_ref, vso_ref, kwo_ref, vwo_ref):
    i = pl.program_id(1)
    tq = ks_ref.shape[0]
    cos = cos_ref[...]
    sin = sin_ref[...]
    blk = (i * tq + lax.broadcasted_iota(I32, (tq, HEAD_DIM), 0)) // SEL_BLOCK
    onehot = jnp.where(blk == lax.broadcasted_iota(I32, (tq, HEAD_DIM), 1), 1.0, 0.0).astype(BF16)
    ks = ks_ref[...]
    kw = kw_ref[...]
    vs_t = vs_ref[...].T
    vw_t = vw_ref[...].T
    for g in range(N_KV_GROUPS):
        cols = slice(g * HEAD_DIM, (g + 1) * HEAD_DIM)
        k = ks[:, cols]
        kso_ref[0, g, 0, :, :HEAD_DIM] = (k * cos + _swap_halves(k) * sin).astype(BF16)
        kso_ref[0, g, 0, :, HEAD_DIM:] = onehot
        k = kw[:, cols]
        kwo_ref[0, g, 0] = (k * cos + _swap_halves(k) * sin).astype(BF16)
        ones = jnp.ones((V_ROWS - HEAD_DIM, tq), BF16)
        vso_ref[0, g, 0, :HEAD_DIM] = vs_t[cols].astype(BF16)
        vso_ref[0, g, 0, HEAD_DIM:] = ones
        vwo_ref[0, g, 0, :HEAD_DIM] = vw_t[cols].astype(BF16)
        vwo_ref[0, g, 0, HEAD_DIM:] = ones


def _kv_prep(proj, col_blk, b, s, cos_f, sin_s):
    g = N_KV_GROUPS
    kvw = g * HEAD_DIM
    tq = ATTN_TILE
    nq = s // tq

    def col(j):
        return pl.BlockSpec((tq, kvw), lambda bi, i: (bi * nq + i, col_blk + j))

    tab = pl.BlockSpec((tq, HEAD_DIM), lambda bi, i: (i, 0))

    def out(r, c):
        return (jax.ShapeDtypeStruct((b, g, nq, r, c), BF16),
                pl.BlockSpec((1, g, 1, r, c), lambda bi, i: (bi, 0, i, 0, 0)))

    outs = [out(tq, 2 * HEAD_DIM), out(V_ROWS, tq), out(tq, HEAD_DIM), out(V_ROWS, tq)]
    return pl.pallas_call(
        _kvprep_kernel,
        out_shape=tuple(o[0] for o in outs),
        grid=(b, nq),
        in_specs=[col(2), col(3), col(4), col(5), tab, tab],
        out_specs=tuple(o[1] for o in outs),
        compiler_params=_cparams(("arbitrary", "arbitrary")),
        name="kv_prep",
    )(proj, proj, proj, proj, cos_f, sin_s)


def _attn_kernel(q_ref, cos_ref, sin_ref, gate_ref, kc_ref, vct_ref, ks_ref, vst_ref, kw_ref, vwt_ref, ovt_ref,
                 o_ref, qa_s, g_s, imp_s, m_s, c_s, acc_s, out_s, sc_s, pc_s, psum_s, s_s, p_s,
                 mw_s, cw_s, accw_s, sw_s, pw_s, *, hg, n_heads):
    g = pl.program_id(1)
    i = pl.program_id(2)
    tq = q_ref.shape[0]
    tk = tq
    ncp = kc_ref.shape[2]
    nsel = ovt_ref.shape[0]
    scale = HEAD_DIM ** -0.5 * math.log2(math.e)

    qt = q_ref[...].T
    cos = cos_ref[...]
    sin = sin_ref[...]
    for h in range(hg):
        x1 = qt[h * HEAD_DIM:h * HEAD_DIM + HALF_DIM]
        x2 = qt[h * HEAD_DIM + HALF_DIM:(h + 1) * HEAD_DIM]
        qa_s[h, :HALF_DIM] = ((x1 * cos - x2 * sin) * scale).astype(BF16)
        qa_s[h, HALF_DIM:HEAD_DIM] = ((x2 * cos + x1 * sin) * scale).astype(BF16)
    g_s[...] = gate_ref[...].T

    def gate(branch, h):
        row = g_s[pl.ds(branch * n_heads + g * hg + h, 1), :]
        return 1.0 / (1.0 + jnp.exp(-row))

    chunks = [slice(c * LANES, (c + 1) * LANES) for c in range(tq // LANES)]

    def qk(dst, kx, qrows):
        for h in range(hg):
            dst[h] = jnp.dot(kx, qa_s[h, :qrows], preferred_element_type=F32)

    qk(sc_s, kc_ref[0, 0], HEAD_DIM)
    c_end = lax.broadcasted_iota(I32, (ncp, LANES), 0) * CMP_STRIDE + (CMP_BLOCK - 1)
    for c, cols in enumerate(chunks):
        cmask = c_end <= i * tq + c * LANES + lax.broadcasted_iota(I32, (ncp, LANES), 1)
        for h in range(hg):
            sm = jnp.where(cmask, sc_s[h, :, cols], NEG)
            m = jnp.max(sm, axis=0, keepdims=True)
            p = jnp.where(cmask, jnp.exp2(sm - m), 0.0)
            den = jnp.maximum(jnp.sum(p, axis=0, keepdims=True), 1e-30)
            pn = p / den
            psum_s[:, cols] = pn if h == 0 else psum_s[:, cols] + pn
            pc_s[h, :, cols] = pn.astype(BF16)
    for h in range(hg):
        oc = jnp.dot(vct_ref[0, 0], pc_s[h], preferred_element_type=F32)
        out_s[h * HEAD_DIM:(h + 1) * HEAD_DIM, :] = oc * gate(0, h)

    def reset(state):
        m_r, _, acc_r, _ = state
        m_r[...] = jnp.full(m_r.shape, NEG, F32)
        acc_r[...] = jnp.zeros(acc_r.shape, F32)

    krow = lax.broadcasted_iota(I32, (tk, LANES), 0)
    lane = lax.broadcasted_iota(I32, (tk, LANES), 1)

    def softmax_pv(state, src, vt, kind, skip=None):
        m_r, c_r, acc_r, p_r = state
        off = None if skip is None else jnp.where(skip, tk, 0)
        for c, cols in enumerate(chunks):
            qcol = lane + c * LANES
            if kind == 'causal':
                mask = krow <= qcol
            elif kind == 'older':
                mask = krow > (qcol if off is None else qcol + off)
            else:
                mask = None if off is None else krow >= off
            for h in range(hg):
                s = src[h, :, cols]
                if mask is not None:
                    s = jnp.where(mask, s, NEG)
                m_old = m_r[h, :, cols]
                m_new = jnp.maximum(m_old, jnp.max(s, axis=0, keepdims=True))
                m_r[h, :, cols] = m_new
                c_r[h, :, cols] = jnp.exp2(m_old - m_new)
                p_r[h, :, cols] = jnp.exp2(s - m_new).astype(BF16)
        for h in range(hg):
            acc_r[h] = acc_r[h] * c_r[h] + jnp.dot(vt, p_r[h], preferred_element_type=F32)

    def finish(state, branch):
        acc_r = state[2]
        for h in range(hg):
            rows = slice(h * HEAD_DIM, (h + 1) * HEAD_DIM)
            den = acc_r[h, HEAD_DIM:HEAD_DIM + 1, :]
            out_s[rows, :] = out_s[rows, :] + acc_r[h, :HEAD_DIM, :] * (gate(branch, h) / den)

    win = (mw_s, cw_s, accw_s, pw_s)
    reset(win)
    nwt = WINDOW // tk
    assert nwt == 2
    t_old = jnp.maximum(i - 2, 0)
    t_mid = jnp.maximum(i - 1, 0)
    qk(sw_s.at[0], kw_ref[0, 0, t_old], HEAD_DIM)
    qk(sw_s.at[1], kw_ref[0, 0, t_mid], HEAD_DIM)
    qk(sw_s.at[2], kw_ref[0, 0, i], HEAD_DIM)
    softmax_pv(win, sw_s.at[0], vwt_ref[0, 0, t_old], 'older', skip=i < 2)
    softmax_pv(win, sw_s.at[1], vwt_ref[0, 0, t_mid], None, skip=i < 1)
    softmax_pv(win, sw_s.at[2], vwt_ref[0, 0, i], 'causal')
    finish(win, 2)

    psum = psum_s[...]
    p_hi = psum.astype(BF16)
    p_lo = (psum - p_hi.astype(F32)).astype(BF16)
    imp = (jnp.dot(ovt_ref[...], p_hi, preferred_element_type=F32)
           + jnp.dot(ovt_ref[...], p_lo, preferred_element_type=F32))
    jb = lax.broadcasted_iota(I32, (nsel, tq), 0)
    cur = (i * tq + lax.broadcasted_iota(I32, (nsel, tq), 1)) // SEL_BLOCK
    forced = (jb == 0) | (jb == cur) | (jb == cur - 1)
    imp = jnp.where(forced, FORCE_SCORE, jnp.where(jb > cur, -FORCE_SCORE, imp))
    imp_s[...] = imp
    nrb = nsel // SUBLANES
    blocks = [imp[r * SUBLANES:(r + 1) * SUBLANES] for r in range(nrb)]
    counts = [jnp.zeros((SUBLANES, tq), F32) for _ in range(nrb)]
    sub = lax.broadcasted_iota(I32, (SUBLANES, tq), 0)
    for jp in range(nsel):
        row = jnp.broadcast_to(imp_s[jp:jp + 1, :], (SUBLANES, tq))
        for r in range(nrb):
            if r * SUBLANES > jp:
                inc = jnp.where(row >= blocks[r], 1.0, 0.0)
            elif r * SUBLANES + SUBLANES - 1 <= jp:
                inc = jnp.where(row > blocks[r], 1.0, 0.0)
            else:
                tie = jnp.where(sub > jp - r * SUBLANES, 1.0, 0.0)
                inc = jnp.where(row > blocks[r], 1.0, jnp.where(row >= blocks[r], tie, 0.0))
            counts[r] = counts[r] + inc
    top_n = min(SEL_TOP_N, nsel)
    bias = jnp.concatenate([jnp.where(cnt < top_n, 0.0, NEG) for cnt in counts], axis=0).astype(BF16)
    for h in range(hg):
        qa_s[h, HEAD_DIM:HEAD_DIM + nsel] = bias
        if nsel < HEAD_DIM:
            qa_s[h, HEAD_DIM + nsel:] = jnp.zeros((HEAD_DIM - nsel, tq), BF16)

    s_a, s_b = s_s.at[0], s_s.at[1]
    sel_q = 2 * HEAD_DIM

    sel = (m_s, c_s, acc_s, p_s)
    reset(sel)
    qk(s_a, ks_ref[0, 0, 0], sel_q)

    def sel_run(first, count, ends_on_diagonal):
        bufs = (s_a, s_b)
        for j in range(count):
            if j + 1 < count or not ends_on_diagonal:
                qk(bufs[(j + 1) % 2], ks_ref[0, 0, first + j + 1], sel_q)
            kind = 'causal' if (ends_on_diagonal and j == count - 1) else None
            softmax_pv(sel, bufs[j % 2], vst_ref[0, 0, first + j], kind)

    def sel_group(gi, carry):
        sel_run(gi * SEL_UNROLL, SEL_UNROLL, False)
        return carry

    lax.fori_loop(0, i // SEL_UNROLL, sel_group, 0)
    for rem in range(SEL_UNROLL):
        @pl.when(i % SEL_UNROLL == rem)
        def _(rem=rem):
            sel_run(i - rem, rem + 1, True)

    finish(sel, 1)

    o_ref[...] = out_s[...].T.astype(o_ref.dtype)


def _attention(proj, cos_t, sin_t, kcmp, vcmp_t, ksx, vs_t, kwx, vw_t, ov_t, *, b, s, pw, qw, gate_col):
    tq = ATTN_TILE
    nq = s // tq
    g = N_KV_GROUPS
    gw = qw // g
    hg = gw // HEAD_DIM
    n_heads = qw // HEAD_DIM
    ncp = kcmp.shape[2]
    nsel = ov_t.shape[0]
    q_blk0 = pw // gw
    kern = functools.partial(_attn_kernel, hg=hg, n_heads=n_heads)
    kv5 = lambda bi, gi, i: (bi, gi, 0, 0, 0)
    kv4 = lambda bi, gi, i: (bi, gi, 0, 0)
    return pl.pallas_call(
        kern,
        out_shape=jax.ShapeDtypeStruct((b * s, qw), BF16),
        grid=(b, g, nq),
        in_specs=[pl.BlockSpec((tq, gw), lambda bi, gi, i: (bi * nq + i, q_blk0 + gi)),
                  pl.BlockSpec((HALF_DIM, tq), lambda bi, gi, i: (0, i)),
                  pl.BlockSpec((HALF_DIM, tq), lambda bi, gi, i: (0, i)),
                  pl.BlockSpec((tq, LANES), lambda bi, gi, i: (bi * nq + i, gate_col // LANES)),
                  pl.BlockSpec((1, 1, ncp, HEAD_DIM), kv4),
                  pl.BlockSpec((1, 1, HEAD_DIM, ncp), kv4),
                  pl.BlockSpec((1, 1, nq, tq, 2 * HEAD_DIM), kv5),
                  pl.BlockSpec((1, 1, nq, V_ROWS, tq), kv5),
                  pl.BlockSpec((1, 1, nq, tq, HEAD_DIM), kv5),
                  pl.BlockSpec((1, 1, nq, V_ROWS, tq), kv5),
                  pl.BlockSpec(ov_t.shape, lambda bi, gi, i: (0, 0))],
        out_specs=pl.BlockSpec((tq, gw), lambda bi, gi, i: (bi * nq + i, gi)),
        scratch_shapes=[pltpu.VMEM((hg, 2 * HEAD_DIM, tq), BF16),
                        pltpu.VMEM((LANES, tq), F32),
                        pltpu.VMEM((nsel, tq), F32),
                        pltpu.VMEM((hg, 1, tq), F32),
                        pltpu.VMEM((hg, 1, tq), F32),
                        pltpu.VMEM((hg, V_ROWS, tq), F32),
                        pltpu.VMEM((gw, tq), F32),
                        pltpu.VMEM((hg, ncp, tq), F32),
                        pltpu.VMEM((hg, ncp, tq), BF16),
                        pltpu.VMEM((ncp, tq), F32),
                        pltpu.VMEM((2, hg, tq, tq), F32),
                        pltpu.VMEM((hg, tq, tq), BF16),
                        pltpu.VMEM((hg, 1, tq), F32),
                        pltpu.VMEM((hg, 1, tq), F32),
                        pltpu.VMEM((hg, V_ROWS, tq), F32),
                        pltpu.VMEM((3, hg, tq, tq), F32),
                        pltpu.VMEM((hg, tq, tq), BF16)],
        compiler_params=_cparams(("arbitrary", "arbitrary", "arbitrary")),
        name="nsa_attention",
    )(proj, cos_t, sin_t, proj, kcmp, vcmp_t, ksx, vs_t, kwx, vw_t, ov_t)


def _layer_norm(r, g, b):
    mu = jnp.mean(r, axis=-1, keepdims=True)
    d = r - mu
    var = jnp.mean(d * d, axis=-1, keepdims=True)
    return d * lax.rsqrt(var + LN_EPS) * g + b


def _sigmoid(x):
    return 1.0 / (1.0 + jnp.exp(-x))


def _merge_kernel(x_ref, pm_ref, at_ref, wpp_ref, wnp_ref, wgp_ref, wga_ref, wo_ref, g_ref, b_ref, o_ref,
                  xb_ref, acc_ref, *, alpha):
    c = pl.program_id(1)

    @pl.when(c == 0)
    def _():
        xb_ref[...] = x_ref[...].astype(BF16)
        acc_ref[...] = jnp.zeros(acc_ref.shape, F32)

    xb = xb_ref[...]
    y_pool = jnp.dot(pm_ref[...], wpp_ref[...], preferred_element_type=F32)
    y_attn = jnp.dot(at_ref[...], wnp_ref[...], preferred_element_type=F32)
    g_pool = _sigmoid(jnp.dot(xb, wgp_ref[...], preferred_element_type=F32))
    g_attn = _sigmoid(jnp.dot(xb, wga_ref[...], preferred_element_type=F32))
    z = g_pool * y_pool + g_attn * y_attn
    acc_ref[...] += jnp.dot(z.astype(BF16), wo_ref[...], preferred_element_type=F32)

    @pl.when(c == pl.num_programs(1) - 1)
    def _():
        o_ref[...] = _layer_norm(alpha * x_ref[...] + acc_ref[...], g_ref[...], b_ref[...])


def _merge_out(x, pm, at, wpp, wnp, wgp, wga, wo, ln_g, ln_b, alpha, tm, ck):
    n, d = x.shape
    pw = pm.shape[1]
    qw = at.shape[1]
    row = lambda i, c: (i, 0)
    colc = lambda i, c: (0, c)
    return pl.pallas_call(
        functools.partial(_merge_kernel, alpha=alpha),
        out_shape=jax.ShapeDtypeStruct((n, d), F32),
        grid=(n // tm, d // ck),
        in_specs=[pl.BlockSpec((tm, d), row), pl.BlockSpec((tm, pw), row), pl.BlockSpec((tm, qw), row),
                  pl.BlockSpec((pw, ck), colc), pl.BlockSpec((qw, ck), colc),
                  pl.BlockSpec((d, ck), colc), pl.BlockSpec((d, ck), colc),
                  pl.BlockSpec((ck, d), lambda i, c: (c, 0)),
                  pl.BlockSpec((1, d), lambda i, c: (0, 0)), pl.BlockSpec((1, d), lambda i, c: (0, 0))],
        out_specs=pl.BlockSpec((tm, d), row),
        scratch_shapes=[pltpu.VMEM((tm, d), BF16), pltpu.VMEM((tm, d), F32)],
        compiler_params=_cparams(("arbitrary", "arbitrary")),
        name="merge_out_ln",
    )(x, pm, at, wpp, wnp, wgp, wga, wo, ln_g, ln_b)


def _router_kernel(h_ref, whi_ref, wlo_ref, b_ref, e_ref, w_ref, *, n_groups, per_group):
    h = h_ref[...]
    h_hi = h.astype(BF16)
    h_lo = (h - h_hi.astype(F32)).astype(BF16)
    logits = (jnp.dot(h_hi, whi_ref[...], preferred_element_type=F32)
              + jnp.dot(h_hi, wlo_ref[...], preferred_element_type=F32)
              + jnp.dot(h_lo, whi_ref[...], preferred_element_type=F32)) + b_ref[...]
    lane = lax.broadcasted_iota(I32, logits.shape, 1)
    far = LANES

    def first_argmax(v, vmax):
        return jnp.min(jnp.where(v == vmax, lane, far), axis=-1, keepdims=True)

    gl = jnp.where(lane < n_groups, logits, NEG)
    gmax = jnp.max(gl, axis=-1, keepdims=True)
    grp = first_argmax(gl, gmax)
    gsum = jnp.sum(jnp.where(lane < n_groups, jnp.exp(gl - gmax), 0.0), axis=-1, keepdims=True)
    g_gate = 1.0 / gsum
    lo = n_groups + grp * per_group
    el = jnp.where((lane >= lo) & (lane < lo + per_group), logits, NEG)
    v1 = jnp.max(el, axis=-1, keepdims=True)
    i1 = first_argmax(el, v1)
    el2 = jnp.where(lane == i1, NEG, el)
    v2 = jnp.max(el2, axis=-1, keepdims=True)
    i2 = first_argmax(el2, v2)
    e21 = jnp.exp(v2 - v1)
    w1 = g_gate / (1.0 + e21)
    w2 = g_gate * e21 / (1.0 + e21)
    e_ref[...] = jnp.where(lane == 0, i1 - n_groups, jnp.where(lane == 1, i2 - n_groups, 0))
    w_ref[...] = jnp.where(lane == 0, w1, jnp.where(lane == 1, w2, 0.0))


def _router(h, w_hi, w_lo, bias, n_groups, per_group, tm):
    n, d = h.shape
    return pl.pallas_call(
        functools.partial(_router_kernel, n_groups=n_groups, per_group=per_group),
        out_shape=(jax.ShapeDtypeStruct((n, LANES), I32), jax.ShapeDtypeStruct((n, LANES), F32)),
        grid=(n // tm,),
        in_specs=[pl.BlockSpec((tm, d), lambda i: (i, 0)),
                  pl.BlockSpec((d, LANES), lambda i: (0, 0)), pl.BlockSpec((d, LANES), lambda i: (0, 0)),
                  pl.BlockSpec((1, LANES), lambda i: (0, 0))],
        out_specs=(pl.BlockSpec((tm, LANES), lambda i: (i, 0)), pl.BlockSpec((tm, LANES), lambda i: (i, 0))),
        compiler_params=_cparams(("arbitrary",)),
        name="moe_router",
    )(h, w_hi, w_lo, bias)


def _expert_kernel(blk0_ref, row_tok_ref, h_hbm, wg_ref, wu_ref, wd_ref, y_hbm,
                   xbuf, ybuf, wgb, wub, wdb, gsem, ysem):
    e = pl.program_id(0)
    n_exp = pl.num_programs(0)
    rows = ybuf.shape[1]
    b0 = blk0_ref[e]
    nb = blk0_ref[e + 1] - b0
    n_used = blk0_ref[n_exp]

    def row_copy(blk, slot, t, u):
        tok = row_tok_ref[blk * rows + t * SUBLANES + u]
        src = h_hbm.at[lax.shift_right_logical(tok, 3), pl.ds(tok & (SUBLANES - 1), 1)]
        return pltpu.make_async_copy(src, xbuf.at[slot, t, pl.ds(u, 1)], gsem.at[slot])

    def start_gather(blk, slot):
        def body(t, c):
            for u in range(SUBLANES):
                row_copy(blk, slot, t, u).start()
            return c
        lax.fori_loop(0, rows // SUBLANES, body, 0)

    def wait_gather(blk, slot):
        def body(t, c):
            for u in range(SUBLANES):
                row_copy(blk, slot, t, u).wait()
            return c
        lax.fori_loop(0, rows // SUBLANES, body, 0)

    def y_copy(blk, slot):
        return pltpu.make_async_copy(ybuf.at[slot], y_hbm.at[pl.ds(blk * rows, rows)], ysem.at[slot])

    @pl.when((e == 0) & (n_used > 0))
    def _():
        start_gather(0, 0)

    @pl.when(nb > 0)
    def _():
        wgb[...] = wg_ref[0].astype(BF16)
        wub[...] = wu_ref[0].astype(BF16)
        wdb[...] = wd_ref[0].astype(BF16)

    def block(j, carry):
        b = b0 + j
        slot = b % 2

        wait_gather(b, slot)
        x = xbuf[slot].reshape(rows, xbuf.shape[3]).astype(BF16)
        hgate = jnp.dot(x, wgb[...], preferred_element_type=F32)
        hup = jnp.dot(x, wub[...], preferred_element_type=F32)
        nxt = jnp.minimum(b + 1, n_used - 1)
        for t in range(rows // SUBLANES):
            for u in range(SUBLANES):
                row_copy(nxt, 1 - slot, t, u).start()
        a = (hgate * _sigmoid(hgate) * hup).astype(BF16)
        y = jnp.dot(a, wdb[...], preferred_element_type=F32)

        @pl.when(b >= 2)
        def _():
            y_copy(b - 2, slot).wait()

        ybuf[slot] = y
        y_copy(b, slot).start()
        return carry

    lax.fori_loop(0, nb, block, 0)

    @pl.when((e == n_exp - 1) & (n_used > 0))
    def _():
        wait_gather(n_used - 1, n_used % 2)

    @pl.when(e == n_exp - 1)
    def _():
        for back in (2, 1):
            @pl.when(n_used >= back)
            def _(back=back):
                y_copy(n_used - back, (n_used - back) % 2).wait()

        n_blk = y_hbm.shape[0] // rows
        ybuf[0] = jnp.zeros(ybuf.shape[1:], ybuf.dtype)

        def zero_start(blk, c):
            y_copy(blk, 0).start()
            return c

        def zero_wait(blk, c):
            y_copy(blk, 0).wait()
            return c

        lax.fori_loop(n_used, n_blk, zero_start, 0)
        lax.fori_loop(n_used, n_blk, zero_wait, 0)


def _experts(blk0, row_tok, h, w_gate, w_up, w_down, n_blk):
    n_exp = w_gate.shape[0]
    rows = EXPERT_ROWS
    d = h.shape[1]
    hid = w_gate.shape[2]
    grid_spec = pltpu.PrefetchScalarGridSpec(
        num_scalar_prefetch=2,
        grid=(n_exp,),
        in_specs=[pl.BlockSpec(memory_space=pl.ANY),
                  pl.BlockSpec((1, d, hid), lambda e, b0, rt: (e, 0, 0)),
                  pl.BlockSpec((1, d, hid), lambda e, b0, rt: (e, 0, 0)),
                  pl.BlockSpec((1, hid, d), lambda e, b0, rt: (e, 0, 0))],
        out_specs=pl.BlockSpec(memory_space=pl.ANY),
        scratch_shapes=[pltpu.VMEM((2, rows // SUBLANES, SUBLANES, d), F32), pltpu.VMEM((2, rows, d), F32),
                        pltpu.VMEM((d, hid), BF16), pltpu.VMEM((d, hid), BF16), pltpu.VMEM((hid, d), BF16),
                        pltpu.SemaphoreType.DMA((2,)), pltpu.SemaphoreType.DMA((2,))],
    )
    return pl.pallas_call(
        _expert_kernel,
        out_shape=jax.ShapeDtypeStruct((n_blk * rows, d), F32),
        grid_spec=grid_spec,
        compiler_params=_cparams(("arbitrary",), row_dma=True),
        name="moe_experts",
    )(blk0, row_tok, h.reshape(h.shape[0] // SUBLANES, SUBLANES, d), w_gate, w_up, w_down)


def _combine_kernel(dest_ref, h_ref, w_ref, y_hbm, g_ref, b_ref, o_ref, ybuf, sem, *, alpha):
    i = pl.program_id(0)
    n_steps = pl.num_programs(0)
    tm = h_ref.shape[0]
    slot = i % 2

    def row_copy(step_, slot_, t, u, k):
        row = dest_ref[(step_ * tm + t * SUBLANES + u) * TOP_K + k]
        src = y_hbm.at[lax.shift_right_logical(row, 3), pl.ds(row & (SUBLANES - 1), 1)]
        return pltpu.make_async_copy(src, ybuf.at[slot_, k, t, pl.ds(u, 1)], sem.at[slot_])

    def start_gather(step_, slot_):
        def body(t, c):
            for u in range(SUBLANES):
                for k in range(TOP_K):
                    row_copy(step_, slot_, t, u, k).start()
            return c
        lax.fori_loop(0, tm // SUBLANES, body, 0)

    def wait_gather(step_, slot_):
        def body(t, c):
            for u in range(SUBLANES):
                for k in range(TOP_K):
                    row_copy(step_, slot_, t, u, k).wait()
            return c
        lax.fori_loop(0, tm // SUBLANES, body, 0)

    @pl.when(i == 0)
    def _():
        start_gather(0, 0)

    wait_gather(i, slot)
    w = w_ref[...]
    y = ybuf[slot, 0].reshape(h_ref.shape) * w[:, 0:1]
    for k in range(1, TOP_K):
        y = y + ybuf[slot, k].reshape(h_ref.shape) * w[:, k:k + 1]
    nxt = jnp.minimum(i + 1, n_steps - 1)
    for t in range(tm // SUBLANES):
        for u in range(SUBLANES):
            for k in range(TOP_K):
                row_copy(nxt, 1 - slot, t, u, k).start()
    o_ref[...] = _layer_norm(alpha * h_ref[...] + y, g_ref[...], b_ref[...])

    @pl.when(i == n_steps - 1)
    def _():
        wait_gather(i, 1 - slot)


def _combine(dest, h, wts, ybuf, ln_g, ln_b, alpha, tm):
    n, d = h.shape
    grid_spec = pltpu.PrefetchScalarGridSpec(
        num_scalar_prefetch=1,
        grid=(n // tm,),
        in_specs=[pl.BlockSpec((tm, d), lambda i, ds: (i, 0)),
                  pl.BlockSpec((tm, LANES), lambda i, ds: (i, 0)),
                  pl.BlockSpec(memory_space=pl.ANY),
                  pl.BlockSpec((1, d), lambda i, ds: (0, 0)), pl.BlockSpec((1, d), lambda i, ds: (0, 0))],
        out_specs=pl.BlockSpec((tm, d), lambda i, ds: (i, 0)),
        scratch_shapes=[pltpu.VMEM((2, TOP_K, tm // SUBLANES, SUBLANES, d), F32), pltpu.SemaphoreType.DMA((2,))],
    )
    return pl.pallas_call(
        functools.partial(_combine_kernel, alpha=alpha),
        out_shape=jax.ShapeDtypeStruct((n, d), F32),
        grid_spec=grid_spec,
        compiler_params=_cparams(("arbitrary",), row_dma=True),
        name="moe_combine_ln",
    )(dest, h, wts, ybuf.reshape(ybuf.shape[0] // SUBLANES, SUBLANES, d), ln_g, ln_b)


def _rope_tables(s):
    inv_freq = ROPE_THETA ** (-2.0 * jnp.arange(HALF_DIM, dtype=F32) / HEAD_DIM)

    def tables(pos):
        ang = pos.astype(F32)[:, None] * inv_freq[None, :]
        return jnp.cos(ang), jnp.sin(ang)

    cos, sin = tables(jnp.arange(s))
    n_rows = s // CMP_STRIDE
    c_end = jnp.arange(n_rows) * CMP_STRIDE + (CMP_BLOCK - 1)
    cos_c, sin_c = tables(c_end)
    full = lambda c: jnp.concatenate([c, c], axis=1)
    signed = lambda sn: jnp.concatenate([-sn, sn], axis=1)
    return cos.T, sin.T, full(cos), signed(sin), full(cos_c), signed(sin_c)


def _overlap_t(s):
    n_rows = s // CMP_STRIDE
    n_cmp = (s - CMP_BLOCK) // CMP_STRIDE + 1
    n_sel = s // SEL_BLOCK
    c_start = np.arange(n_rows) * CMP_STRIDE
    s_start = np.arange(n_sel) * SEL_BLOCK
    ov = ((c_start[None, :] + CMP_BLOCK - 1 >= s_start[:, None])
          & (c_start[None, :] <= s_start[:, None] + SEL_BLOCK - 1)
          & (np.arange(n_rows)[None, :] < n_cmp))
    return jnp.asarray(ov.astype(np.float32), dtype=BF16)


def _dispatch_plan(eid, n_experts):
    n = eid.shape[0]
    m = n * TOP_K
    rows_per = EXPERT_ROWS
    flat_e = eid.reshape(-1)
    onehot = (flat_e[:, None] == jnp.arange(n_experts)[None, :]).astype(I32)
    pos = jnp.take_along_axis(jnp.cumsum(onehot, axis=0), flat_e[:, None], axis=1)[:, 0] - 1
    sizes = jnp.sum(onehot, axis=0)
    padded = (sizes + rows_per - 1) // rows_per * rows_per
    ends = jnp.cumsum(padded)
    dest = (ends - padded)[flat_e] + pos
    n_blk = -(-(m + n_experts * (rows_per - 1)) // rows_per)
    row_tok = jnp.zeros((n_blk * rows_per,), I32).at[dest].set(jnp.arange(m, dtype=I32) // TOP_K)
    blk0 = jnp.concatenate([jnp.zeros((1,), I32), ends.astype(I32) // rows_per])
    return dest.astype(I32), row_tok, blk0, n_blk


def kernel(x, w_in, pool_mix, pool_scale, w_pool_proj, w_nsa_proj, cmp_pos_k, cmp_pos_v, cmp_k_w1, cmp_k_w2,
           cmp_v_w1, cmp_v_w2, w_out, ln1_g, ln1_b, router_group_w, router_group_b, router_expert_w,
           router_expert_b, w_gate, w_up, w_down, ln2_g, ln2_b):
    b, s, d = x.shape
    n = b * s
    depth = w_in.shape[0]
    alpha = (2.0 * depth) ** 0.25
    pw = pool_mix.shape[1] * pool_mix.shape[2]
    qw = w_nsa_proj.shape[1]
    kvw = N_KV_GROUPS * HEAD_DIM
    n_groups, _, per_group = router_expert_w.shape[1:]
    n_experts = n_groups * per_group
    gate_w = 3 * (qw // HEAD_DIM)
    assert s % ATTN_TILE == 0 and WINDOW % ATTN_TILE == 0 and ATTN_TILE % SEL_BLOCK == 0
    assert gate_w <= LANES and n_groups + n_experts <= LANES and s // SEL_BLOCK <= HEAD_DIM
    c_q = pw
    c_kv = pw + qw
    c_gate = c_kv + 6 * kvw
    c_merge = c_gate + gate_w
    tn = 768
    width_a = -(-(c_gate + LANES) // tn) * tn
    assert c_gate % LANES == 0

    cos_t, sin_t, cos_f, sin_s, cos_c, sin_c = _rope_tables(s)
    ov_t = _overlap_t(s)
    assert c_kv % kvw == 0

    h = x.reshape(n, d)
    for l in range(depth):
        wl = w_in[l]
        w_a = jnp.pad(wl[:, :c_merge], ((0, 0), (0, width_a - c_merge))).astype(BF16)
        w_gp = wl[:, c_merge:c_merge + d].astype(BF16)
        w_ga = wl[:, c_merge + d:].astype(BF16)

        proj = _project(h, w_a, _tile(n, 1024), tn)

        mixed = _pool_mixer(proj, pool_mix[l].astype(BF16), pool_scale[l].reshape(1, pw), b, s, pw, _tile(s, 512))

        kv_blk = c_kv // kvw
        kcmp, vcmp_t = _compress(
            proj, kv_blk, b, s,
            cmp_pos_k[l].reshape(1, CMP_BLOCK * HEAD_DIM), cmp_pos_v[l].reshape(1, CMP_BLOCK * HEAD_DIM),
            cmp_k_w1[l].astype(BF16), cmp_k_w2[l].astype(BF16), cmp_v_w1[l].astype(BF16),
            cmp_v_w2[l].T.astype(BF16), cos_c, sin_c)
        ksx, vs_t, kwx, vw_t = _kv_prep(proj, kv_blk, b, s, cos_f, sin_s)

        attn = _attention(proj, cos_t, sin_t, kcmp, vcmp_t, ksx, vs_t, kwx, vw_t, ov_t,
                          b=b, s=s, pw=pw, qw=qw, gate_col=c_gate)

        h = _merge_out(h, mixed, attn, w_pool_proj[l].astype(BF16), w_nsa_proj[l].astype(BF16), w_gp, w_ga,
                       w_out[l].astype(BF16), ln1_g[l].reshape(1, d), ln1_b[l].reshape(1, d), alpha,
                       _tile(n, 512), _tile(d, 512))

        w_r = jnp.concatenate([router_group_w[l], router_expert_w[l].transpose(1, 0, 2).reshape(d, n_experts)], axis=1)
        w_r = jnp.pad(w_r, ((0, 0), (0, LANES - w_r.shape[1])))
        w_r_hi = w_r.astype(BF16)
        w_r_lo = (w_r - w_r_hi.astype(F32)).astype(BF16)
        b_r = jnp.pad(jnp.concatenate([router_group_b[l], router_expert_b[l].reshape(-1)]),
                      (0, LANES - n_groups - n_experts)).reshape(1, LANES)
        eid_l, wts_l = _router(h, w_r_hi, w_r_lo, b_r, n_groups, per_group, _tile(n, 512))
        dest, row_tok, blk0, n_blk = _dispatch_plan(eid_l[:, :TOP_K], n_experts)

        ybuf = _experts(blk0, row_tok, h, w_gate[l], w_up[l], w_down[l], n_blk)
        h = _combine(dest, h, wts_l, ybuf, ln2_g[l].reshape(1, d), ln2_b[l].reshape(1, d), alpha, _tile(n, 256))
    return h.reshape(b, s, d)
```

```python
import functools
import math

import numpy as np
import jax
import jax.numpy as jnp
from jax import lax
from jax.experimental import pallas as pl
from jax.experimental.pallas import tpu as pltpu

F32 = jnp.float32
BF16 = jnp.bfloat16
I32 = jnp.int32

POOL_WINDOWS = (2, 4, 8, 16)
POOL_GROUPS = 4
POOL_HALO = 16
HEAD_DIM = 64
HALF_DIM = HEAD_DIM // 2
V_ROWS = HEAD_DIM + 16
N_KV_GROUPS = 4
CMP_BLOCK = 32
CMP_STRIDE = 16
SEL_BLOCK = 64
SEL_TOP_N = 16
WINDOW = 512
ROPE_THETA = 10000.0
FORCE_SCORE = 1e6
TOP_K = 2
LN_EPS = 1e-5
NEG = -1e30

LANES = 128
SUBLANES = 8
VMEM_BYTES_V7X = 64 * 1024 * 1024
VMEM_LIMIT = VMEM_BYTES_V7X - 8 * 1024 * 1024

ATTN_TILE = 256
SEL_UNROLL = 4
EXPERT_ROWS = 256


def _cparams(sem, vmem=VMEM_LIMIT, row_dma=False):
    return pltpu.CompilerParams(dimension_semantics=sem, vmem_limit_bytes=vmem, disable_bounds_checks=row_dma)


def _tile(n, pref):
    t = min(n, pref)
    while n % t:
        t //= 2
    return t


def _proj_kernel(x_ref, w_ref, o_ref, xb_ref):
    @pl.when(pl.program_id(1) == 0)
    def _():
        xb_ref[...] = x_ref[...].astype(BF16)

    o_ref[...] = jnp.dot(xb_ref[...], w_ref[...], preferred_element_type=F32)


def _project(x, w, tm, tn):
    n, d = x.shape
    cols = w.shape[1]
    return pl.pallas_call(
        _proj_kernel,
        out_shape=jax.ShapeDtypeStruct((n, cols), F32),
        grid=(n // tm, cols // tn),
        in_specs=[pl.BlockSpec((tm, d), lambda i, j: (i, 0)),
                  pl.BlockSpec((d, tn), lambda i, j: (0, j))],
        out_specs=pl.BlockSpec((tm, tn), lambda i, j: (i, j)),
        scratch_shapes=[pltpu.VMEM((tm, d), BF16)],
        compiler_params=_cparams(("arbitrary", "arbitrary")),
        name="in_proj",
    )(x, w)


def _pool_kernel(u_ref, halo_ref, mix_ref, scale_ref, o_ref, ext_ref):
    i = pl.program_id(1)
    ts = u_ref.shape[0]
    gd = mix_ref.shape[1]
    ext_ref[POOL_HALO:, :] = u_ref[...]
    ext_ref[:POOL_HALO, :] = jnp.where(i == 0, 0.0, halo_ref[...])
    t = i * ts + lax.broadcasted_iota(I32, (ts, gd), 0)
    for g, w in enumerate(POOL_WINDOWS):
        cols = slice(g * gd, (g + 1) * gd)
        s = ext_ref[:, cols]
        k = 1
        while k < w:
            s = s + pltpu.roll(s, k, axis=0)
            k *= 2
        cnt = jnp.minimum(t + 1, w).astype(F32)
        pooled = s[POOL_HALO:, :] / cnt - u_ref[:, cols]
        mixed = jnp.dot(pooled.astype(BF16), mix_ref[g], preferred_element_type=F32)
        o_ref[:, cols] = (mixed * scale_ref[:, cols]).astype(BF16)


def _pool_mixer(proj, pool_mix_b, pool_scale, b, s, pw, ts):
    ns = s // ts
    hb = ts // POOL_HALO
    return pl.pallas_call(
        _pool_kernel,
        out_shape=jax.ShapeDtypeStruct((b * s, pw), BF16),
        grid=(b, ns),
        in_specs=[pl.BlockSpec((ts, pw), lambda bi, i: (bi * ns + i, 0)),
                  pl.BlockSpec((POOL_HALO, pw), lambda bi, i: (jnp.maximum((bi * ns + i) * hb - 1, 0), 0)),
                  pl.BlockSpec(pool_mix_b.shape, lambda bi, i: (0, 0, 0)),
                  pl.BlockSpec((1, pw), lambda bi, i: (0, 0))],
        out_specs=pl.BlockSpec((ts, pw), lambda bi, i: (bi * ns + i, 0)),
        scratch_shapes=[pltpu.VMEM((POOL_HALO + ts, pw), F32)],
        compiler_params=_cparams(("arbitrary", "arbitrary")),
        name="pool_mixer",
    )(proj, proj, pool_mix_b, pool_scale)


def _gelu_tanh(x):
    return 0.5 * x * (1.0 + jnp.tanh(math.sqrt(2.0 / math.pi) * (x + 0.044715 * (x * x * x))))


def _swap_halves(x):
    return jnp.concatenate([x[:, HALF_DIM:], x[:, :HALF_DIM]], axis=1)


def _cmp_kernel(kc0_ref, kc1_ref, vc0_ref, vc1_ref, pk_ref, pv_ref, kw1_ref, kw2_ref, vw1_ref, vw2t_ref,
                cos_ref, sin_ref, ko_ref, vo_ref, r_s):
    kc_refs = (kc0_ref, kc1_ref)
    vc_refs = (vc0_ref, vc1_ref)
    s = kc0_ref.shape[0]
    nrow = s // CMP_STRIDE
    half = CMP_STRIDE * HEAD_DIM

    def hidden(p_ref, w1_ref, g):
        r = r_s[g]
        r_next = pltpu.roll(r, nrow - 1, axis=0)
        a = (r + p_ref[:, :half]).astype(BF16)
        bb = (r_next + p_ref[:, half:]).astype(BF16)
        h = (jnp.dot(a, w1_ref[:half, :], preferred_element_type=F32)
             + jnp.dot(bb, w1_ref[half:, :], preferred_element_type=F32))
        return _gelu_tanh(h).astype(BF16)

    gpl = LANES // HEAD_DIM

    def regroup(src_refs):
        for t in range(CMP_STRIDE):
            for j, src_ref in enumerate(src_refs):
                rows = src_ref[pl.ds(t, nrow, stride=CMP_STRIDE), :]
                for gg in range(gpl):
                    r_s[j * gpl + gg, :, t * HEAD_DIM:(t + 1) * HEAD_DIM] = rows[:, gg * HEAD_DIM:(gg + 1) * HEAD_DIM]

    regroup(kc_refs)
    for g in range(N_KV_GROUPS):
        k = jnp.dot(hidden(pk_ref, kw1_ref, g), kw2_ref[...], preferred_element_type=F32)
        k = k * cos_ref[...] + _swap_halves(k) * sin_ref[...]
        ko_ref[0, g] = k.astype(BF16)
    regroup(vc_refs)
    for g in range(N_KV_GROUPS):
        hv = hidden(pv_ref, vw1_ref, g)
        vt = lax.dot_general(vw2t_ref[...], hv, (((1,), (1,)), ((), ())), preferred_element_type=F32)
        vo_ref[0, g] = vt.astype(BF16)


def _compress(proj, col_blk, b, s, pk, pv, kw1, kw2, vw1, vw2t, cos_c, sin_c):
    g = N_KV_GROUPS
    kvw = g * HEAD_DIM
    nrow = s // CMP_STRIDE

    def whole(a):
        return pl.BlockSpec(a.shape, lambda bi: (0,) * a.ndim)

    assert kvw == 2 * LANES

    def lane_tile(j):
        return pl.BlockSpec((s, LANES), lambda bi: (bi, col_blk * (kvw // LANES) + j))

    return pl.pallas_call(
        _cmp_kernel,
        out_shape=(jax.ShapeDtypeStruct((b, g, nrow, HEAD_DIM), BF16),
                   jax.ShapeDtypeStruct((b, g, HEAD_DIM, nrow), BF16)),
        grid=(b,),
        in_specs=[lane_tile(0), lane_tile(1), lane_tile(2), lane_tile(3),
                  whole(pk), whole(pv), whole(kw1), whole(kw2), whole(vw1), whole(vw2t),
                  whole(cos_c), whole(sin_c)],
        out_specs=(pl.BlockSpec((1, g, nrow, HEAD_DIM), lambda bi: (bi, 0, 0, 0)),
                   pl.BlockSpec((1, g, HEAD_DIM, nrow), lambda bi: (bi, 0, 0, 0))),
        scratch_shapes=[pltpu.VMEM((g, nrow, CMP_STRIDE * HEAD_DIM), F32)],
        compiler_params=_cparams(("arbitrary",)),
        name="compress_kv",
    )(proj, proj, proj, proj, pk, pv, kw1, kw2, vw1, vw2t, cos_c, sin_c)


def _kvprep_kernel(ks_ref, vs_ref, kw_ref, vw_ref, cos_ref, sin_ref, kso_ref, vso_ref, kwo_ref, vwo_ref):
    i = pl.program_id(1)
    tq = ks_ref.shape[0]
    cos = cos_ref[...]
    sin = sin_ref[...]
    blk = (i * tq + lax.broadcasted_iota(I32, (tq, HEAD_DIM), 0)) // SEL_BLOCK
    onehot = jnp.where(blk == lax.broadcasted_iota(I32, (tq, HEAD_DIM), 1), 1.0, 0.0).astype(BF16)
    ks = ks_ref[...]
    kw = kw_ref[...]
    vs_t = vs_ref[...].T
    vw_t = vw_ref[...].T
    for g in range(N_KV_GROUPS):
        cols = slice(g * HEAD_DIM, (g + 1) * HEAD_DIM)
        k = ks[:, cols]
        kso_ref[0, g, 0, :, :HEAD_DIM] = (k * cos + _swap_halves(k) * sin).astype(BF16)
        kso_ref[0, g, 0, :, HEAD_DIM:] = onehot
        k = kw[:, cols]
        kwo_ref[0, g, 0] = (k * cos + _swap_halves(k) * sin).astype(BF16)
        ones = jnp.ones((V_ROWS - HEAD_DIM, tq), BF16)
        vso_ref[0, g, 0, :HEAD_DIM] = vs_t[cols].astype(BF16)
        vso_ref[0, g, 0, HEAD_DIM:] = ones
        vwo_ref[0, g, 0, :HEAD_DIM] = vw_t[cols].astype(BF16)
        vwo_ref[0, g, 0, HEAD_DIM:] = ones


def _kv_prep(proj, col_blk, b, s, cos_f, sin_s):
    g = N_KV_GROUPS
    kvw = g * HEAD_DIM
    tq = ATTN_TILE
    nq = s // tq

    def col(j):
        return pl.BlockSpec((tq, kvw), lambda bi, i: (bi * nq + i, col_blk + j))

    tab = pl.BlockSpec((tq, HEAD_DIM), lambda bi, i: (i, 0))

    def out(r, c):
        return (jax.ShapeDtypeStruct((b, g, nq, r, c), BF16),
                pl.BlockSpec((1, g, 1, r, c), lambda bi, i: (bi, 0, i, 0, 0)))

    outs = [out(tq, 2 * HEAD_DIM), out(V_ROWS, tq), out(tq, HEAD_DIM), out(V_ROWS, tq)]
    return pl.pallas_call(
        _kvprep_kernel,
        out_shape=tuple(o[0] for o in outs),
        grid=(b, nq),
        in_specs=[col(2), col(3), col(4), col(5), tab, tab],
        out_specs=tuple(o[1] for o in outs),
        compiler_params=_cparams(("arbitrary", "arbitrary")),
        name="kv_prep",
    )(proj, proj, proj, proj, cos_f, sin_s)


def _attn_kernel(q_ref, cos_ref, sin_ref, gate_ref, kc_ref, vct_ref, ks_ref, vst_ref, kw_ref, vwt_ref, ovt_ref,
                 o_ref, qa_s, g_s, imp_s, m_s, c_s, acc_s, out_s, sc_s, pc_s, psum_s, s_s, p_s,
                 mw_s, cw_s, accw_s, sw_s, pw_s, *, hg, n_heads):
    g = pl.program_id(1)
    i = pl.program_id(2)
    tq = q_ref.shape[0]
    tk = tq
    ncp = kc_ref.shape[2]
    nsel = ovt_ref.shape[0]
    scale = HEAD_DIM ** -0.5 * math.log2(math.e)

    qt = q_ref[...].T
    cos = cos_ref[...]
    sin = sin_ref[...]
    for h in range(hg):
        x1 = qt[h * HEAD_DIM:h * HEAD_DIM + HALF_DIM]
        x2 = qt[h * HEAD_DIM + HALF_DIM:(h + 1) * HEAD_DIM]
        qa_s[h, :HALF_DIM] = ((x1 * cos - x2 * sin) * scale).astype(BF16)
        qa_s[h, HALF_DIM:HEAD_DIM] = ((x2 * cos + x1 * sin) * scale).astype(BF16)
    g_s[...] = gate_ref[...].T

    def gate(branch, h):
        row = g_s[pl.ds(branch * n_heads + g * hg + h, 1), :]
        return 1.0 / (1.0 + jnp.exp(-row))

    chunks = [slice(c * LANES, (c + 1) * LANES) for c in range(tq // LANES)]

    def qk(dst, kx, qrows):
        for h in range(hg):
            dst[h] = jnp.dot(kx, qa_s[h, :qrows], preferred_element_type=F32)

    qk(sc_s, kc_ref[0, 0], HEAD_DIM)
    c_end = lax.broadcasted_iota(I32, (ncp, LANES), 0) * CMP_STRIDE + (CMP_BLOCK - 1)
    for c, cols in enumerate(chunks):
        cmask = c_end <= i * tq + c * LANES + lax.broadcasted_iota(I32, (ncp, LANES), 1)
        for h in range(hg):
            sm = jnp.where(cmask, sc_s[h, :, cols], NEG)
            m = jnp.max(sm, axis=0, keepdims=True)
            p = jnp.where(cmask, jnp.exp2(sm - m), 0.0)
            den = jnp.maximum(jnp.sum(p, axis=0, keepdims=True), 1e-30)
            pn = p / den
            psum_s[:, cols] = pn if h == 0 else psum_s[:, cols] + pn
            pc_s[h, :, cols] = pn.astype(BF16)
    for h in range(hg):
        oc = jnp.dot(vct_ref[0, 0], pc_s[h], preferred_element_type=F32)
        out_s[h * HEAD_DIM:(h + 1) * HEAD_DIM, :] = oc * gate(0, h)

    def reset(state):
        m_r, _, acc_r, _ = state
        m_r[...] = jnp.full(m_r.shape, NEG, F32)
        acc_r[...] = jnp.zeros(acc_r.shape, F32)

    krow = lax.broadcasted_iota(I32, (tk, LANES), 0)
    lane = lax.broadcasted_iota(I32, (tk, LANES), 1)

    def softmax_pv(state, src, vt, kind, skip=None):
        m_r, c_r, acc_r, p_r = state
        off = None if skip is None else jnp.where(skip, tk, 0)
        for c, cols in enumerate(chunks):
            qcol = lane + c * LANES
            if kind == 'causal':
                mask = krow <= qcol
            elif kind == 'older':
                mask = krow > (qcol if off is None else qcol + off)
            else:
                mask = None if off is None else krow >= off
            for h in range(hg):
                s = src[h, :, cols]
                if mask is not None:
                    s = jnp.where(mask, s, NEG)
                m_old = m_r[h, :, cols]
                m_new = jnp.maximum(m_old, jnp.max(s, axis=0, keepdims=True))
                m_r[h, :, cols] = m_new
                c_r[h, :, cols] = jnp.exp2(m_old - m_new)
                p_r[h, :, cols] = jnp.exp2(s - m_new).astype(BF16)
        for h in range(hg):
            acc_r[h] = acc_r[h] * c_r[h] + jnp.dot(vt, p_r[h], preferred_element_type=F32)

    def finish(state, branch):
        acc_r = state[2]
        for h in range(hg):
            rows = slice(h * HEAD_DIM, (h + 1) * HEAD_DIM)
            den = acc_r[h, HEAD_DIM:HEAD_DIM + 1, :]
            out_s[rows, :] = out_s[rows, :] + acc_r[h, :HEAD_DIM, :] * (gate(branch, h) / den)

    psum = psum_s[...]
    p_hi = psum.astype(BF16)
    p_lo = (psum - p_hi.astype(F32)).astype(BF16)
    imp = (jnp.dot(ovt_ref[...], p_hi, preferred_element_type=F32)
           + jnp.dot(ovt_ref[...], p_lo, preferred_element_type=F32))
    jb = lax.broadcasted_iota(I32, (nsel, tq), 0)
    cur = (i * tq + lax.broadcasted_iota(I32, (nsel, tq), 1)) // SEL_BLOCK
    forced = (jb == 0) | (jb == cur) | (jb == cur - 1)
    imp = jnp.where(forced, FORCE_SCORE, jnp.where(jb > cur, -FORCE_SCORE, imp))
    imp_s[...] = imp
    nrb = nsel // SUBLANES
    blocks = [imp[r * SUBLANES:(r + 1) * SUBLANES] for r in range(nrb)]
    counts = [jnp.zeros((SUBLANES, tq), F32) for _ in range(nrb)]
    sub = lax.broadcasted_iota(I32, (SUBLANES, tq), 0)
    for jp in range(nsel):
        row = jnp.broadcast_to(imp_s[jp:jp + 1, :], (SUBLANES, tq))
        for r in range(nrb):
            if r * SUBLANES > jp:
                inc = jnp.where(row >= blocks[r], 1.0, 0.0)
            elif r * SUBLANES + SUBLANES - 1 <= jp:
                inc = jnp.where(row > blocks[r], 1.0, 0.0)
            else:
                tie = jnp.where(sub > jp - r * SUBLANES, 1.0, 0.0)
                inc = jnp.where(row > blocks[r], 1.0, jnp.where(row >= blocks[r], tie, 0.0))
            counts[r] = counts[r] + inc
    top_n = min(SEL_TOP_N, nsel)
    bias = jnp.concatenate([jnp.where(cnt < top_n, 0.0, NEG) for cnt in counts], axis=0).astype(BF16)
    for h in range(hg):
        qa_s[h, HEAD_DIM:HEAD_DIM + nsel] = bias
        if nsel < HEAD_DIM:
            qa_s[h, HEAD_DIM + nsel:] = jnp.zeros((HEAD_DIM - nsel, tq), BF16)

    s_a, s_b = s_s.at[0], s_s.at[1]
    sel_q = 2 * HEAD_DIM

    sel = (m_s, c_s, acc_s, p_s)
    reset(sel)
    qk(s_a, ks_ref[0, 0, 0], sel_q)

    def sel_run(first, count, ends_on_diagonal):
        bufs = (s_a, s_b)
        for j in range(count):
            if j + 1 < count or not ends_on_diagonal:
                qk(bufs[(j + 1) % 2], ks_ref[0, 0, first + j + 1], sel_q)
            kind = 'causal' if (ends_on_diagonal and j == count - 1) else None
            softmax_pv(sel, bufs[j % 2], vst_ref[0, 0, first + j], kind)

    def sel_group(gi, carry):
        sel_run(gi * SEL_UNROLL, SEL_UNROLL, False)
        return carry

    lax.fori_loop(0, i // SEL_UNROLL, sel_group, 0)
    for rem in range(SEL_UNROLL):
        @pl.when(i % SEL_UNROLL == rem)
        def _(rem=rem):
            sel_run(i - rem, rem + 1, True)

    finish(sel, 1)

    win = (mw_s, cw_s, accw_s, pw_s)
    reset(win)
    nwt = WINDOW // tk
    assert nwt == 2
    t_old = jnp.maximum(i - 2, 0)
    t_mid = jnp.maximum(i - 1, 0)
    qk(sw_s.at[0], kw_ref[0, 0, t_old], HEAD_DIM)
    qk(sw_s.at[1], kw_ref[0, 0, t_mid], HEAD_DIM)
    qk(sw_s.at[2], kw_ref[0, 0, i], HEAD_DIM)
    softmax_pv(win, sw_s.at[0], vwt_ref[0, 0, t_old], 'older', skip=i < 2)
    softmax_pv(win, sw_s.at[1], vwt_ref[0, 0, t_mid], None, skip=i < 1)
    softmax_pv(win, sw_s.at[2], vwt_ref[0, 0, i], 'causal')
    finish(win, 2)

    o_ref[...] = out_s[...].T.astype(o_ref.dtype)


def _attention(proj, cos_t, sin_t, kcmp, vcmp_t, ksx, vs_t, kwx, vw_t, ov_t, *, b, s, pw, qw, gate_col):
    tq = ATTN_TILE
    nq = s // tq
    g = N_KV_GROUPS
    gw = qw // g
    hg = gw // HEAD_DIM
    n_heads = qw // HEAD_DIM
    ncp = kcmp.shape[2]
    nsel = ov_t.shape[0]
    q_blk0 = pw // gw
    kern = functools.partial(_attn_kernel, hg=hg, n_heads=n_heads)
    kv5 = lambda bi, gi, i: (bi, gi, 0, 0, 0)
    kv4 = lambda bi, gi, i: (bi, gi, 0, 0)
    return pl.pallas_call(
        kern,
        out_shape=jax.ShapeDtypeStruct((b * s, qw), BF16),
        grid=(b, g, nq),
        in_specs=[pl.BlockSpec((tq, gw), lambda bi, gi, i: (bi * nq + i, q_blk0 + gi)),
                  pl.BlockSpec((HALF_DIM, tq), lambda bi, gi, i: (0, i)),
                  pl.BlockSpec((HALF_DIM, tq), lambda bi, gi, i: (0, i)),
                  pl.BlockSpec((tq, LANES), lambda bi, gi, i: (bi * nq + i, gate_col // LANES)),
                  pl.BlockSpec((1, 1, ncp, HEAD_DIM), kv4),
                  pl.BlockSpec((1, 1, HEAD_DIM, ncp), kv4),
                  pl.BlockSpec((1, 1, nq, tq, 2 * HEAD_DIM), kv5),
                  pl.BlockSpec((1, 1, nq, V_ROWS, tq), kv5),
                  pl.BlockSpec((1, 1, nq, tq, HEAD_DIM), kv5),
                  pl.BlockSpec((1, 1, nq, V_ROWS, tq), kv5),
                  pl.BlockSpec(ov_t.shape, lambda bi, gi, i: (0, 0))],
        out_specs=pl.BlockSpec((tq, gw), lambda bi, gi, i: (bi * nq + i, gi)),
        scratch_shapes=[pltpu.VMEM((hg, 2 * HEAD_DIM, tq), BF16),
                        pltpu.VMEM((LANES, tq), F32),
                        pltpu.VMEM((nsel, tq), F32),
                        pltpu.VMEM((hg, 1, tq), F32),
                        pltpu.VMEM((hg, 1, tq), F32),
                        pltpu.VMEM((hg, V_ROWS, tq), F32),
                        pltpu.VMEM((gw, tq), F32),
                        pltpu.VMEM((hg, ncp, tq), F32),
                        pltpu.VMEM((hg, ncp, tq), BF16),
                        pltpu.VMEM((ncp, tq), F32),
                        pltpu.VMEM((2, hg, tq, tq), F32),
                        pltpu.VMEM((hg, tq, tq), BF16),
                        pltpu.VMEM((hg, 1, tq), F32),
                        pltpu.VMEM((hg, 1, tq), F32),
                        pltpu.VMEM((hg, V_ROWS, tq), F32),
                        pltpu.VMEM((3, hg, tq, tq), F32),
                        pltpu.VMEM((hg, tq, tq), BF16)],
        compiler_params=_cparams(("arbitrary", "arbitrary", "arbitrary")),
        name="nsa_attention",
    )(proj, cos_t, sin_t, proj, kcmp, vcmp_t, ksx, vs_t, kwx, vw_t, ov_t)


def _layer_norm(r, g, b):
    mu = jnp.mean(r, axis=-1, keepdims=True)
    d = r - mu
    var = jnp.mean(d * d, axis=-1, keepdims=True)
    return d * lax.rsqrt(var + LN_EPS) * g + b


def _sigmoid(x):
    return 1.0 / (1.0 + jnp.exp(-x))


def _merge_kernel(x_ref, pm_ref, at_ref, wpp_ref, wnp_ref, wgp_ref, wga_ref, wo_ref, g_ref, b_ref, o_ref,
                  xb_ref, acc_ref, *, alpha):
    c = pl.program_id(1)

    @pl.when(c == 0)
    def _():
        xb_ref[...] = x_ref[...].astype(BF16)
        acc_ref[...] = jnp.zeros(acc_ref.shape, F32)

    xb = xb_ref[...]
    y_pool = jnp.dot(pm_ref[...], wpp_ref[...], preferred_element_type=F32)
    y_attn = jnp.dot(at_ref[...], wnp_ref[...], preferred_element_type=F32)
    g_pool = _sigmoid(jnp.dot(xb, wgp_ref[...], preferred_element_type=F32))
    g_attn = _sigmoid(jnp.dot(xb, wga_ref[...], preferred_element_type=F32))
    z = g_pool * y_pool + g_attn * y_attn
    acc_ref[...] += jnp.dot(z.astype(BF16), wo_ref[...], preferred_element_type=F32)

    @pl.when(c == pl.num_programs(1) - 1)
    def _():
        o_ref[...] = _layer_norm(alpha * x_ref[...] + acc_ref[...], g_ref[...], b_ref[...])


def _merge_out(x, pm, at, wpp, wnp, wgp, wga, wo, ln_g, ln_b, alpha, tm, ck):
    n, d = x.shape
    pw = pm.shape[1]
    qw = at.shape[1]
    row = lambda i, c: (i, 0)
    colc = lambda i, c: (0, c)
    return pl.pallas_call(
        functools.partial(_merge_kernel, alpha=alpha),
        out_shape=jax.ShapeDtypeStruct((n, d), F32),
        grid=(n // tm, d // ck),
        in_specs=[pl.BlockSpec((tm, d), row), pl.BlockSpec((tm, pw), row), pl.BlockSpec((tm, qw), row),
                  pl.BlockSpec((pw, ck), colc), pl.BlockSpec((qw, ck), colc),
                  pl.BlockSpec((d, ck), colc), pl.BlockSpec((d, ck), colc),
                  pl.BlockSpec((ck, d), lambda i, c: (c, 0)),
                  pl.BlockSpec((1, d), lambda i, c: (0, 0)), pl.BlockSpec((1, d), lambda i, c: (0, 0))],
        out_specs=pl.BlockSpec((tm, d), row),
        scratch_shapes=[pltpu.VMEM((tm, d), BF16), pltpu.VMEM((tm, d), F32)],
        compiler_params=_cparams(("arbitrary", "arbitrary")),
        name="merge_out_ln",
    )(x, pm, at, wpp, wnp, wgp, wga, wo, ln_g, ln_b)


def _router_kernel(h_ref, whi_ref, wlo_ref, b_ref, e_ref, w_ref, *, n_groups, per_group):
    h = h_ref[...]
    h_hi = h.astype(BF16)
    h_lo = (h - h_hi.astype(F32)).astype(BF16)
    logits = (jnp.dot(h_hi, whi_ref[...], preferred_element_type=F32)
              + jnp.dot(h_hi, wlo_ref[...], preferred_element_type=F32)
              + jnp.dot(h_lo, whi_ref[...], preferred_element_type=F32)) + b_ref[...]
    lane = lax.broadcasted_iota(I32, logits.shape, 1)
    far = LANES

    def first_argmax(v, vmax):
        return jnp.min(jnp.where(v == vmax, lane, far), axis=-1, keepdims=True)

    gl = jnp.where(lane < n_groups, logits, NEG)
    gmax = jnp.max(gl, axis=-1, keepdims=True)
    grp = first_argmax(gl, gmax)
    gsum = jnp.sum(jnp.where(lane < n_groups, jnp.exp(gl - gmax), 0.0), axis=-1, keepdims=True)
    g_gate = 1.0 / gsum
    lo = n_groups + grp * per_group
    el = jnp.where((lane >= lo) & (lane < lo + per_group), logits, NEG)
    v1 = jnp.max(el, axis=-1, keepdims=True)
    i1 = first_argmax(el, v1)
    el2 = jnp.where(lane == i1, NEG, el)
    v2 = jnp.max(el2, axis=-1, keepdims=True)
    i2 = first_argmax(el2, v2)
    e21 = jnp.exp(v2 - v1)
    w1 = g_gate / (1.0 + e21)
    w2 = g_gate * e21 / (1.0 + e21)
    e_ref[...] = jnp.where(lane == 0, i1 - n_groups, jnp.where(lane == 1, i2 - n_groups, 0))
    w_ref[...] = jnp.where(lane == 0, w1, jnp.where(lane == 1, w2, 0.0))


def _router(h, w_hi, w_lo, bias, n_groups, per_group, tm):
    n, d = h.shape
    return pl.pallas_call(
        functools.partial(_router_kernel, n_groups=n_groups, per_group=per_group),
        out_shape=(jax.ShapeDtypeStruct((n, LANES), I32), jax.ShapeDtypeStruct((n, LANES), F32)),
        grid=(n // tm,),
        in_specs=[pl.BlockSpec((tm, d), lambda i: (i, 0)),
                  pl.BlockSpec((d, LANES), lambda i: (0, 0)), pl.BlockSpec((d, LANES), lambda i: (0, 0)),
                  pl.BlockSpec((1, LANES), lambda i: (0, 0))],
        out_specs=(pl.BlockSpec((tm, LANES), lambda i: (i, 0)), pl.BlockSpec((tm, LANES), lambda i: (i, 0))),
        compiler_params=_cparams(("arbitrary",)),
        name="moe_router",
    )(h, w_hi, w_lo, bias)


def _expert_kernel(blk0_ref, row_tok_ref, h_hbm, wg_ref, wu_ref, wd_ref, y_hbm,
                   xbuf, ybuf, wgb, wub, wdb, gsem, ysem):
    e = pl.program_id(0)
    n_exp = pl.num_programs(0)
    rows = ybuf.shape[1]
    b0 = blk0_ref[e]
    nb = blk0_ref[e + 1] - b0
    n_used = blk0_ref[n_exp]

    def row_copy(blk, slot, t, u):
        tok = row_tok_ref[blk * rows + t * SUBLANES + u]
        src = h_hbm.at[lax.shift_right_logical(tok, 3), pl.ds(tok & (SUBLANES - 1), 1)]
        return pltpu.make_async_copy(src, xbuf.at[slot, t, pl.ds(u, 1)], gsem.at[slot])

    def start_gather(blk, slot):
        def body(t, c):
            for u in range(SUBLANES):
                row_copy(blk, slot, t, u).start()
            return c
        lax.fori_loop(0, rows // SUBLANES, body, 0)

    def wait_gather(blk, slot):
        def body(t, c):
            for u in range(SUBLANES):
                row_copy(blk, slot, t, u).wait()
            return c
        lax.fori_loop(0, rows // SUBLANES, body, 0)

    def y_copy(blk, slot):
        return pltpu.make_async_copy(ybuf.at[slot], y_hbm.at[pl.ds(blk * rows, rows)], ysem.at[slot])

    @pl.when((e == 0) & (n_used > 0))
    def _():
        start_gather(0, 0)

    @pl.when(nb > 0)
    def _():
        wgb[...] = wg_ref[0].astype(BF16)
        wub[...] = wu_ref[0].astype(BF16)
        wdb[...] = wd_ref[0].astype(BF16)

    def block(j, carry):
        b = b0 + j
        slot = b % 2

        wait_gather(b, slot)
        x = xbuf[slot].reshape(rows, xbuf.shape[3]).astype(BF16)
        hgate = jnp.dot(x, wgb[...], preferred_element_type=F32)
        hup = jnp.dot(x, wub[...], preferred_element_type=F32)
        nxt = jnp.minimum(b + 1, n_used - 1)
        for t in range(rows // SUBLANES):
            for u in range(SUBLANES):
                row_copy(nxt, 1 - slot, t, u).start()
        a = (hgate * _sigmoid(hgate) * hup).astype(BF16)
        y = jnp.dot(a, wdb[...], preferred_element_type=F32)

        @pl.when(b >= 2)
        def _():
            y_copy(b - 2, slot).wait()

        ybuf[slot] = y
        y_copy(b, slot).start()
        return carry

    lax.fori_loop(0, nb, block, 0)

    @pl.when((e == n_exp - 1) & (n_used > 0))
    def _():
        wait_gather(n_used - 1, n_used % 2)

    @pl.when(e == n_exp - 1)
    def _():
        for back in (2, 1):
            @pl.when(n_used >= back)
            def _(back=back):
                y_copy(n_used - back, (n_used - back) % 2).wait()

        n_blk = y_hbm.shape[0] // rows
        ybuf[0] = jnp.zeros(ybuf.shape[1:], ybuf.dtype)

        def zero_start(blk, c):
            y_copy(blk, 0).start()
            return c

        def zero_wait(blk, c):
            y_copy(blk, 0).wait()
            return c

        lax.fori_loop(n_used, n_blk, zero_start, 0)
        lax.fori_loop(n_used, n_blk, zero_wait, 0)


def _experts(blk0, row_tok, h, w_gate, w_up, w_down, n_blk):
    n_exp = w_gate.shape[0]
    rows = EXPERT_ROWS
    d = h.shape[1]
    hid = w_gate.shape[2]
    grid_spec = pltpu.PrefetchScalarGridSpec(
        num_scalar_prefetch=2,
        grid=(n_exp,),
        in_specs=[pl.BlockSpec(memory_space=pl.ANY),
                  pl.BlockSpec((1, d, hid), lambda e, b0, rt: (e, 0, 0)),
                  pl.BlockSpec((1, d, hid), lambda e, b0, rt: (e, 0, 0)),
                  pl.BlockSpec((1, hid, d), lambda e, b0, rt: (e, 0, 0))],
        out_specs=pl.BlockSpec(memory_space=pl.ANY),
        scratch_shapes=[pltpu.VMEM((2, rows // SUBLANES, SUBLANES, d), F32), pltpu.VMEM((2, rows, d), F32),
                        pltpu.VMEM((d, hid), BF16), pltpu.VMEM((d, hid), BF16), pltpu.VMEM((hid, d), BF16),
                        pltpu.SemaphoreType.DMA((2,)), pltpu.SemaphoreType.DMA((2,))],
    )
    return pl.pallas_call(
        _expert_kernel,
        out_shape=jax.ShapeDtypeStruct((n_blk * rows, d), F32),
        grid_spec=grid_spec,
        compiler_params=_cparams(("arbitrary",), row_dma=True),
        name="moe_experts",
    )(blk0, row_tok, h.reshape(h.shape[0] // SUBLANES, SUBLANES, d), w_gate, w_up, w_down)


def _combine_kernel(dest_ref, h_ref, w_ref, y_hbm, g_ref, b_ref, o_ref, ybuf, sem, *, alpha):
    i = pl.program_id(0)
    n_steps = pl.num_programs(0)
    tm = h_ref.shape[0]
    slot = i % 2

    def row_copy(step_, slot_, t, u, k):
        row = dest_ref[(step_ * tm + t * SUBLANES + u) * TOP_K + k]
        src = y_hbm.at[lax.shift_right_logical(row, 3), pl.ds(row & (SUBLANES - 1), 1)]
        return pltpu.make_async_copy(src, ybuf.at[slot_, k, t, pl.ds(u, 1)], sem.at[slot_])

    def start_gather(step_, slot_):
        def body(t, c):
            for u in range(SUBLANES):
                for k in range(TOP_K):
                    row_copy(step_, slot_, t, u, k).start()
            return c
        lax.fori_loop(0, tm // SUBLANES, body, 0)

    def wait_gather(step_, slot_):
        def body(t, c):
            for u in range(SUBLANES):
                for k in range(TOP_K):
                    row_copy(step_, slot_, t, u, k).wait()
            return c
        lax.fori_loop(0, tm // SUBLANES, body, 0)

    @pl.when(i == 0)
    def _():
        start_gather(0, 0)

    wait_gather(i, slot)
    w = w_ref[...]
    y = ybuf[slot, 0].reshape(h_ref.shape) * w[:, 0:1]
    for k in range(1, TOP_K):
        y = y + ybuf[slot, k].reshape(h_ref.shape) * w[:, k:k + 1]
    nxt = jnp.minimum(i + 1, n_steps - 1)
    for t in range(tm // SUBLANES):
        for u in range(SUBLANES):
            for k in range(TOP_K):
                row_copy(nxt, 1 - slot, t, u, k).start()
    o_ref[...] = _layer_norm(alpha * h_ref[...] + y, g_ref[...], b_ref[...])

    @pl.when(i == n_steps - 1)
    def _():
        wait_gather(i, 1 - slot)


def _combine(dest, h, wts, ybuf, ln_g, ln_b, alpha, tm):
    n, d = h.shape
    grid_spec = pltpu.PrefetchScalarGridSpec(
        num_scalar_prefetch=1,
        grid=(n // tm,),
        in_specs=[pl.BlockSpec((tm, d), lambda i, ds: (i, 0)),
                  pl.BlockSpec((tm, LANES), lambda i, ds: (i, 0)),
                  pl.BlockSpec(memory_space=pl.ANY),
                  pl.BlockSpec((1, d), lambda i, ds: (0, 0)), pl.BlockSpec((1, d), lambda i, ds: (0, 0))],
        out_specs=pl.BlockSpec((tm, d), lambda i, ds: (i, 0)),
        scratch_shapes=[pltpu.VMEM((2, TOP_K, tm // SUBLANES, SUBLANES, d), F32), pltpu.SemaphoreType.DMA((2,))],
    )
    return pl.pallas_call(
        functools.partial(_combine_kernel, alpha=alpha),
        out_shape=jax.ShapeDtypeStruct((n, d), F32),
        grid_spec=grid_spec,
        compiler_params=_cparams(("arbitrary",), row_dma=True),
        name="moe_combine_ln",
    )(dest, h, wts, ybuf.reshape(ybuf.shape[0] // SUBLANES, SUBLANES, d), ln_g, ln_b)


def _rope_tables(s):
    inv_freq = ROPE_THETA ** (-2.0 * jnp.arange(HALF_DIM, dtype=F32) / HEAD_DIM)

    def tables(pos):
        ang = pos.astype(F32)[:, None] * inv_freq[None, :]
        return jnp.cos(ang), jnp.sin(ang)

    cos, sin = tables(jnp.arange(s))
    n_rows = s // CMP_STRIDE
    c_end = jnp.arange(n_rows) * CMP_STRIDE + (CMP_BLOCK - 1)
    cos_c, sin_c = tables(c_end)
    full = lambda c: jnp.concatenate([c, c], axis=1)
    signed = lambda sn: jnp.concatenate([-sn, sn], axis=1)
    return cos.T, sin.T, full(cos), signed(sin), full(cos_c), signed(sin_c)


def _overlap_t(s):
    n_rows = s // CMP_STRIDE
    n_cmp = (s - CMP_BLOCK) // CMP_STRIDE + 1
    n_sel = s // SEL_BLOCK
    c_start = np.arange(n_rows) * CMP_STRIDE
    s_start = np.arange(n_sel) * SEL_BLOCK
    ov = ((c_start[None, :] + CMP_BLOCK - 1 >= s_start[:, None])
          & (c_start[None, :] <= s_start[:, None] + SEL_BLOCK - 1)
          & (np.arange(n_rows)[None, :] < n_cmp))
    return jnp.asarray(ov.astype(np.float32), dtype=BF16)


def _dispatch_plan(eid, n_experts):
    n = eid.shape[0]
    m = n * TOP_K
    rows_per = EXPERT_ROWS
    flat_e = eid.reshape(-1)
    onehot = (flat_e[:, None] == jnp.arange(n_experts)[None, :]).astype(I32)
    pos = jnp.take_along_axis(jnp.cumsum(onehot, axis=0), flat_e[:, None], axis=1)[:, 0] - 1
    sizes = jnp.sum(onehot, axis=0)
    padded = (sizes + rows_per - 1) // rows_per * rows_per
    ends = jnp.cumsum(padded)
    dest = (ends - padded)[flat_e] + pos
    n_blk = -(-(m + n_experts * (rows_per - 1)) // rows_per)
    row_tok = jnp.zeros((n_blk * rows_per,), I32).at[dest].set(jnp.arange(m, dtype=I32) // TOP_K)
    blk0 = jnp.concatenate([jnp.zeros((1,), I32), ends.astype(I32) // rows_per])
    return dest.astype(I32), row_tok, blk0, n_blk


def kernel(x, w_in, pool_mix, pool_scale, w_pool_proj, w_nsa_proj, cmp_pos_k, cmp_pos_v, cmp_k_w1, cmp_k_w2,
           cmp_v_w1, cmp_v_w2, w_out, ln1_g, ln1_b, router_group_w, router_group_b, router_expert_w,
           router_expert_b, w_gate, w_up, w_down, ln2_g, ln2_b):
    b, s, d = x.shape
    n = b * s
    depth = w_in.shape[0]
    alpha = (2.0 * depth) ** 0.25
    pw = pool_mix.shape[1] * pool_mix.shape[2]
    qw = w_nsa_proj.shape[1]
    kvw = N_KV_GROUPS * HEAD_DIM
    n_groups, _, per_group = router_expert_w.shape[1:]
    n_experts = n_groups * per_group
    gate_w = 3 * (qw // HEAD_DIM)
    assert s % ATTN_TILE == 0 and WINDOW % ATTN_TILE == 0 and ATTN_TILE % SEL_BLOCK == 0
    assert gate_w <= LANES and n_groups + n_experts <= LANES and s // SEL_BLOCK <= HEAD_DIM
    c_q = pw
    c_kv = pw + qw
    c_gate = c_kv + 6 * kvw
    c_merge = c_gate + gate_w
    tn = 768
    width_a = -(-(c_gate + LANES) // tn) * tn
    assert c_gate % LANES == 0

    cos_t, sin_t, cos_f, sin_s, cos_c, sin_c = _rope_tables(s)
    ov_t = _overlap_t(s)
    assert c_kv % kvw == 0

    h = x.reshape(n, d)
    for l in range(depth):
        wl = w_in[l]
        w_a = jnp.pad(wl[:, :c_merge], ((0, 0), (0, width_a - c_merge))).astype(BF16)
        w_gp = wl[:, c_merge:c_merge + d].astype(BF16)
        w_ga = wl[:, c_merge + d:].astype(BF16)

        proj = _project(h, w_a, _tile(n, 1024), tn)

        mixed = _pool_mixer(proj, pool_mix[l].astype(BF16), pool_scale[l].reshape(1, pw), b, s, pw, _tile(s, 512))

        kv_blk = c_kv // kvw
        kcmp, vcmp_t = _compress(
            proj, kv_blk, b, s,
            cmp_pos_k[l].reshape(1, CMP_BLOCK * HEAD_DIM), cmp_pos_v[l].reshape(1, CMP_BLOCK * HEAD_DIM),
            cmp_k_w1[l].astype(BF16), cmp_k_w2[l].astype(BF16), cmp_v_w1[l].astype(BF16),
            cmp_v_w2[l].T.astype(BF16), cos_c, sin_c)
        ksx, vs_t, kwx, vw_t = _kv_prep(proj, kv_blk, b, s, cos_f, sin_s)

        attn = _attention(proj, cos_t, sin_t, kcmp, vcmp_t, ksx, vs_t, kwx, vw_t, ov_t,
                          b=b, s=s, pw=pw, qw=qw, gate_col=c_gate)

        h = _merge_out(h, mixed, attn, w_pool_proj[l].astype(BF16), w_nsa_proj[l].astype(BF16), w_gp, w_ga,
                       w_out[l].astype(BF16), ln1_g[l].reshape(1, d), ln1_b[l].reshape(1, d), alpha,
                       _tile(n, 512), _tile(d, 512))

        w_r = jnp.concatenate([router_group_w[l], router_expert_w[l].transpose(1, 0, 2).reshape(d, n_experts)], axis=1)
        w_r = jnp.pad(w_r, ((0, 0), (0, LANES - w_r.shape[1])))
        w_r_hi = w_r.astype(BF16)
        w_r_lo = (w_r - w_r_hi.astype(F32)).astype(BF16)
        b_r = jnp.pad(jnp.concatenate([router_group_b[l], router_expert_b[l].reshape(-1)]),
                      (0, LANES - n_groups - n_experts)).reshape(1, LANES)
        eid_l, wts_l = _router(h, w_r_hi, w_r_lo, b_r, n_groups, per_group, _tile(n, 512))
        dest, row_tok, blk0, n_blk = _dispatch_plan(eid_l[:, :TOP_K], n_experts)

        ybuf = _experts(blk0, row_tok, h, w_gate[l], w_up[l], w_down[l], n_blk)
        h = _combine(dest, h, wts_l, ybuf, ln2_g[l].reshape(1, d), ln2_b[l].reshape(1, d), alpha, _tile(n, 256))
    return h.reshape(b, s, d)
```

```python
import functools
import math

import numpy as np
import jax
import jax.numpy as jnp
from jax import lax
from jax.experimental import pallas as pl
from jax.experimental.pallas import tpu as pltpu

F32 = jnp.float32
BF16 = jnp.bfloat16
I32 = jnp.int32

POOL_WINDOWS = (2, 4, 8, 16)
POOL_GROUPS = 4
POOL_HALO = 16
HEAD_DIM = 64
HALF_DIM = HEAD_DIM // 2
V_ROWS = HEAD_DIM + 16
N_KV_GROUPS = 4
CMP_BLOCK = 32
CMP_STRIDE = 16
SEL_BLOCK = 64
SEL_TOP_N = 16
WINDOW = 512
ROPE_THETA = 10000.0
FORCE_SCORE = 1e6
TOP_K = 2
LN_EPS = 1e-5
NEG = -1e30

LANES = 128
SUBLANES = 8
VMEM_BYTES_V7X = 64 * 1024 * 1024
VMEM_LIMIT = VMEM_BYTES_V7X - 8 * 1024 * 1024

PROJ_COL_TILE = 1280
ATTN_TILE = 256
SEL_UNROLL = 4
EXPERT_ROWS = 256


def _cparams(sem, vmem=VMEM_LIMIT):
    return pltpu.CompilerParams(dimension_semantics=sem, vmem_limit_bytes=vmem)


def _tile(n, pref):
    t = min(n, pref)
    while n % t:
        t //= 2
    return t


def _proj_kernel(x_ref, w_ref, o_ref, xb_ref):
    @pl.when(pl.program_id(1) == 0)
    def _():
        xb_ref[...] = x_ref[...].astype(BF16)

    o_ref[...] = jnp.dot(xb_ref[...], w_ref[...], preferred_element_type=F32)


def _project(x, w, tm, tn):
    n, d = x.shape
    cols = w.shape[1]
    return pl.pallas_call(
        _proj_kernel,
        out_shape=jax.ShapeDtypeStruct((n, cols), F32),
        grid=(n // tm, cols // tn),
        in_specs=[pl.BlockSpec((tm, d), lambda i, j: (i, 0)),
                  pl.BlockSpec((d, tn), lambda i, j: (0, j))],
        out_specs=pl.BlockSpec((tm, tn), lambda i, j: (i, j)),
        scratch_shapes=[pltpu.VMEM((tm, d), BF16)],
        compiler_params=_cparams(("arbitrary", "arbitrary")),
        name="in_proj",
    )(x, w)


def _pool_kernel(u_ref, halo_ref, mix_ref, scale_ref, o_ref, ext_ref):
    i = pl.program_id(1)
    ts = u_ref.shape[0]
    gd = mix_ref.shape[1]
    ext_ref[POOL_HALO:, :] = u_ref[...]
    ext_ref[:POOL_HALO, :] = jnp.where(i == 0, 0.0, halo_ref[...])
    t = i * ts + lax.broadcasted_iota(I32, (ts, gd), 0)
    for g, w in enumerate(POOL_WINDOWS):
        cols = slice(g * gd, (g + 1) * gd)
        s = ext_ref[:, cols]
        k = 1
        while k < w:
            s = s + pltpu.roll(s, k, axis=0)
            k *= 2
        cnt = jnp.minimum(t + 1, w).astype(F32)
        pooled = s[POOL_HALO:, :] / cnt - u_ref[:, cols]
        mixed = jnp.dot(pooled.astype(BF16), mix_ref[g], preferred_element_type=F32)
        o_ref[:, cols] = (mixed * scale_ref[:, cols]).astype(BF16)


def _pool_mixer(proj, pool_mix_b, pool_scale, b, s, pw, ts):
    ns = s // ts
    hb = ts // POOL_HALO
    return pl.pallas_call(
        _pool_kernel,
        out_shape=jax.ShapeDtypeStruct((b * s, pw), BF16),
        grid=(b, ns),
        in_specs=[pl.BlockSpec((ts, pw), lambda bi, i: (bi * ns + i, 0)),
                  pl.BlockSpec((POOL_HALO, pw), lambda bi, i: (jnp.maximum((bi * ns + i) * hb - 1, 0), 0)),
                  pl.BlockSpec(pool_mix_b.shape, lambda bi, i: (0, 0, 0)),
                  pl.BlockSpec((1, pw), lambda bi, i: (0, 0))],
        out_specs=pl.BlockSpec((ts, pw), lambda bi, i: (bi * ns + i, 0)),
        scratch_shapes=[pltpu.VMEM((POOL_HALO + ts, pw), F32)],
        compiler_params=_cparams(("arbitrary", "arbitrary")),
        name="pool_mixer",
    )(proj, proj, pool_mix_b, pool_scale)


def _gelu_tanh(x):
    return 0.5 * x * (1.0 + jnp.tanh(math.sqrt(2.0 / math.pi) * (x + 0.044715 * (x * x * x))))


def _swap_halves(x):
    return jnp.concatenate([x[:, HALF_DIM:], x[:, :HALF_DIM]], axis=1)


def _cmp_kernel(kc0_ref, kc1_ref, vc0_ref, vc1_ref, pk_ref, pv_ref, kw1_ref, kw2_ref, vw1_ref, vw2t_ref,
                cos_ref, sin_ref, ko_ref, vo_ref, r_s):
    kc_refs = (kc0_ref, kc1_ref)
    vc_refs = (vc0_ref, vc1_ref)
    s = kc0_ref.shape[0]
    nrow = s // CMP_STRIDE
    half = CMP_STRIDE * HEAD_DIM

    def hidden(p_ref, w1_ref, g):
        r = r_s[g]
        r_next = pltpu.roll(r, nrow - 1, axis=0)
        a = (r + p_ref[:, :half]).astype(BF16)
        bb = (r_next + p_ref[:, half:]).astype(BF16)
        h = (jnp.dot(a, w1_ref[:half, :], preferred_element_type=F32)
             + jnp.dot(bb, w1_ref[half:, :], preferred_element_type=F32))
        return _gelu_tanh(h).astype(BF16)

    gpl = LANES // HEAD_DIM

    def regroup(src_refs):
        for t in range(CMP_STRIDE):
            for j, src_ref in enumerate(src_refs):
                rows = src_ref[pl.ds(t, nrow, stride=CMP_STRIDE), :]
                for gg in range(gpl):
                    r_s[j * gpl + gg, :, t * HEAD_DIM:(t + 1) * HEAD_DIM] = rows[:, gg * HEAD_DIM:(gg + 1) * HEAD_DIM]

    regroup(kc_refs)
    for g in range(N_KV_GROUPS):
        k = jnp.dot(hidden(pk_ref, kw1_ref, g), kw2_ref[...], preferred_element_type=F32)
        k = k * cos_ref[...] + _swap_halves(k) * sin_ref[...]
        ko_ref[0, g] = k.astype(BF16)
    regroup(vc_refs)
    for g in range(N_KV_GROUPS):
        hv = hidden(pv_ref, vw1_ref, g)
        vt = lax.dot_general(vw2t_ref[...], hv, (((1,), (1,)), ((), ())), preferred_element_type=F32)
        vo_ref[0, g] = vt.astype(BF16)


def _compress(proj, col_blk, b, s, pk, pv, kw1, kw2, vw1, vw2t, cos_c, sin_c):
    g = N_KV_GROUPS
    kvw = g * HEAD_DIM
    nrow = s // CMP_STRIDE

    def whole(a):
        return pl.BlockSpec(a.shape, lambda bi: (0,) * a.ndim)

    assert kvw == 2 * LANES

    def lane_tile(j):
        return pl.BlockSpec((s, LANES), lambda bi: (bi, col_blk * (kvw // LANES) + j))

    return pl.pallas_call(
        _cmp_kernel,
        out_shape=(jax.ShapeDtypeStruct((b, g, nrow, HEAD_DIM), BF16),
                   jax.ShapeDtypeStruct((b, g, HEAD_DIM, nrow), BF16)),
        grid=(b,),
        in_specs=[lane_tile(0), lane_tile(1), lane_tile(2), lane_tile(3),
                  whole(pk), whole(pv), whole(kw1), whole(kw2), whole(vw1), whole(vw2t),
                  whole(cos_c), whole(sin_c)],
        out_specs=(pl.BlockSpec((1, g, nrow, HEAD_DIM), lambda bi: (bi, 0, 0, 0)),
                   pl.BlockSpec((1, g, HEAD_DIM, nrow), lambda bi: (bi, 0, 0, 0))),
        scratch_shapes=[pltpu.VMEM((g, nrow, CMP_STRIDE * HEAD_DIM), F32)],
        compiler_params=_cparams(("arbitrary",)),
        name="compress_kv",
    )(proj, proj, proj, proj, pk, pv, kw1, kw2, vw1, vw2t, cos_c, sin_c)


def _kvprep_kernel(ks_ref, vs_ref, kw_ref, vw_ref, cos_ref, sin_ref, kso_ref, vso_ref, kwo_ref, vwo_ref):
    i = pl.program_id(1)
    tq = ks_ref.shape[0]
    cos = cos_ref[...]
    sin = sin_ref[...]
    blk = (i * tq + lax.broadcasted_iota(I32, (tq, HEAD_DIM), 0)) // SEL_BLOCK
    onehot = jnp.where(blk == lax.broadcasted_iota(I32, (tq, HEAD_DIM), 1), 1.0, 0.0).astype(BF16)
    ks = ks_ref[...]
    kw = kw_ref[...]
    vs_t = vs_ref[...].T
    vw_t = vw_ref[...].T
    for g in range(N_KV_GROUPS):
        cols = slice(g * HEAD_DIM, (g + 1) * HEAD_DIM)
        k = ks[:, cols]
        kso_ref[0, g, 0, :, :HEAD_DIM] = (k * cos + _swap_halves(k) * sin).astype(BF16)
        kso_ref[0, g, 0, :, HEAD_DIM:] = onehot
        k = kw[:, cols]
        kwo_ref[0, g, 0] = (k * cos + _swap_halves(k) * sin).astype(BF16)
        ones = jnp.ones((V_ROWS - HEAD_DIM, tq), BF16)
        vso_ref[0, g, 0, :HEAD_DIM] = vs_t[cols].astype(BF16)
        vso_ref[0, g, 0, HEAD_DIM:] = ones
        vwo_ref[0, g, 0, :HEAD_DIM] = vw_t[cols].astype(BF16)
        vwo_ref[0, g, 0, HEAD_DIM:] = ones


def _kv_prep(proj, col_blk, b, s, cos_f, sin_s):
    g = N_KV_GROUPS
    kvw = g * HEAD_DIM
    tq = ATTN_TILE
    nq = s // tq

    def col(j):
        return pl.BlockSpec((tq, kvw), lambda bi, i: (bi * nq + i, col_blk + j))

    tab = pl.BlockSpec((tq, HEAD_DIM), lambda bi, i: (i, 0))

    def out(r, c):
        return (jax.ShapeDtypeStruct((b, g, nq, r, c), BF16),
                pl.BlockSpec((1, g, 1, r, c), lambda bi, i: (bi, 0, i, 0, 0)))

    outs = [out(tq, 2 * HEAD_DIM), out(V_ROWS, tq), out(tq, HEAD_DIM), out(V_ROWS, tq)]
    return pl.pallas_call(
        _kvprep_kernel,
        out_shape=tuple(o[0] for o in outs),
        grid=(b, nq),
        in_specs=[col(2), col(3), col(4), col(5), tab, tab],
        out_specs=tuple(o[1] for o in outs),
        compiler_params=_cparams(("arbitrary", "arbitrary")),
        name="kv_prep",
    )(proj, proj, proj, proj, cos_f, sin_s)


def _attn_kernel(q_ref, cos_ref, sin_ref, gate_ref, kc_ref, vct_ref, ks_ref, vst_ref, kw_ref, vwt_ref, ovt_ref,
                 o_ref, qa_s, g_s, imp_s, m_s, c_s, acc_s, out_s, sc_s, pc_s, psum_s, s_s, p_s,
                 mw_s, cw_s, accw_s, sw_s, pw_s, *, hg, n_heads):
    g = pl.program_id(1)
    i = pl.program_id(2)
    tq = q_ref.shape[0]
    tk = tq
    ncp = kc_ref.shape[2]
    nsel = ovt_ref.shape[0]
    scale = HEAD_DIM ** -0.5 * math.log2(math.e)

    qt = q_ref[...].T
    cos = cos_ref[...]
    sin = sin_ref[...]
    for h in range(hg):
        x1 = qt[h * HEAD_DIM:h * HEAD_DIM + HALF_DIM]
        x2 = qt[h * HEAD_DIM + HALF_DIM:(h + 1) * HEAD_DIM]
        qa_s[h, :HALF_DIM] = ((x1 * cos - x2 * sin) * scale).astype(BF16)
        qa_s[h, HALF_DIM:HEAD_DIM] = ((x2 * cos + x1 * sin) * scale).astype(BF16)
    g_s[...] = gate_ref[...].T

    def gate(branch, h):
        row = g_s[pl.ds(branch * n_heads + g * hg + h, 1), :]
        return 1.0 / (1.0 + jnp.exp(-row))

    chunks = [slice(c * LANES, (c + 1) * LANES) for c in range(tq // LANES)]

    def qk(dst, kx, qrows):
        for h in range(hg):
            dst[h] = jnp.dot(kx, qa_s[h, :qrows], preferred_element_type=F32)

    qk(sc_s, kc_ref[0, 0], HEAD_DIM)
    c_end = lax.broadcasted_iota(I32, (ncp, LANES), 0) * CMP_STRIDE + (CMP_BLOCK - 1)
    for c, cols in enumerate(chunks):
        cmask = c_end <= i * tq + c * LANES + lax.broadcasted_iota(I32, (ncp, LANES), 1)
        for h in range(hg):
            sm = jnp.where(cmask, sc_s[h, :, cols], NEG)
            m = jnp.max(sm, axis=0, keepdims=True)
            p = jnp.where(cmask, jnp.exp2(sm - m), 0.0)
            den = jnp.maximum(jnp.sum(p, axis=0, keepdims=True), 1e-30)
            pn = p / den
            psum_s[:, cols] = pn if h == 0 else psum_s[:, cols] + pn
            pc_s[h, :, cols] = pn.astype(BF16)
    for h in range(hg):
        oc = jnp.dot(vct_ref[0, 0], pc_s[h], preferred_element_type=F32)
        out_s[h * HEAD_DIM:(h + 1) * HEAD_DIM, :] = oc * gate(0, h)

    def reset(state):
        m_r, _, acc_r, _ = state
        m_r[...] = jnp.full(m_r.shape, NEG, F32)
        acc_r[...] = jnp.zeros(acc_r.shape, F32)

    krow = lax.broadcasted_iota(I32, (tk, LANES), 0)
    lane = lax.broadcasted_iota(I32, (tk, LANES), 1)

    def softmax_pv(state, src, vt, kind, skip=None):
        m_r, c_r, acc_r, p_r = state
        off = None if skip is None else jnp.where(skip, tk, 0)
        for c, cols in enumerate(chunks):
            qcol = lane + c * LANES
            if kind == 'causal':
                mask = krow <= qcol
            elif kind == 'older':
                mask = krow > (qcol if off is None else qcol + off)
            else:
                mask = None if off is None else krow >= off
            for h in range(hg):
                s = src[h, :, cols]
                if mask is not None:
                    s = jnp.where(mask, s, NEG)
                m_old = m_r[h, :, cols]
                m_new = jnp.maximum(m_old, jnp.max(s, axis=0, keepdims=True))
                m_r[h, :, cols] = m_new
                c_r[h, :, cols] = jnp.exp2(m_old - m_new)
                p_r[h, :, cols] = jnp.exp2(s - m_new).astype(BF16)
        for h in range(hg):
            acc_r[h] = acc_r[h] * c_r[h] + jnp.dot(vt, p_r[h], preferred_element_type=F32)

    def finish(state, branch):
        acc_r = state[2]
        for h in range(hg):
            rows = slice(h * HEAD_DIM, (h + 1) * HEAD_DIM)
            den = acc_r[h, HEAD_DIM:HEAD_DIM + 1, :]
            out_s[rows, :] = out_s[rows, :] + acc_r[h, :HEAD_DIM, :] * (gate(branch, h) / den)

    psum = psum_s[...]
    p_hi = psum.astype(BF16)
    p_lo = (psum - p_hi.astype(F32)).astype(BF16)
    imp = (jnp.dot(ovt_ref[...], p_hi, preferred_element_type=F32)
           + jnp.dot(ovt_ref[...], p_lo, preferred_element_type=F32))
    jb = lax.broadcasted_iota(I32, (nsel, tq), 0)
    cur = (i * tq + lax.broadcasted_iota(I32, (nsel, tq), 1)) // SEL_BLOCK
    forced = (jb == 0) | (jb == cur) | (jb == cur - 1)
    imp = jnp.where(forced, FORCE_SCORE, jnp.where(jb > cur, -FORCE_SCORE, imp))
    imp_s[...] = imp
    nrb = nsel // SUBLANES
    blocks = [imp[r * SUBLANES:(r + 1) * SUBLANES] for r in range(nrb)]
    counts = [jnp.zeros((SUBLANES, tq), F32) for _ in range(nrb)]
    sub = lax.broadcasted_iota(I32, (SUBLANES, tq), 0)
    for jp in range(nsel):
        row = jnp.broadcast_to(imp_s[jp:jp + 1, :], (SUBLANES, tq))
        for r in range(nrb):
            if r * SUBLANES > jp:
                inc = jnp.where(row >= blocks[r], 1.0, 0.0)
            elif r * SUBLANES + SUBLANES - 1 <= jp:
                inc = jnp.where(row > blocks[r], 1.0, 0.0)
            else:
                tie = jnp.where(sub > jp - r * SUBLANES, 1.0, 0.0)
                inc = jnp.where(row > blocks[r], 1.0, jnp.where(row >= blocks[r], tie, 0.0))
            counts[r] = counts[r] + inc
    top_n = min(SEL_TOP_N, nsel)
    bias = jnp.concatenate([jnp.where(cnt < top_n, 0.0, NEG) for cnt in counts], axis=0).astype(BF16)
    for h in range(hg):
        qa_s[h, HEAD_DIM:HEAD_DIM + nsel] = bias
        if nsel < HEAD_DIM:
            qa_s[h, HEAD_DIM + nsel:] = jnp.zeros((HEAD_DIM - nsel, tq), BF16)

    s_a, s_b = s_s.at[0], s_s.at[1]
    sel_q = 2 * HEAD_DIM

    sel = (m_s, c_s, acc_s, p_s)
    reset(sel)
    qk(s_a, ks_ref[0, 0, 0], sel_q)

    def sel_run(first, count, ends_on_diagonal):
        bufs = (s_a, s_b)
        for j in range(count):
            if j + 1 < count or not ends_on_diagonal:
                qk(bufs[(j + 1) % 2], ks_ref[0, 0, first + j + 1], sel_q)
            kind = 'causal' if (ends_on_diagonal and j == count - 1) else None
            softmax_pv(sel, bufs[j % 2], vst_ref[0, 0, first + j], kind)

    def sel_group(gi, carry):
        sel_run(gi * SEL_UNROLL, SEL_UNROLL, False)
        return carry

    lax.fori_loop(0, i // SEL_UNROLL, sel_group, 0)
    for rem in range(SEL_UNROLL):
        @pl.when(i % SEL_UNROLL == rem)
        def _(rem=rem):
            sel_run(i - rem, rem + 1, True)

    finish(sel, 1)

    win = (mw_s, cw_s, accw_s, pw_s)
    reset(win)
    nwt = WINDOW // tk
    assert nwt == 2
    t_old = jnp.maximum(i - 2, 0)
    t_mid = jnp.maximum(i - 1, 0)
    qk(sw_s.at[0], kw_ref[0, 0, t_old], HEAD_DIM)
    qk(sw_s.at[1], kw_ref[0, 0, t_mid], HEAD_DIM)
    qk(sw_s.at[2], kw_ref[0, 0, i], HEAD_DIM)
    softmax_pv(win, sw_s.at[0], vwt_ref[0, 0, t_old], 'older', skip=i < 2)
    softmax_pv(win, sw_s.at[1], vwt_ref[0, 0, t_mid], None, skip=i < 1)
    softmax_pv(win, sw_s.at[2], vwt_ref[0, 0, i], 'causal')
    finish(win, 2)

    o_ref[...] = out_s[...].T.astype(o_ref.dtype)


def _attention(proj, cos_t, sin_t, kcmp, vcmp_t, ksx, vs_t, kwx, vw_t, ov_t, *, b, s, pw, qw, gate_col):
    tq = ATTN_TILE
    nq = s // tq
    g = N_KV_GROUPS
    gw = qw // g
    hg = gw // HEAD_DIM
    n_heads = qw // HEAD_DIM
    ncp = kcmp.shape[2]
    nsel = ov_t.shape[0]
    q_blk0 = pw // gw
    kern = functools.partial(_attn_kernel, hg=hg, n_heads=n_heads)
    kv5 = lambda bi, gi, i: (bi, gi, 0, 0, 0)
    kv4 = lambda bi, gi, i: (bi, gi, 0, 0)
    return pl.pallas_call(
        kern,
        out_shape=jax.ShapeDtypeStruct((b * s, qw), BF16),
        grid=(b, g, nq),
        in_specs=[pl.BlockSpec((tq, gw), lambda bi, gi, i: (bi * nq + i, q_blk0 + gi)),
                  pl.BlockSpec((HALF_DIM, tq), lambda bi, gi, i: (0, i)),
                  pl.BlockSpec((HALF_DIM, tq), lambda bi, gi, i: (0, i)),
                  pl.BlockSpec((tq, LANES), lambda bi, gi, i: (bi * nq + i, gate_col // LANES)),
                  pl.BlockSpec((1, 1, ncp, HEAD_DIM), kv4),
                  pl.BlockSpec((1, 1, HEAD_DIM, ncp), kv4),
                  pl.BlockSpec((1, 1, nq, tq, 2 * HEAD_DIM), kv5),
                  pl.BlockSpec((1, 1, nq, V_ROWS, tq), kv5),
                  pl.BlockSpec((1, 1, nq, tq, HEAD_DIM), kv5),
                  pl.BlockSpec((1, 1, nq, V_ROWS, tq), kv5),
                  pl.BlockSpec(ov_t.shape, lambda bi, gi, i: (0, 0))],
        out_specs=pl.BlockSpec((tq, gw), lambda bi, gi, i: (bi * nq + i, gi)),
        scratch_shapes=[pltpu.VMEM((hg, 2 * HEAD_DIM, tq), BF16),
                        pltpu.VMEM((LANES, tq), F32),
                        pltpu.VMEM((nsel, tq), F32),
                        pltpu.VMEM((hg, 1, tq), F32),
                        pltpu.VMEM((hg, 1, tq), F32),
                        pltpu.VMEM((hg, V_ROWS, tq), F32),
                        pltpu.VMEM((gw, tq), F32),
                        pltpu.VMEM((hg, ncp, tq), F32),
                        pltpu.VMEM((hg, ncp, tq), BF16),
                        pltpu.VMEM((ncp, tq), F32),
                        pltpu.VMEM((2, hg, tq, tq), F32),
                        pltpu.VMEM((hg, tq, tq), BF16),
                        pltpu.VMEM((hg, 1, tq), F32),
                        pltpu.VMEM((hg, 1, tq), F32),
                        pltpu.VMEM((hg, V_ROWS, tq), F32),
                        pltpu.VMEM((3, hg, tq, tq), F32),
                        pltpu.VMEM((hg, tq, tq), BF16)],
        compiler_params=_cparams(("arbitrary", "arbitrary", "arbitrary")),
        name="nsa_attention",
    )(proj, cos_t, sin_t, proj, kcmp, vcmp_t, ksx, vs_t, kwx, vw_t, ov_t)


def _layer_norm(r, g, b):
    mu = jnp.mean(r, axis=-1, keepdims=True)
    d = r - mu
    var = jnp.mean(d * d, axis=-1, keepdims=True)
    return d * lax.rsqrt(var + LN_EPS) * g + b


def _sigmoid(x):
    return 1.0 / (1.0 + jnp.exp(-x))


def _merge_kernel(x_ref, pm_ref, at_ref, wpp_ref, wnp_ref, wgp_ref, wga_ref, wo_ref, g_ref, b_ref, o_ref,
                  xb_ref, acc_ref, *, alpha):
    c = pl.program_id(1)

    @pl.when(c == 0)
    def _():
        xb_ref[...] = x_ref[...].astype(BF16)
        acc_ref[...] = jnp.zeros(acc_ref.shape, F32)

    xb = xb_ref[...]
    y_pool = jnp.dot(pm_ref[...], wpp_ref[...], preferred_element_type=F32)
    y_attn = jnp.dot(at_ref[...], wnp_ref[...], preferred_element_type=F32)
    g_pool = _sigmoid(jnp.dot(xb, wgp_ref[...], preferred_element_type=F32))
    g_attn = _sigmoid(jnp.dot(xb, wga_ref[...], preferred_element_type=F32))
    z = g_pool * y_pool + g_attn * y_attn
    acc_ref[...] += jnp.dot(z.astype(BF16), wo_ref[...], preferred_element_type=F32)

    @pl.when(c == pl.num_programs(1) - 1)
    def _():
        o_ref[...] = _layer_norm(alpha * x_ref[...] + acc_ref[...], g_ref[...], b_ref[...])


def _merge_out(x, pm, at, wpp, wnp, wgp, wga, wo, ln_g, ln_b, alpha, tm, ck):
    n, d = x.shape
    pw = pm.shape[1]
    qw = at.shape[1]
    row = lambda i, c: (i, 0)
    colc = lambda i, c: (0, c)
    return pl.pallas_call(
        functools.partial(_merge_kernel, alpha=alpha),
        out_shape=jax.ShapeDtypeStruct((n, d), F32),
        grid=(n // tm, d // ck),
        in_specs=[pl.BlockSpec((tm, d), row), pl.BlockSpec((tm, pw), row), pl.BlockSpec((tm, qw), row),
                  pl.BlockSpec((pw, ck), colc), pl.BlockSpec((qw, ck), colc),
                  pl.BlockSpec((d, ck), colc), pl.BlockSpec((d, ck), colc),
                  pl.BlockSpec((ck, d), lambda i, c: (c, 0)),
                  pl.BlockSpec((1, d), lambda i, c: (0, 0)), pl.BlockSpec((1, d), lambda i, c: (0, 0))],
        out_specs=pl.BlockSpec((tm, d), row),
        scratch_shapes=[pltpu.VMEM((tm, d), BF16), pltpu.VMEM((tm, d), F32)],
        compiler_params=_cparams(("arbitrary", "arbitrary")),
        name="merge_out_ln",
    )(x, pm, at, wpp, wnp, wgp, wga, wo, ln_g, ln_b)


def _router_kernel(h_ref, whi_ref, wlo_ref, b_ref, e_ref, w_ref, *, n_groups, per_group):
    h = h_ref[...]
    h_hi = h.astype(BF16)
    h_lo = (h - h_hi.astype(F32)).astype(BF16)
    logits = (jnp.dot(h_hi, whi_ref[...], preferred_element_type=F32)
              + jnp.dot(h_hi, wlo_ref[...], preferred_element_type=F32)
              + jnp.dot(h_lo, whi_ref[...], preferred_element_type=F32)) + b_ref[...]
    lane = lax.broadcasted_iota(I32, logits.shape, 1)
    far = LANES

    def first_argmax(v, vmax):
        return jnp.min(jnp.where(v == vmax, lane, far), axis=-1, keepdims=True)

    gl = jnp.where(lane < n_groups, logits, NEG)
    gmax = jnp.max(gl, axis=-1, keepdims=True)
    grp = first_argmax(gl, gmax)
    gsum = jnp.sum(jnp.where(lane < n_groups, jnp.exp(gl - gmax), 0.0), axis=-1, keepdims=True)
    g_gate = 1.0 / gsum
    lo = n_groups + grp * per_group
    el = jnp.where((lane >= lo) & (lane < lo + per_group), logits, NEG)
    v1 = jnp.max(el, axis=-1, keepdims=True)
    i1 = first_argmax(el, v1)
    el2 = jnp.where(lane == i1, NEG, el)
    v2 = jnp.max(el2, axis=-1, keepdims=True)
    i2 = first_argmax(el2, v2)
    e21 = jnp.exp(v2 - v1)
    w1 = g_gate / (1.0 + e21)
    w2 = g_gate * e21 / (1.0 + e21)
    e_ref[...] = jnp.where(lane == 0, i1 - n_groups, jnp.where(lane == 1, i2 - n_groups, 0))
    w_ref[...] = jnp.where(lane == 0, w1, jnp.where(lane == 1, w2, 0.0))


def _router(h, w_hi, w_lo, bias, n_groups, per_group, tm):
    n, d = h.shape
    return pl.pallas_call(
        functools.partial(_router_kernel, n_groups=n_groups, per_group=per_group),
        out_shape=(jax.ShapeDtypeStruct((n, LANES), I32), jax.ShapeDtypeStruct((n, LANES), F32)),
        grid=(n // tm,),
        in_specs=[pl.BlockSpec((tm, d), lambda i: (i, 0)),
                  pl.BlockSpec((d, LANES), lambda i: (0, 0)), pl.BlockSpec((d, LANES), lambda i: (0, 0)),
                  pl.BlockSpec((1, LANES), lambda i: (0, 0))],
        out_specs=(pl.BlockSpec((tm, LANES), lambda i: (i, 0)), pl.BlockSpec((tm, LANES), lambda i: (i, 0))),
        compiler_params=_cparams(("arbitrary",)),
        name="moe_router",
    )(h, w_hi, w_lo, bias)


def _expert_kernel(blk0_ref, row_tok_ref, h_hbm, wg_ref, wu_ref, wd_ref, y_hbm,
                   xbuf, ybuf, wgb, wub, wdb, gsem, ysem):
    e = pl.program_id(0)
    n_exp = pl.num_programs(0)
    rows = ybuf.shape[1]
    b0 = blk0_ref[e]
    nb = blk0_ref[e + 1] - b0
    n_used = blk0_ref[n_exp]

    def row_copy(blk, slot, t, u):
        tok = row_tok_ref[blk * rows + t * SUBLANES + u]
        src = h_hbm.at[lax.shift_right_logical(tok, 3), pl.ds(tok & (SUBLANES - 1), 1)]
        return pltpu.make_async_copy(src, xbuf.at[slot, t, pl.ds(u, 1)], gsem.at[slot])

    def start_gather(blk, slot):
        def body(t, c):
            for u in range(SUBLANES):
                row_copy(blk, slot, t, u).start()
            return c
        lax.fori_loop(0, rows // SUBLANES, body, 0)

    def wait_gather(blk, slot):
        def body(t, c):
            for u in range(SUBLANES):
                row_copy(blk, slot, t, u).wait()
            return c
        lax.fori_loop(0, rows // SUBLANES, body, 0)

    def y_copy(blk, slot):
        return pltpu.make_async_copy(ybuf.at[slot], y_hbm.at[pl.ds(blk * rows, rows)], ysem.at[slot])

    @pl.when((e == 0) & (n_used > 0))
    def _():
        start_gather(0, 0)

    @pl.when(nb > 0)
    def _():
        wgb[...] = wg_ref[0].astype(BF16)
        wub[...] = wu_ref[0].astype(BF16)
        wdb[...] = wd_ref[0].astype(BF16)

    def block(j, carry):
        b = b0 + j
        slot = b % 2

        wait_gather(b, slot)
        x = xbuf[slot].reshape(rows, xbuf.shape[3]).astype(BF16)
        hgate = jnp.dot(x, wgb[...], preferred_element_type=F32)
        hup = jnp.dot(x, wub[...], preferred_element_type=F32)
        nxt = jnp.minimum(b + 1, n_used - 1)
        for t in range(rows // SUBLANES):
            for u in range(SUBLANES):
                row_copy(nxt, 1 - slot, t, u).start()
        a = (hgate * _sigmoid(hgate) * hup).astype(BF16)
        y = jnp.dot(a, wdb[...], preferred_element_type=F32)

        @pl.when(b >= 2)
        def _():
            y_copy(b - 2, slot).wait()

        ybuf[slot] = y
        y_copy(b, slot).start()
        return carry

    lax.fori_loop(0, nb, block, 0)

    @pl.when((e == n_exp - 1) & (n_used > 0))
    def _():
        wait_gather(n_used - 1, n_used % 2)

    @pl.when(e == n_exp - 1)
    def _():
        for back in (2, 1):
            @pl.when(n_used >= back)
            def _(back=back):
                y_copy(n_used - back, (n_used - back) % 2).wait()

        n_blk = y_hbm.shape[0] // rows
        ybuf[0] = jnp.zeros(ybuf.shape[1:], ybuf.dtype)

        def zero_start(blk, c):
            y_copy(blk, 0).start()
            return c

        def zero_wait(blk, c):
            y_copy(blk, 0).wait()
            return c

        lax.fori_loop(n_used, n_blk, zero_start, 0)
        lax.fori_loop(n_used, n_blk, zero_wait, 0)


def _experts(blk0, row_tok, h, w_gate, w_up, w_down, n_blk):
    n_exp = w_gate.shape[0]
    rows = EXPERT_ROWS
    d = h.shape[1]
    hid = w_gate.shape[2]
    grid_spec = pltpu.PrefetchScalarGridSpec(
        num_scalar_prefetch=2,
        grid=(n_exp,),
        in_specs=[pl.BlockSpec(memory_space=pl.ANY),
                  pl.BlockSpec((1, d, hid), lambda e, b0, rt: (e, 0, 0)),
                  pl.BlockSpec((1, d, hid), lambda e, b0, rt: (e, 0, 0)),
                  pl.BlockSpec((1, hid, d), lambda e, b0, rt: (e, 0, 0))],
        out_specs=pl.BlockSpec(memory_space=pl.ANY),
        scratch_shapes=[pltpu.VMEM((2, rows // SUBLANES, SUBLANES, d), F32), pltpu.VMEM((2, rows, d), F32),
                        pltpu.VMEM((d, hid), BF16), pltpu.VMEM((d, hid), BF16), pltpu.VMEM((hid, d), BF16),
                        pltpu.SemaphoreType.DMA((2,)), pltpu.SemaphoreType.DMA((2,))],
    )
    return pl.pallas_call(
        _expert_kernel,
        out_shape=jax.ShapeDtypeStruct((n_blk * rows, d), F32),
        grid_spec=grid_spec,
        compiler_params=_cparams(("arbitrary",)),
        name="moe_experts",
    )(blk0, row_tok, h.reshape(h.shape[0] // SUBLANES, SUBLANES, d), w_gate, w_up, w_down)


def _combine_kernel(dest_ref, h_ref, w_ref, y_hbm, g_ref, b_ref, o_ref, ybuf, sem, *, alpha):
    i = pl.program_id(0)
    n_steps = pl.num_programs(0)
    tm = h_ref.shape[0]
    slot = i % 2

    def row_copy(step_, slot_, t, u, k):
        row = dest_ref[(step_ * tm + t * SUBLANES + u) * TOP_K + k]
        src = y_hbm.at[lax.shift_right_logical(row, 3), pl.ds(row & (SUBLANES - 1), 1)]
        return pltpu.make_async_copy(src, ybuf.at[slot_, k, t, pl.ds(u, 1)], sem.at[slot_])

    def start_gather(step_, slot_):
        def body(t, c):
            for u in range(SUBLANES):
                for k in range(TOP_K):
                    row_copy(step_, slot_, t, u, k).start()
            return c
        lax.fori_loop(0, tm // SUBLANES, body, 0)

    def wait_gather(step_, slot_):
        def body(t, c):
            for u in range(SUBLANES):
                for k in range(TOP_K):
                    row_copy(step_, slot_, t, u, k).wait()
            return c
        lax.fori_loop(0, tm // SUBLANES, body, 0)

    @pl.when(i == 0)
    def _():
        start_gather(0, 0)

    wait_gather(i, slot)
    w = w_ref[...]
    y = ybuf[slot, 0].reshape(h_ref.shape) * w[:, 0:1]
    for k in range(1, TOP_K):
        y = y + ybuf[slot, k].reshape(h_ref.shape) * w[:, k:k + 1]
    nxt = jnp.minimum(i + 1, n_steps - 1)
    for t in range(tm // SUBLANES):
        for u in range(SUBLANES):
            for k in range(TOP_K):
                row_copy(nxt, 1 - slot, t, u, k).start()
    o_ref[...] = _layer_norm(alpha * h_ref[...] + y, g_ref[...], b_ref[...])

    @pl.when(i == n_steps - 1)
    def _():
        wait_gather(i, 1 - slot)


def _combine(dest, h, wts, ybuf, ln_g, ln_b, alpha, tm):
    n, d = h.shape
    grid_spec = pltpu.PrefetchScalarGridSpec(
        num_scalar_prefetch=1,
        grid=(n // tm,),
        in_specs=[pl.BlockSpec((tm, d), lambda i, ds: (i, 0)),
                  pl.BlockSpec((tm, LANES), lambda i, ds: (i, 0)),
                  pl.BlockSpec(memory_space=pl.ANY),
                  pl.BlockSpec((1, d), lambda i, ds: (0, 0)), pl.BlockSpec((1, d), lambda i, ds: (0, 0))],
        out_specs=pl.BlockSpec((tm, d), lambda i, ds: (i, 0)),
        scratch_shapes=[pltpu.VMEM((2, TOP_K, tm // SUBLANES, SUBLANES, d), F32), pltpu.SemaphoreType.DMA((2,))],
    )
    return pl.pallas_call(
        functools.partial(_combine_kernel, alpha=alpha),
        out_shape=jax.ShapeDtypeStruct((n, d), F32),
        grid_spec=grid_spec,
        compiler_params=_cparams(("arbitrary",)),
        name="moe_combine_ln",
    )(dest, h, wts, ybuf.reshape(ybuf.shape[0] // SUBLANES, SUBLANES, d), ln_g, ln_b)


def _rope_tables(s):
    inv_freq = ROPE_THETA ** (-2.0 * jnp.arange(HALF_DIM, dtype=F32) / HEAD_DIM)

    def tables(pos):
        ang = pos.astype(F32)[:, None] * inv_freq[None, :]
        return jnp.cos(ang), jnp.sin(ang)

    cos, sin = tables(jnp.arange(s))
    n_rows = s // CMP_STRIDE
    c_end = jnp.arange(n_rows) * CMP_STRIDE + (CMP_BLOCK - 1)
    cos_c, sin_c = tables(c_end)
    full = lambda c: jnp.concatenate([c, c], axis=1)
    signed = lambda sn: jnp.concatenate([-sn, sn], axis=1)
    return cos.T, sin.T, full(cos), signed(sin), full(cos_c), signed(sin_c)


def _overlap_t(s):
    n_rows = s // CMP_STRIDE
    n_cmp = (s - CMP_BLOCK) // CMP_STRIDE + 1
    n_sel = s // SEL_BLOCK
    c_start = np.arange(n_rows) * CMP_STRIDE
    s_start = np.arange(n_sel) * SEL_BLOCK
    ov = ((c_start[None, :] + CMP_BLOCK - 1 >= s_start[:, None])
          & (c_start[None, :] <= s_start[:, None] + SEL_BLOCK - 1)
          & (np.arange(n_rows)[None, :] < n_cmp))
    return jnp.asarray(ov.astype(np.float32), dtype=BF16)


def _dispatch_plan(eid, n_experts):
    n = eid.shape[0]
    m = n * TOP_K
    rows_per = EXPERT_ROWS
    flat_e = eid.reshape(-1)
    onehot = (flat_e[:, None] == jnp.arange(n_experts)[None, :]).astype(I32)
    pos = jnp.take_along_axis(jnp.cumsum(onehot, axis=0), flat_e[:, None], axis=1)[:, 0] - 1
    sizes = jnp.sum(onehot, axis=0)
    padded = (sizes + rows_per - 1) // rows_per * rows_per
    ends = jnp.cumsum(padded)
    dest = (ends - padded)[flat_e] + pos
    n_blk = -(-(m + n_experts * (rows_per - 1)) // rows_per)
    row_tok = jnp.zeros((n_blk * rows_per,), I32).at[dest].set(jnp.arange(m, dtype=I32) // TOP_K)
    blk0 = jnp.concatenate([jnp.zeros((1,), I32), ends.astype(I32) // rows_per])
    return dest.astype(I32), row_tok, blk0, n_blk


def kernel(x, w_in, pool_mix, pool_scale, w_pool_proj, w_nsa_proj, cmp_pos_k, cmp_pos_v, cmp_k_w1, cmp_k_w2,
           cmp_v_w1, cmp_v_w2, w_out, ln1_g, ln1_b, router_group_w, router_group_b, router_expert_w,
           router_expert_b, w_gate, w_up, w_down, ln2_g, ln2_b):
    b, s, d = x.shape
    n = b * s
    depth = w_in.shape[0]
    alpha = (2.0 * depth) ** 0.25
    pw = pool_mix.shape[1] * pool_mix.shape[2]
    qw = w_nsa_proj.shape[1]
    kvw = N_KV_GROUPS * HEAD_DIM
    n_groups, _, per_group = router_expert_w.shape[1:]
    n_experts = n_groups * per_group
    gate_w = 3 * (qw // HEAD_DIM)
    assert s % ATTN_TILE == 0 and WINDOW % ATTN_TILE == 0 and ATTN_TILE % SEL_BLOCK == 0
    assert gate_w <= LANES and n_groups + n_experts <= LANES and s // SEL_BLOCK <= HEAD_DIM
    c_q = pw
    c_kv = pw + qw
    c_gate = c_kv + 6 * kvw
    c_merge = c_gate + gate_w
    tn = PROJ_COL_TILE
    width_a = -(-(c_gate + LANES) // tn) * tn
    assert c_gate % LANES == 0

    cos_t, sin_t, cos_f, sin_s, cos_c, sin_c = _rope_tables(s)
    ov_t = _overlap_t(s)
    assert c_kv % kvw == 0

    h = x.reshape(n, d)
    for l in range(depth):
        wl = w_in[l]
        w_a = jnp.pad(wl[:, :c_merge], ((0, 0), (0, width_a - c_merge))).astype(BF16)
        w_gp = wl[:, c_merge:c_merge + d].astype(BF16)
        w_ga = wl[:, c_merge + d:].astype(BF16)

        proj = _project(h, w_a, _tile(n, 1024), tn)

        mixed = _pool_mixer(proj, pool_mix[l].astype(BF16), pool_scale[l].reshape(1, pw), b, s, pw, _tile(s, 512))

        kv_blk = c_kv // kvw
        kcmp, vcmp_t = _compress(
            proj, kv_blk, b, s,
            cmp_pos_k[l].reshape(1, CMP_BLOCK * HEAD_DIM), cmp_pos_v[l].reshape(1, CMP_BLOCK * HEAD_DIM),
            cmp_k_w1[l].astype(BF16), cmp_k_w2[l].astype(BF16), cmp_v_w1[l].astype(BF16),
            cmp_v_w2[l].T.astype(BF16), cos_c, sin_c)
        ksx, vs_t, kwx, vw_t = _kv_prep(proj, kv_blk, b, s, cos_f, sin_s)

        attn = _attention(proj, cos_t, sin_t, kcmp, vcmp_t, ksx, vs_t, kwx, vw_t, ov_t,
                          b=b, s=s, pw=pw, qw=qw, gate_col=c_gate)

        h = _merge_out(h, mixed, attn, w_pool_proj[l].astype(BF16), w_nsa_proj[l].astype(BF16), w_gp, w_ga,
                       w_out[l].astype(BF16), ln1_g[l].reshape(1, d), ln1_b[l].reshape(1, d), alpha,
                       _tile(n, 512), _tile(d, 512))

        w_r = jnp.concatenate([router_group_w[l], router_expert_w[l].transpose(1, 0, 2).reshape(d, n_experts)], axis=1)
        w_r = jnp.pad(w_r, ((0, 0), (0, LANES - w_r.shape[1])))
        w_r_hi = w_r.astype(BF16)
        w_r_lo = (w_r - w_r_hi.astype(F32)).astype(BF16)
        b_r = jnp.pad(jnp.concatenate([router_group_b[l], router_expert_b[l].reshape(-1)]),
                      (0, LANES - n_groups - n_experts)).reshape(1, LANES)
        eid_l, wts_l = _router(h, w_r_hi, w_r_lo, b_r, n_groups, per_group, _tile(n, 512))
        dest, row_tok, blk0, n_blk = _dispatch_plan(eid_l[:, :TOP_K], n_experts)

        ybuf = _experts(blk0, row_tok, h, w_gate[l], w_up[l], w_down[l], n_blk)
        h = _combine(dest, h, wts_l, ybuf, ln2_g[l].reshape(1, d), ln2_b[l].reshape(1, d), alpha, _tile(n, 256))
    return h.reshape(b, s, d)
```
